```python
import math
import jax, jax.numpy as jnp
from jax import lax
import numpy as np

D_MODEL = 4096
BATCH = 2
SEQ = 4096
DEPTH = 2

EPS = 1e-6
NEG_BIG = -1e30
BRANCH_WIDTH = D_MODEL // 4
N_BRANCHES = 3

A_DK = 128
A_DV = 128
A_HEADS = BRANCH_WIDTH // A_DK
A_WIDTH = A_HEADS * A_DK
A_CHUNK = 64
A_COLS = 4 * A_WIDTH

B_HD = 128
B_HEADS = BRANCH_WIDTH // B_HD
B_WIDTH = B_HEADS * B_HD
B_PATTERNS = ((128, 1), (512, 4), (2048, 16))
B_GROUPS = len(B_PATTERNS)
B_BLOCK = 128
B_COLS = B_GROUPS * 3 * B_WIDTH
ROPE_THETA = 500000.0
ROT_DIM = B_HD // 4
ROT_HALF = ROT_DIM // 2

C_WIDTH = BRANCH_WIDTH
C_GROUPS = 8
C_GROUP_CH = C_WIDTH // C_GROUPS
C_CHUNK = 128
C_COLS = 2 * C_WIDTH

IN_COLS = A_COLS + B_COLS + C_COLS

D_FF = 3 * D_MODEL
CONV_W = 3

kernel_name = "hybrid_hgrn2_dilated_spatialgate_block"


def rms_norm(x, g):
    x32 = x.astype(jnp.float32)
    return x32 * lax.rsqrt(jnp.mean(x32 * x32, axis=-1, keepdims=True) + EPS) * g.astype(jnp.float32)


def partial_rope(x, positions):
    inv = jnp.power(jnp.float32(ROPE_THETA), -jnp.arange(0, ROT_DIM, 2, dtype=jnp.float32) / ROT_DIM)
    ang = positions.astype(jnp.float32)[..., None] * inv
    cos = jnp.cos(ang)[:, :, None, :]
    sin = jnp.sin(ang)[:, :, None, :]
    x1 = x[..., :ROT_HALF]
    x2 = x[..., ROT_HALF:ROT_DIM]
    return jnp.concatenate([x1 * cos - x2 * sin, x2 * cos + x1 * sin, x[..., ROT_DIM:]], axis=-1)


def hgrn2(q, f_logit, i_in, g, lb, out_gain):
    f32 = jnp.float32
    Bn, S, _ = q.shape
    N = S // A_CHUNK
    lbf = lb.astype(f32)
    f = lbf + (1.0 - lbf) * jax.nn.sigmoid(f_logit.astype(f32))
    log_f = jnp.log(f)
    k = 1.0 - f

    def chunks(t, d):
        return t.astype(f32).reshape(Bn, N, A_CHUNK, A_HEADS, d).transpose(1, 0, 3, 2, 4)

    qc = chunks(q, A_DK) * (A_DK ** -0.5)
    kc = chunks(k, A_DK)
    vc = chunks(i_in, A_DV)
    lfc = chunks(log_f, A_DK)
    causal = jnp.tril(jnp.ones((A_CHUNK, A_CHUNK), dtype=bool))

    def step(state, inp):
        qt, kt, vt, lft = inp
        cum = jnp.cumsum(lft, axis=2)
        o_inter = jnp.einsum('bhtk,bhkv->bhtv', qt * jnp.exp(cum), state)
        rel = jnp.where(causal[None, None, :, :, None],
                        cum[:, :, :, None, :] - cum[:, :, None, :, :], NEG_BIG)
        scores = jnp.einsum('bhtk,bhsk,bhtsk->bhts', qt, kt, jnp.exp(rel))
        o_intra = jnp.einsum('bhts,bhsv->bhtv', scores, vt)
        last = cum[:, :, -1:, :]
        new_state = (jnp.exp(last[:, :, 0, :, None]) * state
                     + jnp.einsum('bhsk,bhsv->bhkv', kt * jnp.exp(last - cum), vt))
        return new_state, o_inter + o_intra

    s0 = jnp.zeros((Bn, A_HEADS, A_DK, A_DV), f32)
    _, o = lax.scan(step, s0, (qc, kc, vc, lfc))
    o = o.transpose(1, 0, 3, 2, 4).reshape(Bn, S, A_HEADS, A_DV)
    o = rms_norm(o, out_gain).reshape(Bn, S, A_HEADS * A_DV) * jax.nn.silu(g.astype(f32))
    return o


def dilated_group(q, k, v, dil, n_back):
    Bn, S, H, hd = q.shape
    L = S // dil
    nb = -(-L // B_BLOCK)
    Lp = nb * B_BLOCK

    def to_res(t):
        t = t.reshape(Bn, L, dil, H, hd).transpose(0, 2, 3, 1, 4)
        t = jnp.pad(t, ((0, 0), (0, 0), (0, 0), (0, Lp - L), (0, 0)))
        return t.reshape(Bn, dil, H, nb, B_BLOCK, hd)

    def with_prev(t):
        prev = jnp.pad(t, ((0, 0), (0, 0), (0, 0), (1, 0), (0, 0), (0, 0)))[:, :, :, :-1]
        return jnp.concatenate([prev, t], axis=4)

    qb = to_res(q)
    kk = with_prev(to_res(k))
    vv = with_prev(to_res(v))
    s = jnp.einsum('brhnqe,brhnke->brhnqk', qb, kk) * (hd ** -0.5)
    qi = jnp.arange(B_BLOCK)[:, None] + B_BLOCK
    ki = jnp.arange(2 * B_BLOCK)[None, :]
    dist = qi - ki
    blk = jnp.arange(nb)[:, None, None]
    valid = (dist >= 0) & (dist <= n_back) & (blk * B_BLOCK + ki - B_BLOCK >= 0)
    s = jnp.where(valid, s, NEG_BIG)
    lse = jax.nn.logsumexp(s, axis=-1)
    p = jnp.exp(s - lse[..., None])
    o = jnp.einsum('brhnqk,brhnke->brhnqe', p, vv)

    def from_res(t):
        tail = t.shape[5:]
        t = t.reshape((Bn, dil, H, Lp) + tail)[:, :, :, :L]
        t = t.transpose((0, 3, 1, 2) + tuple(range(4, t.ndim)))
        return t.reshape((Bn, S, H) + tail)

    return from_res(o), from_res(lse)


def dilated_mixer(proj_b, q_gain, k_gain, positions):
    Bn, S, _ = proj_b.shape
    parts = proj_b.astype(jnp.float32).reshape(Bn, S, B_GROUPS, 3, B_HEADS, B_HD)
    outs = []
    lses = []
    for gi, (win, dil) in enumerate(B_PATTERNS):
        q = partial_rope(rms_norm(parts[:, :, gi, 0], q_gain[gi]), positions)
        k = partial_rope(rms_norm(parts[:, :, gi, 1], k_gain[gi]), positions)
        o, lse = dilated_group(q, k, parts[:, :, gi, 2], dil, win // dil)
        outs.append(o)
        lses.append(lse)
    w = jax.nn.softmax(jnp.stack(lses, axis=0), axis=0)
    o = jnp.sum(w[..., None] * jnp.stack(outs, axis=0), axis=0)
    return o.reshape(Bn, S, B_WIDTH)


def spatial_gating(proj_c, ln_g, ln_b, w_s, b_s):
    f32 = jnp.float32
    Bn, S, _ = proj_c.shape
    z = jax.nn.gelu(proj_c.astype(f32), approximate=False)
    u = z[..., :C_WIDTH]
    v = z[..., C_WIDTH:]
    mu = jnp.mean(v, axis=-1, keepdims=True)
    var = jnp.mean(jnp.square(v - mu), axis=-1, keepdims=True)
    v = (v - mu) * lax.rsqrt(var + EPS) * ln_g.astype(f32) + ln_b.astype(f32)
    N = S // C_CHUNK
    v = v.reshape(Bn, N, C_CHUNK, C_GROUPS, C_GROUP_CH)
    mask = jnp.tril(jnp.ones((C_CHUNK, C_CHUNK), f32))
    mixed = (jnp.einsum('gts,bnsgc->bntgc', w_s.astype(f32) * mask, v)
             + b_s.astype(f32).T[None, None, :, :, None])
    return u * mixed.reshape(Bn, S, C_WIDTH)


def setup_inputs(seed: int = 0) -> dict:
    key = jax.random.key(seed)
    ks = jax.random.split(key, 20)
    f32 = jnp.float32
    nrm = lambda k, shape, scale: jax.random.normal(k, shape, f32) * scale
    return {
        "x": nrm(ks[0], (BATCH, SEQ, D_MODEL), 1.0),
        "positions": jnp.tile(jnp.arange(SEQ, dtype=jnp.int32)[None, :], (BATCH, 1)),
        "norm_mix": 1.0 + nrm(ks[1], (DEPTH, D_MODEL), 0.02),
        "w_in": nrm(ks[2], (DEPTH, D_MODEL, IN_COLS), D_MODEL ** -0.5),
        "hgrn_lower_bounds": nrm(ks[3], (DEPTH, A_WIDTH), 0.5),
        "hgrn_out_norm": 1.0 + nrm(ks[4], (DEPTH, A_DV), 0.02),
        "q_norm": 1.0 + nrm(ks[5], (DEPTH, B_GROUPS, B_HD), 0.02),
        "k_norm": 1.0 + nrm(ks[6], (DEPTH, B_GROUPS, B_HD), 0.02),
        "sg_ln_g": 1.0 + nrm(ks[7], (DEPTH, C_WIDTH), 0.02),
        "sg_ln_b": nrm(ks[8], (DEPTH, C_WIDTH), 0.02),
        "sg_w": nrm(ks[9], (DEPTH, C_GROUPS, C_CHUNK, C_CHUNK), C_CHUNK ** -0.5),
        "sg_b": 1.0 + nrm(ks[10], (DEPTH, C_GROUPS, C_CHUNK), 0.02),
        "w_gate": nrm(ks[11], (DEPTH, D_MODEL, N_BRANCHES * D_MODEL), D_MODEL ** -0.5),
        "w_branch": nrm(ks[12], (DEPTH, N_BRANCHES, BRANCH_WIDTH, D_MODEL), BRANCH_WIDTH ** -0.5),
        "w_out": nrm(ks[13], (DEPTH, D_MODEL, D_MODEL), D_MODEL ** -0.5),
        "norm_ffn": 1.0 + nrm(ks[14], (DEPTH, D_MODEL), 0.02),
        "w_up": nrm(ks[15], (DEPTH, D_MODEL, 2 * D_FF), D_MODEL ** -0.5),
        "ffn_conv_w": nrm(ks[16], (DEPTH, CONV_W, D_FF), CONV_W ** -0.5),
        "ffn_conv_b": nrm(ks[17], (DEPTH, D_FF), 0.02),
        "w_down": nrm(ks[18], (DEPTH, D_FF, D_MODEL), D_FF ** -0.5),
    }


def reference(x, positions, norm_mix, w_in, hgrn_lower_bounds, hgrn_out_norm, q_norm, k_norm,
              sg_ln_g, sg_ln_b, sg_w, sg_b, w_gate, w_branch, w_out, norm_ffn, w_up,
              ffn_conv_w, ffn_conv_b, w_down):
    Bn, S, D = x.shape
    sm = jax.nn.softmax(hgrn_lower_bounds.astype(jnp.float32), axis=0)
    lower_bounds = jnp.cumsum(sm, axis=0) - sm[0:1]
    for l in range(DEPTH):
        xn = rms_norm(x, norm_mix[l]).astype(x.dtype)
        proj = xn @ w_in[l]
        pa = proj[..., :A_COLS]
        pb = proj[..., A_COLS:A_COLS + B_COLS]
        pc = proj[..., A_COLS + B_COLS:]
        qa, fa, ia, ga = jnp.split(pa, 4, axis=-1)
        ya = hgrn2(qa, fa, ia, ga, lower_bounds[l], hgrn_out_norm[l])
        yb = dilated_mixer(pb, q_norm[l], k_norm[l], positions)
        yc = spatial_gating(pc, sg_ln_g[l], sg_ln_b[l], sg_w[l], sg_b[l])
        ys = jnp.stack([ya, yb, yc], axis=2).astype(x.dtype)
        branches = jnp.einsum('bsiw,iwd->bsid', ys, w_branch[l])
        gates = jax.nn.sigmoid((xn @ w_gate[l]).reshape(Bn, S, N_BRANCHES, D))
        merged = jnp.sum(gates * branches, axis=2)
        x = x + merged @ w_out[l]
        xn = rms_norm(x, norm_ffn[l]).astype(x.dtype)
        up = xn @ w_up[l]
        gate_h = up[..., :D_FF]
        val_h = up[..., D_FF:]
        gp = jnp.pad(gate_h, ((0, 0), (CONV_W - 1, 0), (0, 0)))
        cw = ffn_conv_w[l]
        conv = (cw[0] * gp[:, 0:S] + cw[1] * gp[:, 1:S + 1] + cw[2] * gp[:, 2:S + 2]
                + ffn_conv_b[l])
        x = x + (jax.nn.silu(conv) * val_h) @ w_down[l]
    return x
```

```python
import functools
import math

import jax
import jax.numpy as jnp
from jax import lax
from jax.experimental import pallas as pl
from jax.experimental.pallas import tpu as pltpu

F32 = jnp.float32
BF16 = jnp.bfloat16

EPS = 1e-6
NEG_BIG = -1e30

LANES = 128
VMEM_LIMIT = 56 * 1024 * 1024

HEAD_DIM = 128
N_HEADS = 8
BRANCH_WIDTH = N_HEADS * HEAD_DIM
N_BRANCHES = 3

A_CHUNK = 64
A_SUB = 16
A_STEP_ROWS = 512

B_PATTERNS = ((128, 1), (512, 4), (2048, 16))
B_BLOCK = 128
ROPE_THETA = 500000.0
ROT_DIM = HEAD_DIM // 4
ROT_HALF = ROT_DIM // 2

C_CHUNK = 128
CONV_W = 3


def _params(*sem):
    return pltpu.CompilerParams(dimension_semantics=sem, vmem_limit_bytes=VMEM_LIMIT)


def _dot(a, b):
    return jnp.dot(a, b, preferred_element_type=F32)


def _dot_nt(a, b):
    return lax.dot_general(a, b, (((1,), (1,)), ((), ())), preferred_element_type=F32)


def _dot_tn(a, b):
    return lax.dot_general(a, b, (((0,), (0,)), ((), ())), preferred_element_type=F32)


def _rmsnorm_kernel(x_ref, g_ref, o_ref):
    x = x_ref[...]
    ms = jnp.mean(x * x, axis=-1, keepdims=True)
    o_ref[...] = (x * lax.rsqrt(ms + EPS) * g_ref[...]).astype(o_ref.dtype)


def rmsnorm(x, g, *, rows=256):
    m, d = x.shape
    return pl.pallas_call(
        _rmsnorm_kernel,
        grid=(m // rows,),
        in_specs=[pl.BlockSpec((rows, d), lambda i: (i, 0)),
                  pl.BlockSpec((1, d), lambda i: (0, 0))],
        out_specs=pl.BlockSpec((rows, d), lambda i: (i, 0)),
        out_shape=jax.ShapeDtypeStruct((m, d), BF16),
        compiler_params=_params("parallel"),
        name="rmsnorm",
    )(x, g.reshape(1, d))


def _matmul_kernel(x_ref, w_ref, o_ref):
    o_ref[...] = _dot(x_ref[...], w_ref[...]).astype(o_ref.dtype)


def matmul(x, w, *, tm, tn, out_dtype=F32):
    m, k = x.shape
    n = w.shape[1]
    return pl.pallas_call(
        _matmul_kernel,
        grid=(m // tm, n // tn),
        in_specs=[pl.BlockSpec((tm, k), lambda i, j: (i, 0)),
                  pl.BlockSpec((k, tn), lambda i, j: (0, j))],
        out_specs=pl.BlockSpec((tm, tn), lambda i, j: (i, j)),
        out_shape=jax.ShapeDtypeStruct((m, n), out_dtype),
        compiler_params=_params("parallel", "parallel"),
        name="proj_matmul",
    )(x, w)


def _matmul_res_kernel(x_ref, w_ref, r_ref, o_ref):
    k = pl.program_id(2)

    @pl.when(k == 0)
    def _():
        o_ref[...] = r_ref[...]

    o_ref[...] += _dot(x_ref[...], w_ref[...])


def matmul_residual(x, w, res, *, tm, tn, tk):
    m, k = x.shape
    n = w.shape[1]
    return pl.pallas_call(
        _matmul_res_kernel,
        grid=(m // tm, n // tn, k // tk),
        in_specs=[pl.BlockSpec((tm, tk), lambda i, j, kk: (i, kk)),
                  pl.BlockSpec((tk, tn), lambda i, j, kk: (kk, j)),
                  pl.BlockSpec((tm, tn), lambda i, j, kk: (i, j))],
        out_specs=pl.BlockSpec((tm, tn), lambda i, j, kk: (i, j)),
        out_shape=jax.ShapeDtypeStruct((m, n), F32),
        compiler_params=_params("parallel", "parallel", "arbitrary"),
        name="matmul_residual",
    )(x, w, res)


def _hgrn_kernel(lbp_ref, gain_ref, q_ref, f_ref, i_ref, g_ref, o_ref, st_ref, kk_s, cum_s,
                 *, layer, n_chunks):
    n = pl.program_id(2)

    @pl.when(n == 0)
    def _():
        st_ref[...] = jnp.zeros_like(st_ref)

    lbp = lbp_ref[...]
    e = jnp.exp(lbp - jnp.max(lbp, axis=0, keepdims=True))
    sm = e / jnp.sum(e, axis=0, keepdims=True)
    cs = sm[0:1]
    for li in range(1, layer + 1):
        cs = cs + sm[li:li + 1]
    lb = cs - sm[0:1]

    gain = gain_ref[...]
    n_sub = A_CHUNK // A_SUB
    row = lax.broadcasted_iota(jnp.int32, (A_CHUNK, A_CHUNK), 0)
    col = lax.broadcasted_iota(jnp.int32, (A_CHUNK, A_CHUNK), 1)
    tri = (row >= col).astype(F32)
    off_mask = (row // A_SUB) > (col // A_SUB)
    rblk = lax.broadcasted_iota(jnp.int32, (A_CHUNK, HEAD_DIM), 0) // A_SUB
    sub_row = lax.broadcasted_iota(jnp.int32, (A_SUB, HEAD_DIM), 0)
    scale = HEAD_DIM ** -0.5

    def chunk(c, carry):
        r0 = pl.multiple_of(c * A_CHUNK, A_CHUNK)
        sl = pl.ds(r0, A_CHUNK)
        q = q_ref[sl, :] * scale
        v = i_ref[sl, :]
        f = lb + (1.0 - lb) * jax.nn.sigmoid(f_ref[sl, :])
        kk = 1.0 - f
        cum = jnp.dot(tri, jnp.log(f), precision=lax.Precision.HIGHEST,
                      preferred_element_type=F32)
        ends = [cum[(j + 1) * A_SUB - 1:(j + 1) * A_SUB, :] for j in range(n_sub)]
        last = ends[-1]
        eblk = jnp.concatenate([jnp.broadcast_to(ej, (A_SUB, HEAD_DIM)) for ej in ends], axis=0)
        khat = kk * jnp.exp(eblk - cum)
        st = st_ref[...]

        o = _dot_nt((q * jnp.exp(cum)).astype(BF16), st.astype(BF16))

        s_off = jnp.zeros((A_CHUNK, A_CHUNK), F32)
        for j in range(n_sub - 1):
            qj = q * jnp.exp(jnp.minimum(cum - ends[j], 0.0))
            kj = jnp.where(rblk == j, khat, 0.0)
            s_off = s_off + _dot_nt(qj.astype(BF16), kj.astype(BF16))
        s_off = jnp.where(off_mask, s_off, 0.0)
        o = o + _dot(s_off.astype(BF16), v.astype(BF16))

        kk_s[...] = kk
        cum_s[...] = cum
        diag = []
        for b in range(n_sub):
            qb = q[b * A_SUB:(b + 1) * A_SUB]
            cb = cum[b * A_SUB:(b + 1) * A_SUB]
            ob = jnp.zeros((A_SUB, HEAD_DIM), F32)
            for s in range(A_SUB):
                r = b * A_SUB + s
                k_row = kk_s[r:r + 1, :]
                c_row = cum_s[r:r + 1, :]
                v_row = i_ref[pl.ds(r0 + r, 1), :]
                dec = jnp.exp(jnp.where(sub_row >= s, cb - c_row, NEG_BIG))
                w = jnp.sum(qb * k_row * dec, axis=-1, keepdims=True)
                ob = ob + w * v_row
            diag.append(ob)
        o = o + jnp.concatenate(diag, axis=0)

        ktil = khat * jnp.exp(last - eblk)
        st_ref[...] = st * jnp.exp(last) + _dot_tn(v.astype(BF16), ktil.astype(BF16))

        ms = jnp.mean(o * o, axis=-1, keepdims=True)
        on = o * lax.rsqrt(ms + EPS) * gain
        g = g_ref[sl, :]
        o_ref[sl, :] = (on * (g * jax.nn.sigmoid(g))).astype(o_ref.dtype)
        return carry

    lax.fori_loop(0, n_chunks, chunk, 0)


def hgrn2_mixer(proj3, lower_bounds, out_gain, layer):
    bsz, s, _ = proj3.shape
    depth = lower_bounds.shape[0]
    ts = min(A_STEP_ROWS, s)
    col = lambda part: (lambda b, h, n: (b, n, part * N_HEADS + h))
    blk = lambda part: pl.BlockSpec((None, ts, HEAD_DIM), col(part))
    return pl.pallas_call(
        functools.partial(_hgrn_kernel, layer=layer, n_chunks=ts // A_CHUNK),
        grid=(bsz, N_HEADS, s // ts),
        in_specs=[pl.BlockSpec((depth, HEAD_DIM), lambda b, h, n: (0, h)),
                  pl.BlockSpec((1, HEAD_DIM), lambda b, h, n: (0, 0)),
                  blk(0), blk(1), blk(2), blk(3)],
        out_specs=pl.BlockSpec((None, ts, HEAD_DIM), lambda b, h, n: (b, n, h)),
        out_shape=jax.ShapeDtypeStruct((bsz, s, BRANCH_WIDTH), BF16),
        scratch_shapes=[pltpu.VMEM((HEAD_DIM, HEAD_DIM), F32),
                        pltpu.VMEM((A_CHUNK, HEAD_DIM), F32),
                        pltpu.VMEM((A_CHUNK, HEAD_DIM), F32)],
        compiler_params=_params("parallel", "parallel", "arbitrary"),
        name="hgrn2",
    )(lower_bounds, out_gain.reshape(1, HEAD_DIM), proj3, proj3, proj3, proj3)


def _rope_tables(positions):
    inv = jnp.power(jnp.float32(ROPE_THETA), -jnp.arange(0, ROT_DIM, 2, dtype=F32) / ROT_DIM)
    ang = positions.astype(F32)[..., None] * inv
    cos = jnp.cos(ang)
    sin = jnp.sin(ang)
    zero = jnp.zeros_like(sin)
    pad = jnp.zeros(ang.shape[:-1] + (HEAD_DIM - ROT_DIM,), F32)
    cos_f = jnp.concatenate([cos, cos, pad + 1.0], axis=-1)
    sin_lo = jnp.concatenate([-sin, zero, pad], axis=-1)
    sin_hi = jnp.concatenate([zero, sin, pad], axis=-1)
    return cos_f, sin_lo, sin_hi


def _dilated_kernel(qg_ref, kg_ref, cos_ref, slo_ref, shi_ref, q_ref, k_ref, v_ref,
                    o_ref, lse_ref, kprev, vprev):
    n = pl.program_id(2)

    @pl.when(n == 0)
    def _():
        kprev[...] = jnp.zeros_like(kprev)
        vprev[...] = jnp.zeros_like(vprev)

    cos = cos_ref[...]
    slo = slo_ref[...]
    shi = shi_ref[...]
    qg = qg_ref[...]
    kg = kg_ref[...]
    qi = lax.broadcasted_iota(jnp.int32, (B_BLOCK, B_BLOCK), 0)
    ki = lax.broadcasted_iota(jnp.int32, (B_BLOCK, B_BLOCK), 1)
    cur_ok = ki <= qi
    prev_ok = jnp.logical_and(ki >= qi, n > 0)
    scale = HEAD_DIM ** -0.5

    def norm_rope(x, gain):
        ms = jnp.mean(x * x, axis=-1, keepdims=True)
        xn = x * lax.rsqrt(ms + EPS) * gain
        return (xn * cos + pltpu.roll(xn, HEAD_DIM - ROT_HALF, 1) * slo
                + pltpu.roll(xn, ROT_HALF, 1) * shi)

    for h in range(N_HEADS):
        hs = slice(h * HEAD_DIM, (h + 1) * HEAD_DIM)
        q = norm_rope(q_ref[:, hs], qg).astype(BF16)
        k = norm_rope(k_ref[:, hs], kg).astype(BF16)
        v = v_ref[:, hs].astype(BF16)
        kp = kprev[:, hs]
        vp = vprev[:, hs]
        s_cur = jnp.where(cur_ok, _dot_nt(q, k) * scale, NEG_BIG)
        s_prev = jnp.where(prev_ok, _dot_nt(q, kp) * scale, NEG_BIG)
        m = jnp.maximum(jnp.max(s_cur, axis=-1, keepdims=True),
                        jnp.max(s_prev, axis=-1, keepdims=True))
        p_cur = jnp.exp(s_cur - m)
        p_prev = jnp.exp(s_prev - m)
        l = jnp.sum(p_cur, axis=-1, keepdims=True) + jnp.sum(p_prev, axis=-1, keepdims=True)
        acc = _dot(p_cur.astype(BF16), v) + _dot(p_prev.astype(BF16), vp)
        o_ref[:, hs] = acc / l
        lse_ref[:, hs] = jnp.broadcast_to(m + jnp.log(l), (B_BLOCK, HEAD_DIM))
        kprev[:, hs] = k
        vprev[:, hs] = v


def dilated_group(proj3, tables, q_gain, k_gain, group, dil, col0):
    bsz, s, cols = proj3.shape
    ln = s // dil
    view = proj3.reshape(bsz, ln, dil * cols)
    tabs = [t.reshape(bsz, ln, dil * HEAD_DIM) for t in tables]
    per_row = cols // BRANCH_WIDTH
    base = col0 // BRANCH_WIDTH + 3 * group

    def part(p):
        return pl.BlockSpec((None, B_BLOCK, BRANCH_WIDTH),
                            lambda b, r, n: (b, n, r * per_row + base + p))

    tab = pl.BlockSpec((None, B_BLOCK, HEAD_DIM), lambda b, r, n: (b, n, r))
    gain = pl.BlockSpec((1, HEAD_DIM), lambda b, r, n: (0, 0))
    out = pl.BlockSpec((None, B_BLOCK, BRANCH_WIDTH), lambda b, r, n: (b, n, r))
    o, lse = pl.pallas_call(
        _dilated_kernel,
        grid=(bsz, dil, ln // B_BLOCK),
        in_specs=[gain, gain, tab, tab, tab, part(0), part(1), part(2)],
        out_specs=[out, out],
        out_shape=[jax.ShapeDtypeStruct((bsz, ln, dil * BRANCH_WIDTH), F32)] * 2,
        scratch_shapes=[pltpu.VMEM((B_BLOCK, BRANCH_WIDTH), BF16),
                        pltpu.VMEM((B_BLOCK, BRANCH_WIDTH), BF16)],
        compiler_params=_params("parallel", "parallel", "arbitrary"),
        name=f"dilated_attn_g{group}",
    )(q_gain.reshape(1, HEAD_DIM), k_gain.reshape(1, HEAD_DIM), *tabs, view, view, view)
    return o.reshape(bsz, s, BRANCH_WIDTH), lse.reshape(bsz, s, BRANCH_WIDTH)


def _combine_kernel(o0, o1, o2, l0, l1, l2, y_ref):
    a, b, c = l0[...], l1[...], l2[...]
    m = jnp.maximum(jnp.maximum(a, b), c)
    ea, eb, ec = jnp.exp(a - m), jnp.exp(b - m), jnp.exp(c - m)
    den = ea + eb + ec
    y = (ea / den) * o0[...] + (eb / den) * o1[...] + (ec / den) * o2[...]
    y_ref[...] = y.astype(y_ref.dtype)


def combine_groups(outs, lses, *, rows=512):
    bsz, s, w = outs[0].shape
    m = bsz * s
    spec = pl.BlockSpec((rows, w), lambda i: (i, 0))
    flat = [t.reshape(m, w) for t in list(outs) + list(lses)]
    return pl.pallas_call(
        _combine_kernel,
        grid=(m // rows,),
        in_specs=[spec] * 6,
        out_specs=spec,
        out_shape=jax.ShapeDtypeStruct((m, w), BF16),
        compiler_params=_params("parallel"),
        name="combine_groups",
    )(*flat)


def _spatial_kernel(u_ref, v_ref, lng_ref, lnb_ref, w_ref, b_ref, o_ref):
    inv_sqrt2 = 1.0 / math.sqrt(2.0)

    def gelu(x):
        return 0.5 * x * (1.0 + lax.erf(x * inv_sqrt2))

    v = gelu(v_ref[...])
    mu = jnp.mean(v, axis=-1, keepdims=True)
    vc = v - mu
    var = jnp.mean(vc * vc, axis=-1, keepdims=True)
    vn = vc * lax.rsqrt(var + EPS) * lng_ref[...] + lnb_ref[...]
    row = lax.broadcasted_iota(jnp.int32, (C_CHUNK, C_CHUNK), 0)
    col = lax.broadcasted_iota(jnp.int32, (C_CHUNK, C_CHUNK), 1)
    causal = row >= col
    bias = b_ref[...]
    for g in range(N_HEADS):
        gs = slice(g * HEAD_DIM, (g + 1) * HEAD_DIM)
        wm = jnp.where(causal, w_ref[g], 0.0).astype(BF16)
        mixed = _dot(wm, vn[:, gs].astype(BF16)) + bias[:, g:g + 1]
        o_ref[:, gs] = (gelu(u_ref[:, gs]) * mixed).astype(o_ref.dtype)


def spatial_gating(proj, ln_g, ln_b, w_s, b_s, col0):
    m, _ = proj.shape
    ublk = col0 // BRANCH_WIDTH
    return pl.pallas_call(
        _spatial_kernel,
        grid=(m // C_CHUNK,),
        in_specs=[pl.BlockSpec((C_CHUNK, BRANCH_WIDTH), lambda i: (i, ublk)),
                  pl.BlockSpec((C_CHUNK, BRANCH_WIDTH), lambda i: (i, ublk + 1)),
                  pl.BlockSpec((1, BRANCH_WIDTH), lambda i: (0, 0)),
                  pl.BlockSpec((1, BRANCH_WIDTH), lambda i: (0, 0)),
                  pl.BlockSpec((N_HEADS, C_CHUNK, C_CHUNK), lambda i: (0, 0, 0)),
                  pl.BlockSpec((C_CHUNK, N_HEADS), lambda i: (0, 0))],
        out_specs=pl.BlockSpec((C_CHUNK, BRANCH_WIDTH), lambda i: (i, 0)),
        out_shape=jax.ShapeDtypeStruct((m, BRANCH_WIDTH), BF16),
        compiler_params=_params("parallel"),
        name="spatial_gating",
    )(proj, proj, ln_g.reshape(1, -1), ln_b.reshape(1, -1), w_s, b_s.T)


def _merge_kernel(xn_ref, y_ref, wg_ref, wb_ref, o_ref, acc_ref):
    i = pl.program_id(2)
    gate = jax.nn.sigmoid(_dot(xn_ref[...], wg_ref[...]))
    term = gate * _dot(y_ref[...], wb_ref[...])

    @pl.when(i == 0)
    def _():
        acc_ref[...] = term

    @pl.when(i > 0)
    def _():
        acc_ref[...] += term

    @pl.when(i == N_BRANCHES - 1)
    def _():
        o_ref[...] = acc_ref[...].astype(o_ref.dtype)


def gated_merge(xn, ys, w_gate, w_branch, *, tm, tn):
    m, d = xn.shape
    w = ys.shape[2]
    nblk = d // tn
    return pl.pallas_call(
        _merge_kernel,
        grid=(m // tm, nblk, N_BRANCHES),
        in_specs=[pl.BlockSpec((tm, d), lambda a, b, i: (a, 0)),
                  pl.BlockSpec((None, tm, w), lambda a, b, i: (i, a, 0)),
                  pl.BlockSpec((d, tn), lambda a, b, i: (0, i * nblk + b)),
                  pl.BlockSpec((None, w, tn), lambda a, b, i: (i, 0, b))],
        out_specs=pl.BlockSpec((tm, tn), lambda a, b, i: (a, b)),
        out_shape=jax.ShapeDtypeStruct((m, d), BF16),
        scratch_shapes=[pltpu.VMEM((tm, tn), F32)],
        compiler_params=_params("parallel", "parallel", "arbitrary"),
        name="gated_merge",
    )(xn, ys, w_gate, w_branch)


def _ffn_up_kernel(x_ref, wg_ref, wv_ref, cw_ref, cb_ref, o_ref, g_s, *, tiles_per_seq):
    i = pl.program_id(1)
    tm = x_ref.shape[0]
    halo = 8

    @pl.when(i % tiles_per_seq == 0)
    def _():
        g_s[0:halo, :] = jnp.zeros((halo, g_s.shape[1]), F32)

    @pl.when(i % tiles_per_seq != 0)
    def _():
        g_s[0:halo, :] = g_s[tm:tm + halo, :]

    x = x_ref[...]
    g_s[halo:tm + halo, :] = _dot(x, wg_ref[...])
    val = _dot(x, wv_ref[...])
    cw = cw_ref[...]
    conv = (cw[0:1] * g_s[halo - 2:tm + halo - 2, :] + cw[1:2] * g_s[halo - 1:tm + halo - 1, :]
            + cw[2:3] * g_s[halo:tm + halo, :] + cb_ref[...])
    o_ref[...] = (conv * jax.nn.sigmoid(conv) * val).astype(o_ref.dtype)


def ffn_up(xn, w_up, conv_w, conv_b, seq_len, *, tm, tn):
    m, d = xn.shape
    dff = conv_w.shape[1]
    nblk = dff // tn
    return pl.pallas_call(
        functools.partial(_ffn_up_kernel, tiles_per_seq=seq_len // tm),
        grid=(nblk, m // tm),
        in_specs=[pl.BlockSpec((tm, d), lambda j, i: (i, 0)),
                  pl.BlockSpec((d, tn), lambda j, i: (0, j)),
                  pl.BlockSpec((d, tn), lambda j, i: (0, nblk + j)),
                  pl.BlockSpec((CONV_W, tn), lambda j, i: (0, j)),
                  pl.BlockSpec((1, tn), lambda j, i: (0, j))],
        out_specs=pl.BlockSpec((tm, tn), lambda j, i: (i, j)),
        out_shape=jax.ShapeDtypeStruct((m, dff), BF16),
        scratch_shapes=[pltpu.VMEM((tm + 8, tn), F32)],
        compiler_params=_params("parallel", "arbitrary"),
        name="ffn_up_conv",
    )(xn, w_up, w_up, conv_w, conv_b.reshape(1, dff))


def kernel(x, positions, norm_mix, w_in, hgrn_lower_bounds, hgrn_out_norm, q_norm, k_norm,
           sg_ln_g, sg_ln_b, sg_w, sg_b, w_gate, w_branch, w_out, norm_ffn, w_up,
           ffn_conv_w, ffn_conv_b, w_down):
    bsz, s, d = x.shape
    m = bsz * s
    depth = w_in.shape[0]
    in_cols = w_in.shape[2]
    a_cols = 4 * BRANCH_WIDTH
    b_cols = len(B_PATTERNS) * 3 * BRANCH_WIDTH
    tables = _rope_tables(positions)
    xf = x.reshape(m, d)
    for l in range(depth):
        xn = rmsnorm(xf, norm_mix[l])
        proj = matmul(xn, w_in[l].astype(BF16), tm=1024, tn=512)
        proj3 = proj.reshape(bsz, s, in_cols)
        ya = hgrn2_mixer(proj3, hgrn_lower_bounds, hgrn_out_norm[l], l).reshape(m, BRANCH_WIDTH)
        outs, lses = [], []
        for gi, (win, dil) in enumerate(B_PATTERNS):
            assert win // dil == B_BLOCK and (s // dil) % B_BLOCK == 0
            o, lse = dilated_group(proj3, tables, q_norm[l, gi], k_norm[l, gi], gi, dil, a_cols)
            outs.append(o)
            lses.append(lse)
        yb = combine_groups(outs, lses)
        yc = spatial_gating(proj, sg_ln_g[l], sg_ln_b[l], sg_w[l], sg_b[l], a_cols + b_cols)
        ys = jnp.stack([ya, yb, yc], axis=0)
        merged = gated_merge(xn, ys, w_gate[l].astype(BF16), w_branch[l].astype(BF16),
                             tm=1024, tn=512)
        xf = matmul_residual(merged, w_out[l].astype(BF16), xf, tm=1024, tn=512, tk=d)
        xn = rmsnorm(xf, norm_ffn[l])
        h = ffn_up(xn, w_up[l].astype(BF16), ffn_conv_w[l], ffn_conv_b[l], s, tm=1024, tn=512)
        xf = matmul_residual(h, w_down[l].astype(BF16), xf, tm=1024, tn=512, tk=d)
    return xf.reshape(bsz, s, d)
```

```python
import functools
import math

import jax
import jax.numpy as jnp
from jax import lax
from jax.experimental import pallas as pl
from jax.experimental.pallas import tpu as pltpu

F32 = jnp.float32
BF16 = jnp.bfloat16

EPS = 1e-6
NEG_BIG = -1e30

LANES = 128
SUBLANES = 8
VMEM_LIMIT = 56 * 1024 * 1024

HEAD_DIM = 128
N_HEADS = 8
BRANCH_WIDTH = N_HEADS * HEAD_DIM
N_BRANCHES = 3

A_CHUNK = 64
A_SUB = 16
A_STEP_ROWS = 512
A_STEP_HEADS = 4

B_PATTERNS = ((128, 1), (512, 4), (2048, 16))
B_BLOCK = 128
B_TILE = 2048
ROPE_THETA = 500000.0
ROT_DIM = HEAD_DIM // 4
ROT_HALF = ROT_DIM // 2

C_CHUNK = 128
CONV_W = 3


def _params(*sem):
    return pltpu.CompilerParams(dimension_semantics=sem, vmem_limit_bytes=VMEM_LIMIT)


def _dot(a, b):
    return jnp.dot(a, b, preferred_element_type=F32)


def _dot_nt(a, b):
    return lax.dot_general(a, b, (((1,), (1,)), ((), ())), preferred_element_type=F32)


def _dot_tn(a, b):
    return lax.dot_general(a, b, (((0,), (0,)), ((), ())), preferred_element_type=F32)


def _rmsnorm_kernel(x_ref, g_ref, o_ref):
    x = x_ref[...]
    ms = jnp.mean(x * x, axis=-1, keepdims=True)
    o_ref[...] = (x * lax.rsqrt(ms + EPS) * g_ref[...]).astype(o_ref.dtype)


def rmsnorm(x, g, *, rows=256):
    m, d = x.shape
    return pl.pallas_call(
        _rmsnorm_kernel,
        grid=(m // rows,),
        in_specs=[pl.BlockSpec((rows, d), lambda i: (i, 0)),
                  pl.BlockSpec((1, d), lambda i: (0, 0))],
        out_specs=pl.BlockSpec((rows, d), lambda i: (i, 0)),
        out_shape=jax.ShapeDtypeStruct((m, d), BF16),
        compiler_params=_params("parallel"),
        name="rmsnorm",
    )(x, g.reshape(1, d))


def _matmul_kernel(x_ref, w_ref, o_ref):
    o_ref[...] = _dot(x_ref[...], w_ref[...]).astype(o_ref.dtype)


def matmul(x, w, layer, *, tm, tn, out_dtype=F32):
    m, k = x.shape
    n = w.shape[2]
    return pl.pallas_call(
        _matmul_kernel,
        grid=(m // tm, n // tn),
        in_specs=[pl.BlockSpec((tm, k), lambda i, j: (i, 0)),
                  pl.BlockSpec((None, k, tn), lambda i, j: (layer, 0, j))],
        out_specs=pl.BlockSpec((tm, tn), lambda i, j: (i, j)),
        out_shape=jax.ShapeDtypeStruct((m, n), out_dtype),
        compiler_params=_params("parallel", "parallel"),
        name="proj_matmul",
    )(x, w)


def _matmul_res_kernel(x_ref, w_ref, r_ref, o_ref):
    k = pl.program_id(2)

    @pl.when(k == 0)
    def _():
        o_ref[...] = r_ref[...]

    o_ref[...] += _dot(x_ref[...], w_ref[...])


def matmul_residual(x, w, layer, res, *, tm, tn, tk):
    m, k = x.shape
    n = w.shape[2]
    return pl.pallas_call(
        _matmul_res_kernel,
        grid=(m // tm, n // tn, k // tk),
        in_specs=[pl.BlockSpec((tm, tk), lambda i, j, kk: (i, kk)),
                  pl.BlockSpec((None, tk, tn), lambda i, j, kk: (layer, kk, j)),
                  pl.BlockSpec((tm, tn), lambda i, j, kk: (i, j))],
        out_specs=pl.BlockSpec((tm, tn), lambda i, j, kk: (i, j)),
        out_shape=jax.ShapeDtypeStruct((m, n), F32),
        compiler_params=_params("parallel", "parallel", "arbitrary"),
        name="matmul_residual",
    )(x, w, res)


def _hgrn_kernel(lbp_ref, gain_ref, q_ref, f_ref, i_ref, g_ref, o_ref, st_ref, kk_s, cum_s, v_s,
                 *, layer, n_chunks, n_heads):
    n = pl.program_id(2)

    @pl.when(n == 0)
    def _():
        st_ref[...] = jnp.zeros_like(st_ref)

    lbp = lbp_ref[...]
    e = jnp.exp(lbp - jnp.max(lbp, axis=0, keepdims=True))
    sm = e / jnp.sum(e, axis=0, keepdims=True)
    cs = sm[0:1]
    for li in range(1, layer + 1):
        cs = cs + sm[li:li + 1]
    lb_all = cs - sm[0:1]

    gain = gain_ref[...]
    n_sub = A_CHUNK // A_SUB
    row = lax.broadcasted_iota(jnp.int32, (A_CHUNK, A_CHUNK), 0)
    col = lax.broadcasted_iota(jnp.int32, (A_CHUNK, A_CHUNK), 1)
    tri = (row >= col).astype(F32)
    off_mask = (row // A_SUB) > (col // A_SUB)
    rblk = lax.broadcasted_iota(jnp.int32, (A_CHUNK, HEAD_DIM), 0) // A_SUB
    half_row = lax.broadcasted_iota(jnp.int32, (SUBLANES, HEAD_DIM), 0)
    scale = HEAD_DIM ** -0.5

    def head_chunk(hh, r0):
        sl = pl.ds(r0, A_CHUNK)
        hs = slice(hh * HEAD_DIM, (hh + 1) * HEAD_DIM)
        lb = lb_all[:, hs]
        q = q_ref[sl, hs] * scale
        v = i_ref[sl, hs]
        f = lb + (1.0 - lb) * jax.nn.sigmoid(f_ref[sl, hs])
        kk = 1.0 - f
        cum = jnp.dot(tri, jnp.log(f), precision=lax.Precision.HIGHEST,
                      preferred_element_type=F32)
        ends = [cum[(j + 1) * A_SUB - 1:(j + 1) * A_SUB, :] for j in range(n_sub)]
        last = ends[-1]
        eblk = jnp.concatenate([jnp.broadcast_to(ej, (A_SUB, HEAD_DIM)) for ej in ends], axis=0)
        khat = kk * jnp.exp(eblk - cum)
        st = st_ref[hh]

        o = _dot_nt((q * jnp.exp(cum)).astype(BF16), st.astype(BF16))

        s_off = jnp.zeros((A_CHUNK, A_CHUNK), F32)
        for j in range(n_sub - 1):
            qj = q * jnp.exp(jnp.minimum(cum - ends[j], 0.0))
            kj = jnp.where(rblk == j, khat, 0.0)
            s_off = s_off + _dot_nt(qj.astype(BF16), kj.astype(BF16))
        s_off = jnp.where(off_mask, s_off, 0.0)
        o = o + _dot(s_off.astype(BF16), v.astype(BF16))

        kk_s[hh] = kk
        cum_s[hh] = cum
        v_s[hh] = v
        diag = []
        for b in range(n_sub):
            base = b * A_SUB
            q_lo, q_hi = q[base:base + SUBLANES], q[base + SUBLANES:base + A_SUB]
            c_lo, c_hi = cum[base:base + SUBLANES], cum[base + SUBLANES:base + A_SUB]
            o_lo = jnp.zeros((SUBLANES, HEAD_DIM), F32)
            o_hi = jnp.zeros((SUBLANES, HEAD_DIM), F32)
            for s in range(A_SUB):
                r = base + s
                k_row = kk_s[hh, r:r + 1, :]
                c_row = cum_s[hh, r:r + 1, :]
                v_row = v_s[hh, r:r + 1, :]
                if s < SUBLANES:
                    dec = jnp.exp(jnp.where(half_row >= s, c_lo - c_row, NEG_BIG))
                    o_lo = o_lo + jnp.sum(q_lo * k_row * dec, axis=-1, keepdims=True) * v_row
                    dec = jnp.exp(c_hi - c_row)
                else:
                    dec = jnp.exp(jnp.where(half_row >= s - SUBLANES, c_hi - c_row, NEG_BIG))
                o_hi = o_hi + jnp.sum(q_hi * k_row * dec, axis=-1, keepdims=True) * v_row
            diag += [o_lo, o_hi]
        o = o + jnp.concatenate(diag, axis=0)

        ktil = khat * jnp.exp(last - eblk)
        st_ref[hh] = st * jnp.exp(last) + _dot_tn(v.astype(BF16), ktil.astype(BF16))

        ms = jnp.mean(o * o, axis=-1, keepdims=True)
        on = o * lax.rsqrt(ms + EPS) * gain
        g = g_ref[sl, hs]
        o_ref[sl, hs] = (on * (g * jax.nn.sigmoid(g))).astype(o_ref.dtype)

    def chunk(c, carry):
        r0 = pl.multiple_of(c * A_CHUNK, A_CHUNK)
        for hh in range(n_heads):
            head_chunk(hh, r0)
        return carry

    lax.fori_loop(0, n_chunks, chunk, 0)


def hgrn2_mixer(proj3, lower_bounds, out_gain, layer):
    bsz, s, _ = proj3.shape
    depth = lower_bounds.shape[0]
    ts = min(A_STEP_ROWS, s)
    hb = A_STEP_HEADS
    n_hg = N_HEADS // hb
    wide = hb * HEAD_DIM
    blk = lambda part: pl.BlockSpec((None, ts, wide), lambda b, h, n: (b, n, part * n_hg + h))
    return pl.pallas_call(
        functools.partial(_hgrn_kernel, layer=layer, n_chunks=ts // A_CHUNK, n_heads=hb),
        grid=(bsz, n_hg, s // ts),
        in_specs=[pl.BlockSpec((depth, wide), lambda b, h, n: (0, h)),
                  pl.BlockSpec((1, HEAD_DIM), lambda b, h, n: (0, 0)),
                  blk(0), blk(1), blk(2), blk(3)],
        out_specs=pl.BlockSpec((None, ts, wide), lambda b, h, n: (b, n, h)),
        out_shape=jax.ShapeDtypeStruct((bsz, s, BRANCH_WIDTH), BF16),
        scratch_shapes=[pltpu.VMEM((hb, HEAD_DIM, HEAD_DIM), F32)]
                       + [pltpu.VMEM((hb, A_CHUNK, HEAD_DIM), F32)] * 3,
        compiler_params=_params("parallel", "parallel", "arbitrary"),
        name="hgrn2",
    )(lower_bounds, out_gain.reshape(1, HEAD_DIM), proj3, proj3, proj3, proj3)


def _rope_tables(positions):
    inv = jnp.power(jnp.float32(ROPE_THETA), -jnp.arange(0, ROT_DIM, 2, dtype=F32) / ROT_DIM)
    ang = positions.astype(F32)[..., None] * inv
    cos = jnp.cos(ang)
    sin = jnp.sin(ang)
    zero = jnp.zeros_like(sin)
    pad = jnp.zeros(ang.shape[:-1] + (HEAD_DIM - ROT_DIM,), F32)
    cos_f = jnp.concatenate([cos, cos, pad + 1.0], axis=-1)
    sin_lo = jnp.concatenate([-sin, zero, pad], axis=-1)
    sin_hi = jnp.concatenate([zero, sin, pad], axis=-1)
    return cos_f, sin_lo, sin_hi


def _dilated_kernel(qg_ref, kg_ref, cos_ref, slo_ref, shi_ref, *rest):
    n_grp = len(B_PATTERNS)
    qkv = rest[:3 * n_grp]
    y_ref = rest[3 * n_grp]
    scratch = rest[3 * n_grp + 1:]
    kv_s = scratch[:2 * n_grp]
    o_s = scratch[2 * n_grp:3 * n_grp]
    lse_s = scratch[3 * n_grp:]
    n = pl.program_id(2)

    @pl.when(n == 0)
    def _():
        for g in range(n_grp):
            zeros = jnp.zeros((B_PATTERNS[g][1], B_BLOCK, HEAD_DIM), BF16)
            kv_s[2 * g][:, 0:B_BLOCK, :] = zeros
            kv_s[2 * g + 1][:, 0:B_BLOCK, :] = zeros

    qi = lax.broadcasted_iota(jnp.int32, (B_BLOCK, 2 * B_BLOCK), 0)
    ci = lax.broadcasted_iota(jnp.int32, (B_BLOCK, 2 * B_BLOCK), 1)
    cur_ok = jnp.logical_and(ci >= B_BLOCK, ci - B_BLOCK <= qi)
    prev_ok = jnp.logical_and(ci < B_BLOCK, ci >= qi)
    mask_inner = jnp.logical_or(cur_ok, prev_ok)
    mask_first = jnp.logical_or(cur_ok, jnp.logical_and(prev_ok, n > 0))
    scale = HEAD_DIM ** -0.5

    def norm_rope(x, gain, cos, slo, shi):
        ms = jnp.mean(x * x, axis=-1, keepdims=True)
        xn = x * lax.rsqrt(ms + EPS) * gain
        return (xn * cos + pltpu.roll(xn, HEAD_DIM - ROT_HALF, 1) * slo
                + pltpu.roll(xn, ROT_HALF, 1) * shi)

    for g, (win, dil) in enumerate(B_PATTERNS):
        q_ref, k_ref, v_ref = qkv[3 * g:3 * g + 3]
        ks, vs = kv_s[2 * g], kv_s[2 * g + 1]
        qg = qg_ref[g:g + 1, :]
        kg = kg_ref[g:g + 1, :]
        n_blk = B_TILE // (B_BLOCK * dil)
        for r in range(dil):
            for j in range(n_blk):
                start = j * B_BLOCK * dil + r
                rows = pl.ds(start, B_BLOCK) if dil == 1 else pl.ds(start, B_BLOCK, stride=dil)
                cos, slo, shi = cos_ref[rows, :], slo_ref[rows, :], shi_ref[rows, :]
                cur = slice((j + 1) * B_BLOCK, (j + 2) * B_BLOCK)
                ks[r, cur, :] = norm_rope(k_ref[rows, :], kg, cos, slo, shi).astype(BF16)
                vs[r, cur, :] = v_ref[rows, :].astype(BF16)
                q = (norm_rope(q_ref[rows, :], qg, cos, slo, shi) * scale).astype(BF16)
                both = slice(j * B_BLOCK, (j + 2) * B_BLOCK)
                s = _dot_nt(q, ks[r, both, :])
                s = jnp.where(mask_first if j == 0 else mask_inner, s, NEG_BIG)
                s_lo, s_hi = s[:, :B_BLOCK], s[:, B_BLOCK:]
                m_blk = jnp.max(jnp.maximum(s_lo, s_hi), axis=-1, keepdims=True)
                p_lo = jnp.exp(s_lo - m_blk)
                p_hi = jnp.exp(s_hi - m_blk)
                l_blk = jnp.sum(p_lo + p_hi, axis=-1, keepdims=True)
                p = jnp.concatenate([p_lo, p_hi], axis=1).astype(BF16)
                o_s[g][rows, :] = _dot(p, vs[r, both, :]) / l_blk
                lse_s[g][rows, :] = jnp.broadcast_to(m_blk + jnp.log(l_blk), (B_BLOCK, HEAD_DIM))
            ks[r, 0:B_BLOCK, :] = ks[r, n_blk * B_BLOCK:(n_blk + 1) * B_BLOCK, :]
            vs[r, 0:B_BLOCK, :] = vs[r, n_blk * B_BLOCK:(n_blk + 1) * B_BLOCK, :]

    def combine(c, carry):
        rows = pl.ds(pl.multiple_of(c * B_BLOCK, B_BLOCK), B_BLOCK)
        lse = [ref[rows, :] for ref in lse_s]
        m = functools.reduce(jnp.maximum, lse)
        w = [jnp.exp(x - m) for x in lse]
        num = functools.reduce(jnp.add, [wg * ref[rows, :] for wg, ref in zip(w, o_s)])
        y_ref[rows, :] = (num / functools.reduce(jnp.add, w)).astype(y_ref.dtype)
        return carry

    lax.fori_loop(0, B_TILE // B_BLOCK, combine, 0)


def dilated_mixer(proj3, tables, q_gain, k_gain, col0):
    bsz, s, _ = proj3.shape
    n_grp = len(B_PATTERNS)
    assert s % B_TILE == 0

    def part(g, p):
        base = (col0 + (3 * g + p) * BRANCH_WIDTH) // HEAD_DIM
        return pl.BlockSpec((None, B_TILE, HEAD_DIM), lambda b, h, n: (b, n, base + h))

    tab = pl.BlockSpec((None, B_TILE, HEAD_DIM), lambda b, h, n: (b, n, 0))
    gain = pl.BlockSpec((n_grp, HEAD_DIM), lambda b, h, n: (0, 0))
    kv_scratch = []
    for win, dil in B_PATTERNS:
        assert win // dil == B_BLOCK and B_TILE % (B_BLOCK * dil) == 0
        rows = (B_TILE // (B_BLOCK * dil) + 1) * B_BLOCK
        kv_scratch += [pltpu.VMEM((dil, rows, HEAD_DIM), BF16)] * 2
    return pl.pallas_call(
        _dilated_kernel,
        grid=(bsz, N_HEADS, s // B_TILE),
        in_specs=[gain, gain, tab, tab, tab]
                 + [part(g, p) for g in range(n_grp) for p in range(3)],
        out_specs=pl.BlockSpec((None, B_TILE, HEAD_DIM), lambda b, h, n: (b, n, h)),
        out_shape=jax.ShapeDtypeStruct((bsz, s, BRANCH_WIDTH), BF16),
        scratch_shapes=kv_scratch + [pltpu.VMEM((B_TILE, HEAD_DIM), F32)] * (2 * n_grp),
        compiler_params=_params("parallel", "parallel", "arbitrary"),
        name="dilated_attn",
    )(q_gain, k_gain, *tables, *([proj3] * (3 * n_grp)))


def _spatial_kernel(u_ref, v_ref, lng_ref, lnb_ref, w_ref, b_ref, o_ref):
    inv_sqrt2 = 1.0 / math.sqrt(2.0)

    def gelu(x):
        return 0.5 * x * (1.0 + lax.erf(x * inv_sqrt2))

    v = gelu(v_ref[...])
    mu = jnp.mean(v, axis=-1, keepdims=True)
    vc = v - mu
    var = jnp.mean(vc * vc, axis=-1, keepdims=True)
    vn = vc * lax.rsqrt(var + EPS) * lng_ref[...] + lnb_ref[...]
    row = lax.broadcasted_iota(jnp.int32, (C_CHUNK, C_CHUNK), 0)
    col = lax.broadcasted_iota(jnp.int32, (C_CHUNK, C_CHUNK), 1)
    causal = row >= col
    bias = b_ref[...]
    for g in range(N_HEADS):
        gs = slice(g * HEAD_DIM, (g + 1) * HEAD_DIM)
        wm = jnp.where(causal, w_ref[g], 0.0).astype(BF16)
        mixed = _dot(wm, vn[:, gs].astype(BF16)) + bias[:, g:g + 1]
        o_ref[:, gs] = (gelu(u_ref[:, gs]) * mixed).astype(o_ref.dtype)


def spatial_gating(proj, ln_g, ln_b, w_s, b_s, col0):
    m, _ = proj.shape
    ublk = col0 // BRANCH_WIDTH
    return pl.pallas_call(
        _spatial_kernel,
        grid=(m // C_CHUNK,),
        in_specs=[pl.BlockSpec((C_CHUNK, BRANCH_WIDTH), lambda i: (i, ublk)),
                  pl.BlockSpec((C_CHUNK, BRANCH_WIDTH), lambda i: (i, ublk + 1)),
                  pl.BlockSpec((1, BRANCH_WIDTH), lambda i: (0, 0)),
                  pl.BlockSpec((1, BRANCH_WIDTH), lambda i: (0, 0)),
                  pl.BlockSpec((N_HEADS, C_CHUNK, C_CHUNK), lambda i: (0, 0, 0)),
                  pl.BlockSpec((C_CHUNK, N_HEADS), lambda i: (0, 0))],
        out_specs=pl.BlockSpec((C_CHUNK, BRANCH_WIDTH), lambda i: (i, 0)),
        out_shape=jax.ShapeDtypeStruct((m, BRANCH_WIDTH), BF16),
        compiler_params=_params("parallel"),
        name="spatial_gating",
    )(proj, proj, ln_g.reshape(1, -1), ln_b.reshape(1, -1), w_s, b_s.T)


def _merge_kernel(xn_ref, y_ref, wg_ref, wb_ref, o_ref, acc_ref):
    i = pl.program_id(2)
    gate = jax.nn.sigmoid(_dot(xn_ref[...], wg_ref[...]))
    term = gate * _dot(y_ref[...], wb_ref[...])

    @pl.when(i == 0)
    def _():
        acc_ref[...] = term

    @pl.when(i > 0)
    def _():
        acc_ref[...] += term

    @pl.when(i == N_BRANCHES - 1)
    def _():
        o_ref[...] = acc_ref[...].astype(o_ref.dtype)


def gated_merge(xn, ys, w_gate, w_branch, layer, *, tm, tn):
    m, d = xn.shape
    w = ys.shape[2]
    nblk = d // tn
    return pl.pallas_call(
        _merge_kernel,
        grid=(m // tm, nblk, N_BRANCHES),
        in_specs=[pl.BlockSpec((tm, d), lambda a, b, i: (a, 0)),
                  pl.BlockSpec((None, tm, w), lambda a, b, i: (i, a, 0)),
                  pl.BlockSpec((None, d, tn), lambda a, b, i: (layer, 0, i * nblk + b)),
                  pl.BlockSpec((None, None, w, tn), lambda a, b, i: (layer, i, 0, b))],
        out_specs=pl.BlockSpec((tm, tn), lambda a, b, i: (a, b)),
        out_shape=jax.ShapeDtypeStruct((m, d), BF16),
        scratch_shapes=[pltpu.VMEM((tm, tn), F32)],
        compiler_params=_params("parallel", "parallel", "arbitrary"),
        name="gated_merge",
    )(xn, ys, w_gate, w_branch)


def _ffn_up_kernel(x_ref, wg_ref, wv_ref, cw_ref, cb_ref, o_ref, g_s, *, tiles_per_seq):
    i = pl.program_id(1)
    tm = x_ref.shape[0]
    halo = SUBLANES

    @pl.when(i % tiles_per_seq == 0)
    def _():
        g_s[0:halo, :] = jnp.zeros((halo, g_s.shape[1]), F32)

    @pl.when(i % tiles_per_seq != 0)
    def _():
        g_s[0:halo, :] = g_s[tm:tm + halo, :]

    x = x_ref[...]
    g_s[halo:tm + halo, :] = _dot(x, wg_ref[...])
    val = _dot(x, wv_ref[...])
    cw = cw_ref[...]
    conv = (cw[0:1] * g_s[halo - 2:tm + halo - 2, :] + cw[1:2] * g_s[halo - 1:tm + halo - 1, :]
            + cw[2:3] * g_s[halo:tm + halo, :] + cb_ref[...])
    o_ref[...] = (conv * jax.nn.sigmoid(conv) * val).astype(o_ref.dtype)


def ffn_up(xn, w_up, layer, conv_w, conv_b, seq_len, *, tm, tn):
    m, d = xn.shape
    dff = conv_w.shape[1]
    nblk = dff // tn
    return pl.pallas_call(
        functools.partial(_ffn_up_kernel, tiles_per_seq=seq_len // tm),
        grid=(nblk, m // tm),
        in_specs=[pl.BlockSpec((tm, d), lambda j, i: (i, 0)),
                  pl.BlockSpec((None, d, tn), lambda j, i: (layer, 0, j)),
                  pl.BlockSpec((None, d, tn), lambda j, i: (layer, 0, nblk + j)),
                  pl.BlockSpec((CONV_W, tn), lambda j, i: (0, j)),
                  pl.BlockSpec((1, tn), lambda j, i: (0, j))],
        out_specs=pl.BlockSpec((tm, tn), lambda j, i: (i, j)),
        out_shape=jax.ShapeDtypeStruct((m, dff), BF16),
        scratch_shapes=[pltpu.VMEM((tm + SUBLANES, tn), F32)],
        compiler_params=_params("parallel", "arbitrary"),
        name="ffn_up_conv",
    )(xn, w_up, w_up, conv_w, conv_b.reshape(1, dff))


def kernel(x, positions, norm_mix, w_in, hgrn_lower_bounds, hgrn_out_norm, q_norm, k_norm,
           sg_ln_g, sg_ln_b, sg_w, sg_b, w_gate, w_branch, w_out, norm_ffn, w_up,
           ffn_conv_w, ffn_conv_b, w_down):
    bsz, s, d = x.shape
    m = bsz * s
    depth = w_in.shape[0]
    in_cols = w_in.shape[2]
    a_cols = 4 * BRANCH_WIDTH
    b_cols = len(B_PATTERNS) * 3 * BRANCH_WIDTH
    tables = _rope_tables(positions)
    w_in, w_gate, w_branch, w_out, w_up, w_down = (
        w.astype(BF16) for w in (w_in, w_gate, w_branch, w_out, w_up, w_down))
    xf = x.reshape(m, d)
    for l in range(depth):
        xn = rmsnorm(xf, norm_mix[l])
        proj = matmul(xn, w_in, l, tm=1024, tn=512)
        proj3 = proj.reshape(bsz, s, in_cols)
        ya = hgrn2_mixer(proj3, hgrn_lower_bounds, hgrn_out_norm[l], l).reshape(m, BRANCH_WIDTH)
        yb = dilated_mixer(proj3, tables, q_norm[l], k_norm[l], a_cols).reshape(m, BRANCH_WIDTH)
        yc = spatial_gating(proj, sg_ln_g[l], sg_ln_b[l], sg_w[l], sg_b[l], a_cols + b_cols)
        ys = jnp.stack([ya, yb, yc], axis=0)
        merged = gated_merge(xn, ys, w_gate, w_branch, l, tm=1024, tn=512)
        xf = matmul_residual(merged, w_out, l, xf, tm=1024, tn=512, tk=d)
        xn = rmsnorm(xf, norm_ffn[l])
        h = ffn_up(xn, w_up, l, ffn_conv_w[l], ffn_conv_b[l], s, tm=1024, tn=512)
        xf = matmul_residual(h, w_down, l, xf, tm=1024, tn=512, tk=d)
    return xf.reshape(bsz, s, d)
```

```python
import functools
import math

import jax
import jax.numpy as jnp
from jax import lax
from jax.experimental import pallas as pl
from jax.experimental.pallas import tpu as pltpu

F32 = jnp.float32
BF16 = jnp.bfloat16

EPS = 1e-6
NEG_BIG = -1e30

LANES = 128
SUBLANES = 8
VMEM_LIMIT = 56 * 1024 * 1024

HEAD_DIM = 128
N_HEADS = 8
BRANCH_WIDTH = N_HEADS * HEAD_DIM
N_BRANCHES = 3

A_CHUNK = 64
A_SUB = 16
A_STEP_ROWS = 512
A_STEP_HEADS = 4

B_PATTERNS = ((128, 1), (512, 4), (2048, 16))
B_BLOCK = 128
B_TILE = 2048
B_PREP_ROWS = 256
B_BATCH = 8
ROPE_THETA = 500000.0
ROT_DIM = HEAD_DIM // 4
ROT_HALF = ROT_DIM // 2

C_CHUNK = 128
CONV_W = 3


def _params(*sem):
    return pltpu.CompilerParams(dimension_semantics=sem, vmem_limit_bytes=VMEM_LIMIT)


def _dot(a, b):
    return jnp.dot(a, b, preferred_element_type=F32)


def _dot_nt(a, b):
    return lax.dot_general(a, b, (((1,), (1,)), ((), ())), preferred_element_type=F32)


def _dot_tn(a, b):
    return lax.dot_general(a, b, (((0,), (0,)), ((), ())), preferred_element_type=F32)


def _rmsnorm_kernel(x_ref, g_ref, o_ref):
    x = x_ref[...]
    ms = jnp.mean(x * x, axis=-1, keepdims=True)
    o_ref[...] = (x * lax.rsqrt(ms + EPS) * g_ref[...]).astype(o_ref.dtype)


def rmsnorm(x, g, *, rows=256):
    m, d = x.shape
    return pl.pallas_call(
        _rmsnorm_kernel,
        grid=(m // rows,),
        in_specs=[pl.BlockSpec((rows, d), lambda i: (i, 0)),
                  pl.BlockSpec((1, d), lambda i: (0, 0))],
        out_specs=pl.BlockSpec((rows, d), lambda i: (i, 0)),
        out_shape=jax.ShapeDtypeStruct((m, d), BF16),
        compiler_params=_params("parallel"),
        name="rmsnorm",
    )(x, g.reshape(1, d))


def _matmul_kernel(x_ref, w_ref, o_ref):
    o_ref[...] = _dot(x_ref[...], w_ref[...]).astype(o_ref.dtype)


def matmul(x, w, layer, *, tm, tn, out_dtype=F32):
    m, k = x.shape
    n = w.shape[2]
    return pl.pallas_call(
        _matmul_kernel,
        grid=(m // tm, n // tn),
        in_specs=[pl.BlockSpec((tm, k), lambda i, j: (i, 0)),
                  pl.BlockSpec((None, k, tn), lambda i, j: (layer, 0, j))],
        out_specs=pl.BlockSpec((tm, tn), lambda i, j: (i, j)),
        out_shape=jax.ShapeDtypeStruct((m, n), out_dtype),
        compiler_params=_params("parallel", "parallel"),
        name="proj_matmul",
    )(x, w)


def _matmul_res_kernel(x_ref, w_ref, r_ref, o_ref):
    k = pl.program_id(2)

    @pl.when(k == 0)
    def _():
        o_ref[...] = r_ref[...]

    o_ref[...] += _dot(x_ref[...], w_ref[...])


def matmul_residual(x, w, layer, res, *, tm, tn, tk):
    m, k = x.shape
    n = w.shape[2]
    return pl.pallas_call(
        _matmul_res_kernel,
        grid=(m // tm, n // tn, k // tk),
        in_specs=[pl.BlockSpec((tm, tk), lambda i, j, kk: (i, kk)),
                  pl.BlockSpec((None, tk, tn), lambda i, j, kk: (layer, kk, j)),
                  pl.BlockSpec((tm, tn), lambda i, j, kk: (i, j))],
        out_specs=pl.BlockSpec((tm, tn), lambda i, j, kk: (i, j)),
        out_shape=jax.ShapeDtypeStruct((m, n), F32),
        compiler_params=_params("parallel", "parallel", "arbitrary"),
        name="matmul_residual",
    )(x, w, res)


def _hgrn_kernel(lbp_ref, gain_ref, q_ref, f_ref, i_ref, g_ref, o_ref, st_ref, kk_s, cum_s, v_s,
                 *, layer, n_chunks, n_heads):
    n = pl.program_id(2)

    @pl.when(n == 0)
    def _():
        st_ref[...] = jnp.zeros_like(st_ref)

    lbp = lbp_ref[...]
    e = jnp.exp(lbp - jnp.max(lbp, axis=0, keepdims=True))
    sm = e / jnp.sum(e, axis=0, keepdims=True)
    cs = sm[0:1]
    for li in range(1, layer + 1):
        cs = cs + sm[li:li + 1]
    lb_all = cs - sm[0:1]

    gain = gain_ref[...]
    n_sub = A_CHUNK // A_SUB
    row = lax.broadcasted_iota(jnp.int32, (A_CHUNK, A_CHUNK), 0)
    col = lax.broadcasted_iota(jnp.int32, (A_CHUNK, A_CHUNK), 1)
    tri = (row >= col).astype(F32)
    off_mask = (row // A_SUB) > (col // A_SUB)
    rblk = lax.broadcasted_iota(jnp.int32, (A_CHUNK, HEAD_DIM), 0) // A_SUB
    half_row = lax.broadcasted_iota(jnp.int32, (SUBLANES, HEAD_DIM), 0)
    scale = HEAD_DIM ** -0.5

    def head_chunk(hh, r0):
        sl = pl.ds(r0, A_CHUNK)
        hs = slice(hh * HEAD_DIM, (hh + 1) * HEAD_DIM)
        lb = lb_all[:, hs]
        q = q_ref[sl, hs] * scale
        v = i_ref[sl, hs]
        f = lb + (1.0 - lb) * jax.nn.sigmoid(f_ref[sl, hs])
        kk = 1.0 - f
        cum = jnp.dot(tri, jnp.log(f), precision=lax.Precision.HIGHEST,
                      preferred_element_type=F32)
        ends = [cum[(j + 1) * A_SUB - 1:(j + 1) * A_SUB, :] for j in range(n_sub)]
        last = ends[-1]
        eblk = jnp.concatenate([jnp.broadcast_to(ej, (A_SUB, HEAD_DIM)) for ej in ends], axis=0)
        khat = kk * jnp.exp(eblk - cum)
        st = st_ref[hh]

        o = _dot_nt((q * jnp.exp(cum)).astype(BF16), st.astype(BF16))

        s_off = jnp.zeros((A_CHUNK, A_CHUNK), F32)
        for j in range(n_sub - 1):
            qj = q * jnp.exp(jnp.minimum(cum - ends[j], 0.0))
            kj = jnp.where(rblk == j, khat, 0.0)
            s_off = s_off + _dot_nt(qj.astype(BF16), kj.astype(BF16))
        s_off = jnp.where(off_mask, s_off, 0.0)
        o = o + _dot(s_off.astype(BF16), v.astype(BF16))

        kk_s[hh] = kk
        cum_s[hh] = cum
        v_s[hh] = v
        diag = []
        for b in range(n_sub):
            base = b * A_SUB
            q_lo, q_hi = q[base:base + SUBLANES], q[base + SUBLANES:base + A_SUB]
            c_lo, c_hi = cum[base:base + SUBLANES], cum[base + SUBLANES:base + A_SUB]
            o_lo = jnp.zeros((SUBLANES, HEAD_DIM), F32)
            o_hi = jnp.zeros((SUBLANES, HEAD_DIM), F32)
            for s in range(A_SUB):
                r = base + s
                k_row = kk_s[hh, r:r + 1, :]
                c_row = cum_s[hh, r:r + 1, :]
                v_row = v_s[hh, r:r + 1, :]
                if s < SUBLANES:
                    dec = jnp.exp(jnp.where(half_row >= s, c_lo - c_row, NEG_BIG))
                    o_lo = o_lo + jnp.sum(q_lo * k_row * dec, axis=-1, keepdims=True) * v_row
                    dec = jnp.exp(c_hi - c_row)
                else:
                    dec = jnp.exp(jnp.where(half_row >= s - SUBLANES, c_hi - c_row, NEG_BIG))
                o_hi = o_hi + jnp.sum(q_hi * k_row * dec, axis=-1, keepdims=True) * v_row
            diag += [o_lo, o_hi]
        o = o + jnp.concatenate(diag, axis=0)

        ktil = khat * jnp.exp(last - eblk)
        st_ref[hh] = st * jnp.exp(last) + _dot_tn(v.astype(BF16), ktil.astype(BF16))

        ms = jnp.mean(o * o, axis=-1, keepdims=True)
        on = o * lax.rsqrt(ms + EPS) * gain
        g = g_ref[sl, hs]
        o_ref[sl, hs] = (on * (g * jax.nn.sigmoid(g))).astype(o_ref.dtype)

    def chunk(c, carry):
        r0 = pl.multiple_of(c * A_CHUNK, A_CHUNK)
        for hh in range(n_heads):
            head_chunk(hh, r0)
        return carry

    lax.fori_loop(0, n_chunks, chunk, 0)


def hgrn2_mixer(proj3, lower_bounds, out_gain, layer):
    bsz, s, _ = proj3.shape
    depth = lower_bounds.shape[0]
    ts = min(A_STEP_ROWS, s)
    hb = A_STEP_HEADS
    n_hg = N_HEADS // hb
    wide = hb * HEAD_DIM
    blk = lambda part: pl.BlockSpec((None, ts, wide), lambda b, h, n: (b, n, part * n_hg + h))
    return pl.pallas_call(
        functools.partial(_hgrn_kernel, layer=layer, n_chunks=ts // A_CHUNK, n_heads=hb),
        grid=(bsz, n_hg, s // ts),
        in_specs=[pl.BlockSpec((depth, wide), lambda b, h, n: (0, h)),
                  pl.BlockSpec((1, HEAD_DIM), lambda b, h, n: (0, 0)),
                  blk(0), blk(1), blk(2), blk(3)],
        out_specs=pl.BlockSpec((None, ts, wide), lambda b, h, n: (b, n, h)),
        out_shape=jax.ShapeDtypeStruct((bsz, s, BRANCH_WIDTH), BF16),
        scratch_shapes=[pltpu.VMEM((hb, HEAD_DIM, HEAD_DIM), F32)]
                       + [pltpu.VMEM((hb, A_CHUNK, HEAD_DIM), F32)] * 3,
        compiler_params=_params("parallel", "parallel", "arbitrary"),
        name="hgrn2",
    )(lower_bounds, out_gain.reshape(1, HEAD_DIM), proj3, proj3, proj3, proj3)


def _rope_tables(positions):
    inv = jnp.power(jnp.float32(ROPE_THETA), -jnp.arange(0, ROT_DIM, 2, dtype=F32) / ROT_DIM)
    ang = positions.astype(F32)[..., None] * inv
    cos = jnp.cos(ang)
    sin = jnp.sin(ang)
    pad = jnp.zeros(ang.shape[:-1] + (HEAD_DIM - ROT_DIM,), F32)
    cos_f = jnp.concatenate([cos, cos, pad + 1.0], axis=-1)
    sin_f = jnp.concatenate([sin, sin, pad], axis=-1)
    return cos_f, sin_f


def _dilated_kernel(qg_ref, kg_ref, cos_ref, sin_ref, *rest):
    n_grp = len(B_PATTERNS)
    qkv = rest[:3 * n_grp]
    y_ref = rest[3 * n_grp]
    scratch = list(rest[3 * n_grp + 1:])
    qn_s, kn_s, kc, vc, o_s, lse_s = (scratch[i * n_grp:(i + 1) * n_grp] for i in range(6))
    s_s, p_s, vb_s = scratch[6 * n_grp:]
    n = pl.program_id(2)

    @pl.when(n == 0)
    def _():
        for g in range(n_grp):
            kc[g][...] = jnp.zeros_like(kc[g])
            vc[g][...] = jnp.zeros_like(vc[g])

    scale = HEAD_DIM ** -0.5

    src = lax.broadcasted_iota(jnp.int32, (HEAD_DIM, HEAD_DIM), 0)
    dst = lax.broadcasted_iota(jnp.int32, (HEAD_DIM, HEAD_DIM), 1)
    rot = jnp.where(jnp.logical_and(dst < ROT_HALF, src == dst + ROT_HALF), -1.0,
                    jnp.where(jnp.logical_and(jnp.logical_and(dst >= ROT_HALF, dst < ROT_DIM),
                                              src == dst - ROT_HALF), 1.0, 0.0)).astype(BF16)

    def norm_rope(x, gain, cos, sin):
        ms = jnp.mean(x * x, axis=-1, keepdims=True)
        xn = x * lax.rsqrt(ms + EPS) * gain
        hi = xn.astype(BF16)
        lo = (xn - hi.astype(F32)).astype(BF16)
        return xn * cos + (_dot(hi, rot) + _dot(lo, rot)) * sin

    def prep(c, carry):
        rows = pl.ds(pl.multiple_of(c * B_PREP_ROWS, B_PREP_ROWS), B_PREP_ROWS)
        cos, sin = cos_ref[rows, :], sin_ref[rows, :]
        for g in range(n_grp):
            q_ref, k_ref = qkv[3 * g], qkv[3 * g + 1]
            qn_s[g][rows, :] = norm_rope(q_ref[rows, :], qg_ref[g:g + 1, :], cos, sin) * scale
            kn_s[g][rows, :] = norm_rope(k_ref[rows, :], kg_ref[g:g + 1, :], cos, sin)
        return carry

    lax.fori_loop(0, B_TILE // B_PREP_ROWS, prep, 0)

    qi = lax.broadcasted_iota(jnp.int32, (B_BLOCK, 2 * B_BLOCK), 0)
    ci = lax.broadcasted_iota(jnp.int32, (B_BLOCK, 2 * B_BLOCK), 1)
    cur_ok = jnp.logical_and(ci >= B_BLOCK, ci - B_BLOCK <= qi)
    prev_ok = jnp.logical_and(ci < B_BLOCK, ci >= qi)
    mask_inner = jnp.logical_or(cur_ok, prev_ok)
    mask_first = jnp.logical_or(cur_ok, jnp.logical_and(prev_ok, n > 0))
    ones = jnp.ones((2 * B_BLOCK, HEAD_DIM), BF16)

    for g, (win, dil) in enumerate(B_PATTERNS):
        v_ref = qkv[3 * g + 2]
        n_blk = B_TILE // (B_BLOCK * dil)
        blocks = [(r, j) for r in range(dil) for j in range(n_blk)]
        k_prev = v_prev = None
        for b0 in range(0, len(blocks), B_BATCH):
            batch = blocks[b0:b0 + B_BATCH]
            rows_of = []
            for i, (r, j) in enumerate(batch):
                start = j * B_BLOCK * dil + r
                rows = pl.ds(start, B_BLOCK) if dil == 1 else pl.ds(start, B_BLOCK, stride=dil)
                rows_of.append(rows)
                blk = slice(i * B_BLOCK, (i + 1) * B_BLOCK)
                if j == 0:
                    k_prev, v_prev = kc[g][r], vc[g][r]
                k_cur = kn_s[g][rows, :].astype(BF16)
                v_cur = v_ref[rows, :].astype(BF16)
                q = qn_s[g][rows, :].astype(BF16)
                s = _dot_nt(q, jnp.concatenate([k_prev, k_cur], axis=0))
                s_s[blk, :] = jnp.where(mask_first if j == 0 else mask_inner, s, NEG_BIG)
                vb_s[i, 0:B_BLOCK, :] = v_prev
                vb_s[i, B_BLOCK:2 * B_BLOCK, :] = v_cur
                if j == n_blk - 1:
                    kc[g][r] = k_cur
                    vc[g][r] = v_cur
                k_prev, v_prev = k_cur, v_cur
            live = slice(0, len(batch) * B_BLOCK)
            m = jnp.max(jnp.maximum(s_s[live, 0:B_BLOCK], s_s[live, B_BLOCK:2 * B_BLOCK]),
                        axis=-1, keepdims=True)
            p_s[live, :] = jnp.exp(s_s[live, :] - m).astype(BF16)
            for i, rows in enumerate(rows_of):
                blk = slice(i * B_BLOCK, (i + 1) * B_BLOCK)
                pv = _dot(p_s[blk, :], jnp.concatenate([vb_s[i], ones], axis=1))
                den = pv[:, HEAD_DIM:]
                o_s[g][rows, :] = pv[:, :HEAD_DIM] / den
                lse_s[g][rows, :] = m[blk] + jnp.log(den)

    def combine(c, carry):
        rows = pl.ds(pl.multiple_of(c * B_BLOCK, B_BLOCK), B_BLOCK)
        lse = [ref[rows, :] for ref in lse_s]
        m = functools.reduce(jnp.maximum, lse)
        w = [jnp.exp(x - m) for x in lse]
        num = functools.reduce(jnp.add, [wg * ref[rows, :] for wg, ref in zip(w, o_s)])
        y_ref[rows, :] = (num / functools.reduce(jnp.add, w)).astype(y_ref.dtype)
        return carry

    lax.fori_loop(0, B_TILE // B_BLOCK, combine, 0)


def dilated_mixer(proj3, tables, q_gain, k_gain, col0):
    bsz, s, _ = proj3.shape
    n_grp = len(B_PATTERNS)
    assert s % B_TILE == 0

    def part(g, p):
        base = (col0 + (3 * g + p) * BRANCH_WIDTH) // HEAD_DIM
        return pl.BlockSpec((None, B_TILE, HEAD_DIM), lambda b, h, n: (b, n, base + h))

    tab = pl.BlockSpec((None, B_TILE, HEAD_DIM), lambda b, h, n: (b, n, 0))
    gain = pl.BlockSpec((n_grp, HEAD_DIM), lambda b, h, n: (0, 0))
    for win, dil in B_PATTERNS:
        assert win // dil == B_BLOCK and B_TILE % (B_BLOCK * dil) == 0
    token_f32 = [pltpu.VMEM((B_TILE, HEAD_DIM), F32)] * n_grp
    carry = [pltpu.VMEM((dil, B_BLOCK, HEAD_DIM), BF16) for _, dil in B_PATTERNS]
    scratch = (token_f32 + token_f32 + carry + carry + token_f32 + token_f32
               + [pltpu.VMEM((B_BATCH * B_BLOCK, 2 * B_BLOCK), F32),
                  pltpu.VMEM((B_BATCH * B_BLOCK, 2 * B_BLOCK), BF16),
                  pltpu.VMEM((B_BATCH, 2 * B_BLOCK, HEAD_DIM), BF16)])
    return pl.pallas_call(
        _dilated_kernel,
        grid=(bsz, N_HEADS, s // B_TILE),
        in_specs=[gain, gain, tab, tab]
                 + [part(g, p) for g in range(n_grp) for p in range(3)],
        out_specs=pl.BlockSpec((None, B_TILE, HEAD_DIM), lambda b, h, n: (b, n, h)),
        out_shape=jax.ShapeDtypeStruct((bsz, s, BRANCH_WIDTH), BF16),
        scratch_shapes=scratch,
        compiler_params=_params("parallel", "parallel", "arbitrary"),
        name="dilated_attn",
    )(q_gain, k_gain, *tables, *([proj3] * (3 * n_grp)))


def _spatial_kernel(u_ref, v_ref, lng_ref, lnb_ref, w_ref, b_ref, o_ref):
    inv_sqrt2 = 1.0 / math.sqrt(2.0)

    def gelu(x):
        return 0.5 * x * (1.0 + lax.erf(x * inv_sqrt2))

    v = gelu(v_ref[...])
    mu = jnp.mean(v, axis=-1, keepdims=True)
    vc = v - mu
    var = jnp.mean(vc * vc, axis=-1, keepdims=True)
    vn = vc * lax.rsqrt(var + EPS) * lng_ref[...] + lnb_ref[...]
    row = lax.broadcasted_iota(jnp.int32, (C_CHUNK, C_CHUNK), 0)
    col = lax.broadcasted_iota(jnp.int32, (C_CHUNK, C_CHUNK), 1)
    causal = row >= col
    bias = b_ref[...]
    for g in range(N_HEADS):
        gs = slice(g * HEAD_DIM, (g + 1) * HEAD_DIM)
        wm = jnp.where(causal, w_ref[g], 0.0).astype(BF16)
        mixed = _dot(wm, vn[:, gs].astype(BF16)) + bias[:, g:g + 1]
        o_ref[:, gs] = (gelu(u_ref[:, gs]) * mixed).astype(o_ref.dtype)


def spatial_gating(proj, ln_g, ln_b, w_s, b_s, col0):
    m, _ = proj.shape
    ublk = col0 // BRANCH_WIDTH
    return pl.pallas_call(
        _spatial_kernel,
        grid=(m // C_CHUNK,),
        in_specs=[pl.BlockSpec((C_CHUNK, BRANCH_WIDTH), lambda i: (i, ublk)),
                  pl.BlockSpec((C_CHUNK, BRANCH_WIDTH), lambda i: (i, ublk + 1)),
                  pl.BlockSpec((1, BRANCH_WIDTH), lambda i: (0, 0)),
                  pl.BlockSpec((1, BRANCH_WIDTH), lambda i: (0, 0)),
                  pl.BlockSpec((N_HEADS, C_CHUNK, C_CHUNK), lambda i: (0, 0, 0)),
                  pl.BlockSpec((C_CHUNK, N_HEADS), lambda i: (0, 0))],
        out_specs=pl.BlockSpec((C_CHUNK, BRANCH_WIDTH), lambda i: (i, 0)),
        out_shape=jax.ShapeDtypeStruct((m, BRANCH_WIDTH), BF16),
        compiler_params=_params("parallel"),
        name="spatial_gating",
    )(proj, proj, ln_g.reshape(1, -1), ln_b.reshape(1, -1), w_s, b_s.T)


def _merge_kernel(xn_ref, y_ref, wg_ref, wb_ref, o_ref, acc_ref):
    i = pl.program_id(2)
    gate = jax.nn.sigmoid(_dot(xn_ref[...], wg_ref[...]))
    term = gate * _dot(y_ref[...], wb_ref[...])

    @pl.when(i == 0)
    def _():
        acc_ref[...] = term

    @pl.when(i > 0)
    def _():
        acc_ref[...] += term

    @pl.when(i == N_BRANCHES - 1)
    def _():
        o_ref[...] = acc_ref[...].astype(o_ref.dtype)


def gated_merge(xn, ys, w_gate, w_branch, layer, *, tm, tn):
    m, d = xn.shape
    w = ys.shape[2]
    nblk = d // tn
    return pl.pallas_call(
        _merge_kernel,
        grid=(m // tm, nblk, N_BRANCHES),
        in_specs=[pl.BlockSpec((tm, d), lambda a, b, i: (a, 0)),
                  pl.BlockSpec((None, tm, w), lambda a, b, i: (i, a, 0)),
                  pl.BlockSpec((None, d, tn), lambda a, b, i: (layer, 0, i * nblk + b)),
                  pl.BlockSpec((None, None, w, tn), lambda a, b, i: (layer, i, 0, b))],
        out_specs=pl.BlockSpec((tm, tn), lambda a, b, i: (a, b)),
        out_shape=jax.ShapeDtypeStruct((m, d), BF16),
        scratch_shapes=[pltpu.VMEM((tm, tn), F32)],
        compiler_params=_params("parallel", "parallel", "arbitrary"),
        name="gated_merge",
    )(xn, ys, w_gate, w_branch)


def _ffn_up_kernel(x_ref, wg_ref, wv_ref, cw_ref, cb_ref, o_ref, g_s, *, tiles_per_seq):
    i = pl.program_id(1)
    tm = x_ref.shape[0]
    halo = SUBLANES

    @pl.when(i % tiles_per_seq == 0)
    def _():
        g_s[0:halo, :] = jnp.zeros((halo, g_s.shape[1]), F32)

    @pl.when(i % tiles_per_seq != 0)
    def _():
        g_s[0:halo, :] = g_s[tm:tm + halo, :]

    x = x_ref[...]
    g_s[halo:tm + halo, :] = _dot(x, wg_ref[...])
    val = _dot(x, wv_ref[...])
    cw = cw_ref[...]
    conv = (cw[0:1] * g_s[halo - 2:tm + halo - 2, :] + cw[1:2] * g_s[halo - 1:tm + halo - 1, :]
            + cw[2:3] * g_s[halo:tm + halo, :] + cb_ref[...])
    o_ref[...] = (conv * jax.nn.sigmoid(conv) * val).astype(o_ref.dtype)


def ffn_up(xn, w_up, layer, conv_w, conv_b, seq_len, *, tm, tn):
    m, d = xn.shape
    dff = conv_w.shape[1]
    nblk = dff // tn
    return pl.pallas_call(
        functools.partial(_ffn_up_kernel, tiles_per_seq=seq_len // tm),
        grid=(nblk, m // tm),
        in_specs=[pl.BlockSpec((tm, d), lambda j, i: (i, 0)),
                  pl.BlockSpec((None, d, tn), lambda j, i: (layer, 0, j)),
                  pl.BlockSpec((None, d, tn), lambda j, i: (layer, 0, nblk + j)),
                  pl.BlockSpec((CONV_W, tn), lambda j, i: (0, j)),
                  pl.BlockSpec((1, tn), lambda j, i: (0, j))],
        out_specs=pl.BlockSpec((tm, tn), lambda j, i: (i, j)),
        out_shape=jax.ShapeDtypeStruct((m, dff), BF16),
        scratch_shapes=[pltpu.VMEM((tm + SUBLANES, tn), F32)],
        compiler_params=_params("parallel", "arbitrary"),
        name="ffn_up_conv",
    )(xn, w_up, w_up, conv_w, conv_b.reshape(1, dff))


def kernel(x, positions, norm_mix, w_in, hgrn_lower_bounds, hgrn_out_norm, q_norm, k_norm,
           sg_ln_g, sg_ln_b, sg_w, sg_b, w_gate, w_branch, w_out, norm_ffn, w_up,
           ffn_conv_w, ffn_conv_b, w_down):
    bsz, s, d = x.shape
    m = bsz * s
    depth = w_in.shape[0]
    in_cols = w_in.shape[2]
    a_cols = 4 * BRANCH_WIDTH
    b_cols = len(B_PATTERNS) * 3 * BRANCH_WIDTH
    tables = _rope_tables(positions)
    w_in, w_gate, w_branch, w_out, w_up, w_down = (
        w.astype(BF16) for w in (w_in, w_gate, w_branch, w_out, w_up, w_down))
    xf = x.reshape(m, d)
    for l in range(depth):
        xn = rmsnorm(xf, norm_mix[l])
        proj = matmul(xn, w_in, l, tm=1024, tn=1024)
        proj3 = proj.reshape(bsz, s, in_cols)
        ya = hgrn2_mixer(proj3, hgrn_lower_bounds, hgrn_out_norm[l], l).reshape(m, BRANCH_WIDTH)
        yb = dilated_mixer(proj3, tables, q_norm[l], k_norm[l], a_cols).reshape(m, BRANCH_WIDTH)
        yc = spatial_gating(proj, sg_ln_g[l], sg_ln_b[l], sg_w[l], sg_b[l], a_cols + b_cols)
        ys = jnp.stack([ya, yb, yc], axis=0)
        merged = gated_merge(xn, ys, w_gate, w_branch, l, tm=1024, tn=512)
        xf = matmul_residual(merged, w_out, l, xf, tm=1024, tn=512, tk=d)
        xn = rmsnorm(xf, norm_ffn[l])
        h = ffn_up(xn, w_up, l, ffn_conv_w[l], ffn_conv_b[l], s, tm=1024, tn=512)
        xf = matmul_residual(h, w_down, l, xf, tm=1024, tn=1024, tk=d)
    return xf.reshape(bsz, s, d)
```

```python
import functools
import math

import jax
import jax.numpy as jnp
from jax import lax
from jax.experimental import pallas as pl
from jax.experimental.pallas import tpu as pltpu

F32 = jnp.float32
BF16 = jnp.bfloat16

EPS = 1e-6
NEG_BIG = -1e30

LANES = 128
SUBLANES = 8
VMEM_LIMIT = 56 * 1024 * 1024

HEAD_DIM = 128
N_HEADS = 8
BRANCH_WIDTH = N_HEADS * HEAD_DIM
N_BRANCHES = 3

A_CHUNK = 64
A_SUB = 16
A_STEP_ROWS = 512
A_STEP_HEADS = 8
A_UNROLL = 2

B_PATTERNS = ((128, 1), (512, 4), (2048, 16))
B_BLOCK = 128
B_TILE = 2048
B_PREP_ROWS = 256
B_BATCH = 8
ROPE_THETA = 500000.0
ROT_DIM = HEAD_DIM // 4
ROT_HALF = ROT_DIM // 2

C_CHUNK = 128
CONV_W = 3


def _params(*sem):
    return pltpu.CompilerParams(dimension_semantics=sem, vmem_limit_bytes=VMEM_LIMIT)


def _dot(a, b):
    return jnp.dot(a, b, preferred_element_type=F32)


def _dot_nt(a, b):
    return lax.dot_general(a, b, (((1,), (1,)), ((), ())), preferred_element_type=F32)


def _dot_tn(a, b):
    return lax.dot_general(a, b, (((0,), (0,)), ((), ())), preferred_element_type=F32)


def _rmsnorm_kernel(x_ref, g_ref, o_ref):
    x = x_ref[...]
    ms = jnp.mean(x * x, axis=-1, keepdims=True)
    o_ref[...] = (x * lax.rsqrt(ms + EPS) * g_ref[...]).astype(o_ref.dtype)


def rmsnorm(x, g, *, rows=256):
    m, d = x.shape
    return pl.pallas_call(
        _rmsnorm_kernel,
        grid=(m // rows,),
        in_specs=[pl.BlockSpec((rows, d), lambda i: (i, 0)),
                  pl.BlockSpec((1, d), lambda i: (0, 0))],
        out_specs=pl.BlockSpec((rows, d), lambda i: (i, 0)),
        out_shape=jax.ShapeDtypeStruct((m, d), BF16),
        compiler_params=_params("parallel"),
        name="rmsnorm",
    )(x, g.reshape(1, d))


def _weight_spec(w, layer, block, index):
    if layer is None:
        return pl.BlockSpec(block, index)
    return pl.BlockSpec((None,) + block, lambda *ids: (layer,) + index(*ids))


class SideCast:
    def __init__(self, w, layer, block, grid):
        _, r, c = w.shape
        br, bc = block
        n_col = c // bc
        n_blocks = (r // br) * n_col
        assert r % br == 0 and c % bc == 0 and n_blocks <= math.prod(grid)

        def block_index(*ids):
            t = ids[0]
            for extent, i in zip(grid[1:], ids[1:]):
                t = t * extent + i
            t = jnp.minimum(t, n_blocks - 1)
            return t // n_col, t % n_col

        self.w = w
        self.in_spec = pl.BlockSpec((None, br, bc), lambda *ids: (layer,) + block_index(*ids))
        self.out_spec = pl.BlockSpec((br, bc), block_index)
        self.out_shape = jax.ShapeDtypeStruct((r, c), BF16)


def _cast_block(src_ref, dst_ref):
    dst_ref[...] = src_ref[...].astype(dst_ref.dtype)


def _matmul_kernel(x_ref, w_ref, side_ref, o_ref, side_o_ref):
    _cast_block(side_ref, side_o_ref)
    o_ref[...] = _dot(x_ref[...], w_ref[...]).astype(o_ref.dtype)


def matmul(x, w, layer, side_w, side_block, *, tm, tn, out_dtype=F32):
    m, k = x.shape
    n = w.shape[-1]
    grid = (m // tm, n // tn)
    side = SideCast(side_w, layer, side_block, grid)
    return pl.pallas_call(
        _matmul_kernel,
        grid=grid,
        in_specs=[pl.BlockSpec((tm, k), lambda i, j: (i, 0)),
                  _weight_spec(w, layer, (k, tn), lambda i, j: (0, j)),
                  side.in_spec],
        out_specs=[pl.BlockSpec((tm, tn), lambda i, j: (i, j)), side.out_spec],
        out_shape=[jax.ShapeDtypeStruct((m, n), out_dtype), side.out_shape],
        compiler_params=_params("arbitrary", "arbitrary"),
        name="proj_matmul",
    )(x, w, side.w)


def _matmul_res_kernel(x_ref, w_ref, r_ref, o_ref):
    k = pl.program_id(2)

    @pl.when(k == 0)
    def _():
        o_ref[...] = r_ref[...]

    o_ref[...] += _dot(x_ref[...], w_ref[...])


def matmul_residual(x, w, layer, res, *, tm, tn, tk):
    m, k = x.shape
    n = w.shape[-1]
    return pl.pallas_call(
        _matmul_res_kernel,
        grid=(m // tm, n // tn, k // tk),
        in_specs=[pl.BlockSpec((tm, tk), lambda i, j, kk: (i, kk)),
                  _weight_spec(w, layer, (tk, tn), lambda i, j, kk: (kk, j)),
                  pl.BlockSpec((tm, tn), lambda i, j, kk: (i, j))],
        out_specs=pl.BlockSpec((tm, tn), lambda i, j, kk: (i, j)),
        out_shape=jax.ShapeDtypeStruct((m, n), F32),
        compiler_params=_params("parallel", "parallel", "arbitrary"),
        name="matmul_residual",
    )(x, w, res)


def _hgrn_kernel(lbp_ref, gain_ref, q_ref, f_ref, i_ref, g_ref, o_ref, st_ref, ck_s, v_s,
                 *, layer, n_chunks, n_heads):
    n = pl.program_id(2)

    @pl.when(n == 0)
    def _():
        st_ref[...] = jnp.zeros_like(st_ref)

    lbp = lbp_ref[...]
    e = jnp.exp(lbp - jnp.max(lbp, axis=0, keepdims=True))
    sm = e / jnp.sum(e, axis=0, keepdims=True)
    cs = sm[0:1]
    for li in range(1, layer + 1):
        cs = cs + sm[li:li + 1]
    lb_all = cs - sm[0:1]

    gain = gain_ref[...]
    n_sub = A_CHUNK // A_SUB
    row = lax.broadcasted_iota(jnp.int32, (A_CHUNK, A_CHUNK), 0)
    col = lax.broadcasted_iota(jnp.int32, (A_CHUNK, A_CHUNK), 1)
    tri = (row >= col).astype(F32)
    off_mask = (row // A_SUB) > (col // A_SUB)
    rblk = lax.broadcasted_iota(jnp.int32, (A_CHUNK, HEAD_DIM), 0) // A_SUB
    half_row = lax.broadcasted_iota(jnp.int32, (SUBLANES, HEAD_DIM), 0)
    scale = HEAD_DIM ** -0.5

    def head_chunk(hh, r0):
        sl = pl.ds(r0, A_CHUNK)
        hs = slice(hh * HEAD_DIM, (hh + 1) * HEAD_DIM)
        lb = lb_all[:, hs]
        q = q_ref[sl, hs] * scale
        v = i_ref[sl, hs]
        f = lb + (1.0 - lb) * jax.nn.sigmoid(f_ref[sl, hs])
        kk = 1.0 - f
        cum = jnp.dot(tri, jnp.log2(f), precision=lax.Precision.HIGHEST,
                      preferred_element_type=F32)
        ends = [cum[(j + 1) * A_SUB - 1:(j + 1) * A_SUB, :] for j in range(n_sub)]
        last = ends[-1]
        eblk = jnp.concatenate([jnp.broadcast_to(ej, (A_SUB, HEAD_DIM)) for ej in ends], axis=0)
        khat = kk * jnp.exp2(eblk - cum)
        st = st_ref[hh]

        o = _dot_nt((q * jnp.exp2(cum)).astype(BF16), st.astype(BF16))

        s_off = jnp.zeros((A_CHUNK, A_CHUNK), F32)
        for j in range(n_sub - 1):
            qj = q * jnp.exp2(jnp.minimum(cum - ends[j], 0.0))
            kj = jnp.where(rblk == j, khat, 0.0)
            s_off = s_off + _dot_nt(qj.astype(BF16), kj.astype(BF16))
        s_off = jnp.where(off_mask, s_off, 0.0)
        o = o + _dot(s_off.astype(BF16), v.astype(BF16))

        ck_s[hh] = cum - jnp.log2(kk)
        v_s[hh] = v
        diag = []
        for b in range(n_sub):
            base = b * A_SUB
            q_lo, q_hi = q[base:base + SUBLANES], q[base + SUBLANES:base + A_SUB]
            c_lo, c_hi = cum[base:base + SUBLANES], cum[base + SUBLANES:base + A_SUB]
            o_lo = jnp.zeros((SUBLANES, HEAD_DIM), F32)
            o_hi = jnp.zeros((SUBLANES, HEAD_DIM), F32)
            for s in range(A_SUB):
                r = base + s
                c_row = ck_s[hh, r:r + 1, :]
                v_row = v_s[hh, r:r + 1, :]
                if s < SUBLANES:
                    dec = jnp.exp2(jnp.where(half_row >= s, c_lo - c_row, NEG_BIG))
                    o_lo = o_lo + jnp.sum(q_lo * dec, axis=-1, keepdims=True) * v_row
                    dec = jnp.exp2(c_hi - c_row)
                else:
                    dec = jnp.exp2(jnp.where(half_row >= s - SUBLANES, c_hi - c_row, NEG_BIG))
                o_hi = o_hi + jnp.sum(q_hi * dec, axis=-1, keepdims=True) * v_row
            diag += [o_lo, o_hi]
        o = o + jnp.concatenate(diag, axis=0)

        ktil = khat * jnp.exp2(last - eblk)
        st_ref[hh] = st * jnp.exp2(last) + _dot_tn(v.astype(BF16), ktil.astype(BF16))

        ms = jnp.mean(o * o, axis=-1, keepdims=True)
        on = o * lax.rsqrt(ms + EPS) * gain
        g = g_ref[sl, hs]
        o_ref[sl, hs] = (on * (g * jax.nn.sigmoid(g))).astype(o_ref.dtype)

    def chunk(c, carry):
        r0 = pl.multiple_of(c * A_CHUNK, A_CHUNK)
        for hh in range(n_heads):
            head_chunk(hh, r0)
        return carry

    lax.fori_loop(0, n_chunks, chunk, 0, unroll=A_UNROLL)


def hgrn2_mixer(proj3, lower_bounds, out_gain, layer):
    bsz, s, _ = proj3.shape
    depth = lower_bounds.shape[0]
    ts = min(A_STEP_ROWS, s)
    hb = A_STEP_HEADS
    n_hg = N_HEADS // hb
    wide = hb * HEAD_DIM
    blk = lambda part: pl.BlockSpec((None, ts, wide), lambda b, h, n: (b, n, part * n_hg + h))
    return pl.pallas_call(
        functools.partial(_hgrn_kernel, layer=layer, n_chunks=ts // A_CHUNK, n_heads=hb),
        grid=(bsz, n_hg, s // ts),
        in_specs=[pl.BlockSpec((depth, wide), lambda b, h, n: (0, h)),
                  pl.BlockSpec((1, HEAD_DIM), lambda b, h, n: (0, 0)),
                  blk(0), blk(1), blk(2), blk(3)],
        out_specs=pl.BlockSpec((None, ts, wide), lambda b, h, n: (b, n, h)),
        out_shape=jax.ShapeDtypeStruct((bsz, s, BRANCH_WIDTH), BF16),
        scratch_shapes=[pltpu.VMEM((hb, HEAD_DIM, HEAD_DIM), F32)]
                       + [pltpu.VMEM((hb, A_CHUNK, HEAD_DIM), F32)] * 2,
        compiler_params=_params("parallel", "parallel", "arbitrary"),
        name="hgrn2",
    )(lower_bounds, out_gain.reshape(1, HEAD_DIM), proj3, proj3, proj3, proj3)


def _rope_tables(positions):
    inv = jnp.power(jnp.float32(ROPE_THETA), -jnp.arange(0, ROT_DIM, 2, dtype=F32) / ROT_DIM)
    ang = positions.astype(F32)[..., None] * inv
    cos = jnp.cos(ang)
    sin = jnp.sin(ang)
    pad = jnp.zeros(ang.shape[:-1] + (HEAD_DIM - ROT_DIM,), F32)
    cos_f = jnp.concatenate([cos, cos, pad + 1.0], axis=-1)
    sin_f = jnp.concatenate([sin, sin, pad], axis=-1)
    return cos_f, sin_f


def _dilated_kernel(qg_ref, kg_ref, cos_ref, sin_ref, *rest):
    n_grp = len(B_PATTERNS)
    qkv = rest[:3 * n_grp]
    y_ref = rest[3 * n_grp]
    scratch = list(rest[3 * n_grp + 1:])
    qn_s, kn_s, kc, vc, o_s, lse_s = (scratch[i * n_grp:(i + 1) * n_grp] for i in range(6))
    s_s, p_s, vb_s = scratch[6 * n_grp:]
    n = pl.program_id(2)

    @pl.when(n == 0)
    def _():
        for g in range(n_grp):
            kc[g][...] = jnp.zeros_like(kc[g])
            vc[g][...] = jnp.zeros_like(vc[g])

    scale = HEAD_DIM ** -0.5

    src = lax.broadcasted_iota(jnp.int32, (HEAD_DIM, HEAD_DIM), 0)
    dst = lax.broadcasted_iota(jnp.int32, (HEAD_DIM, HEAD_DIM), 1)
    rot = jnp.where(jnp.logical_and(dst < ROT_HALF, src == dst + ROT_HALF), -1.0,
                    jnp.where(jnp.logical_and(jnp.logical_and(dst >= ROT_HALF, dst < ROT_DIM),
                                              src == dst - ROT_HALF), 1.0, 0.0)).astype(BF16)

    def norm_rope(x, gain, cos, sin):
        ms = jnp.mean(x * x, axis=-1, keepdims=True)
        xn = x * lax.rsqrt(ms + EPS) * gain
        hi = xn.astype(BF16)
        lo = (xn - hi.astype(F32)).astype(BF16)
        return xn * cos + (_dot(hi, rot) + _dot(lo, rot)) * sin

    def prep(c, carry):
        rows = pl.ds(pl.multiple_of(c * B_PREP_ROWS, B_PREP_ROWS), B_PREP_ROWS)
        cos, sin = cos_ref[rows, :], sin_ref[rows, :]
        for g in range(n_grp):
            q_ref, k_ref = qkv[3 * g], qkv[3 * g + 1]
            qn_s[g][rows, :] = norm_rope(q_ref[rows, :], qg_ref[g:g + 1, :], cos, sin) * scale
            kn_s[g][rows, :] = norm_rope(k_ref[rows, :], kg_ref[g:g + 1, :], cos, sin)
        return carry

    lax.fori_loop(0, B_TILE // B_PREP_ROWS, prep, 0)

    qi = lax.broadcasted_iota(jnp.int32, (B_BLOCK, 2 * B_BLOCK), 0)
    ci = lax.broadcasted_iota(jnp.int32, (B_BLOCK, 2 * B_BLOCK), 1)
    cur_ok = jnp.logical_and(ci >= B_BLOCK, ci - B_BLOCK <= qi)
    prev_ok = jnp.logical_and(ci < B_BLOCK, ci >= qi)
    mask_inner = jnp.logical_or(cur_ok, prev_ok)
    mask_first = jnp.logical_or(cur_ok, jnp.logical_and(prev_ok, n > 0))
    ones = jnp.ones((2 * B_BLOCK, HEAD_DIM), BF16)

    for g, (win, dil) in enumerate(B_PATTERNS):
        v_ref = qkv[3 * g + 2]
        n_blk = B_TILE // (B_BLOCK * dil)
        blocks = [(r, j) for r in range(dil) for j in range(n_blk)]
        k_prev = v_prev = None
        for b0 in range(0, len(blocks), B_BATCH):
            batch = blocks[b0:b0 + B_BATCH]
            rows_of = []
            for i, (r, j) in enumerate(batch):
                start = j * B_BLOCK * dil + r
                rows = pl.ds(start, B_BLOCK) if dil == 1 else pl.ds(start, B_BLOCK, stride=dil)
                rows_of.append(rows)
                blk = slice(i * B_BLOCK, (i + 1) * B_BLOCK)
                if j == 0:
                    k_prev, v_prev = kc[g][r], vc[g][r]
                k_cur = kn_s[g][rows, :].astype(BF16)
                v_cur = v_ref[rows, :].astype(BF16)
                q = qn_s[g][rows, :].astype(BF16)
                s = _dot_nt(q, jnp.concatenate([k_prev, k_cur], axis=0))
                s_s[blk, :] = jnp.where(mask_first if j == 0 else mask_inner, s, NEG_BIG)
                vb_s[i, 0:B_BLOCK, :] = v_prev
                vb_s[i, B_BLOCK:2 * B_BLOCK, :] = v_cur
                if j == n_blk - 1:
                    kc[g][r] = k_cur
                    vc[g][r] = v_cur
                k_prev, v_prev = k_cur, v_cur
            live = slice(0, len(batch) * B_BLOCK)
            m = jnp.max(jnp.maximum(s_s[live, 0:B_BLOCK], s_s[live, B_BLOCK:2 * B_BLOCK]),
                        axis=-1, keepdims=True)
            p_s[live, :] = jnp.exp(s_s[live, :] - m).astype(BF16)
            for i, rows in enumerate(rows_of):
                blk = slice(i * B_BLOCK, (i + 1) * B_BLOCK)
                pv = _dot(p_s[blk, :], jnp.concatenate([vb_s[i], ones], axis=1))
                den = pv[:, HEAD_DIM:]
                o_s[g][rows, :] = pv[:, :HEAD_DIM] / den
                lse_s[g][rows, :] = m[blk] + jnp.log(den)

    def combine(c, carry):
        rows = pl.ds(pl.multiple_of(c * B_BLOCK, B_BLOCK), B_BLOCK)
        lse = [ref[rows, :] for ref in lse_s]
        m = functools.reduce(jnp.maximum, lse)
        w = [jnp.exp(x - m) for x in lse]
        num = functools.reduce(jnp.add, [wg * ref[rows, :] for wg, ref in zip(w, o_s)])
        y_ref[rows, :] = (num / functools.reduce(jnp.add, w)).astype(y_ref.dtype)
        return carry

    lax.fori_loop(0, B_TILE // B_BLOCK, combine, 0)


def dilated_mixer(proj3, tables, q_gain, k_gain, col0):
    bsz, s, _ = proj3.shape
    n_grp = len(B_PATTERNS)
    assert s % B_TILE == 0

    def part(g, p):
        base = (col0 + (3 * g + p) * BRANCH_WIDTH) // HEAD_DIM
        return pl.BlockSpec((None, B_TILE, HEAD_DIM), lambda b, h, n: (b, n, base + h))

    tab = pl.BlockSpec((None, B_TILE, HEAD_DIM), lambda b, h, n: (b, n, 0))
    gain = pl.BlockSpec((n_grp, HEAD_DIM), lambda b, h, n: (0, 0))
    for win, dil in B_PATTERNS:
        assert win // dil == B_BLOCK and B_TILE % (B_BLOCK * dil) == 0
    token_f32 = [pltpu.VMEM((B_TILE, HEAD_DIM), F32)] * n_grp
    carry = [pltpu.VMEM((dil, B_BLOCK, HEAD_DIM), BF16) for _, dil in B_PATTERNS]
    scratch = (token_f32 + token_f32 + carry + carry + token_f32 + token_f32
               + [pltpu.VMEM((B_BATCH * B_BLOCK, 2 * B_BLOCK), F32),
                  pltpu.VMEM((B_BATCH * B_BLOCK, 2 * B_BLOCK), BF16),
                  pltpu.VMEM((B_BATCH, 2 * B_BLOCK, HEAD_DIM), BF16)])
    return pl.pallas_call(
        _dilated_kernel,
        grid=(bsz, N_HEADS, s // B_TILE),
        in_specs=[gain, gain, tab, tab]
                 + [part(g, p) for g in range(n_grp) for p in range(3)],
        out_specs=pl.BlockSpec((None, B_TILE, HEAD_DIM), lambda b, h, n: (b, n, h)),
        out_shape=jax.ShapeDtypeStruct((bsz, s, BRANCH_WIDTH), BF16),
        scratch_shapes=scratch,
        compiler_params=_params("parallel", "parallel", "arbitrary"),
        name="dilated_attn",
    )(q_gain, k_gain, *tables, *([proj3] * (3 * n_grp)))


def _spatial_kernel(u_ref, v_ref, lng_ref, lnb_ref, w_ref, b_ref, o_ref):
    inv_sqrt2 = 1.0 / math.sqrt(2.0)

    def gelu(x):
        return 0.5 * x * (1.0 + lax.erf(x * inv_sqrt2))

    v = gelu(v_ref[...])
    mu = jnp.mean(v, axis=-1, keepdims=True)
    vc = v - mu
    var = jnp.mean(vc * vc, axis=-1, keepdims=True)
    vn = vc * lax.rsqrt(var + EPS) * lng_ref[...] + lnb_ref[...]
    row = lax.broadcasted_iota(jnp.int32, (C_CHUNK, C_CHUNK), 0)
    col = lax.broadcasted_iota(jnp.int32, (C_CHUNK, C_CHUNK), 1)
    causal = row >= col
    bias = b_ref[...]
    for g in range(N_HEADS):
        gs = slice(g * HEAD_DIM, (g + 1) * HEAD_DIM)
        wm = jnp.where(causal, w_ref[g], 0.0).astype(BF16)
        mixed = _dot(wm, vn[:, gs].astype(BF16)) + bias[:, g:g + 1]
        o_ref[:, gs] = (gelu(u_ref[:, gs]) * mixed).astype(o_ref.dtype)


def spatial_gating(proj, ln_g, ln_b, w_s, b_s, col0):
    m, _ = proj.shape
    ublk = col0 // BRANCH_WIDTH
    return pl.pallas_call(
        _spatial_kernel,
        grid=(m // C_CHUNK,),
        in_specs=[pl.BlockSpec((C_CHUNK, BRANCH_WIDTH), lambda i: (i, ublk)),
                  pl.BlockSpec((C_CHUNK, BRANCH_WIDTH), lambda i: (i, ublk + 1)),
                  pl.BlockSpec((1, BRANCH_WIDTH), lambda i: (0, 0)),
                  pl.BlockSpec((1, BRANCH_WIDTH), lambda i: (0, 0)),
                  pl.BlockSpec((N_HEADS, C_CHUNK, C_CHUNK), lambda i: (0, 0, 0)),
                  pl.BlockSpec((C_CHUNK, N_HEADS), lambda i: (0, 0))],
        out_specs=pl.BlockSpec((C_CHUNK, BRANCH_WIDTH), lambda i: (i, 0)),
        out_shape=jax.ShapeDtypeStruct((m, BRANCH_WIDTH), BF16),
        compiler_params=_params("parallel"),
        name="spatial_gating",
    )(proj, proj, ln_g.reshape(1, -1), ln_b.reshape(1, -1), w_s, b_s.T)


def _merge_kernel(xn_ref, y_ref, wg_ref, wb_ref, side_ref, o_ref, side_o_ref, acc_ref):
    i = pl.program_id(2)
    _cast_block(side_ref, side_o_ref)
    gate = jax.nn.sigmoid(_dot(xn_ref[...], wg_ref[...]))
    term = gate * _dot(y_ref[...], wb_ref[...])

    @pl.when(i == 0)
    def _():
        acc_ref[...] = term

    @pl.when(i > 0)
    def _():
        acc_ref[...] += term

    @pl.when(i == N_BRANCHES - 1)
    def _():
        o_ref[...] = acc_ref[...].astype(o_ref.dtype)


def gated_merge(xn, ys, w_gate, w_branch, layer, side_w, side_block, *, tm, tn):
    m, d = xn.shape
    w = ys.shape[2]
    nblk = d // tn
    grid = (m // tm, nblk, N_BRANCHES)
    side = SideCast(side_w, layer, side_block, grid)
    return pl.pallas_call(
        _merge_kernel,
        grid=grid,
        in_specs=[pl.BlockSpec((tm, d), lambda a, b, i: (a, 0)),
                  pl.BlockSpec((None, tm, w), lambda a, b, i: (i, a, 0)),
                  pl.BlockSpec((d, tn), lambda a, b, i: (0, i * nblk + b)),
                  pl.BlockSpec((None, None, w, tn), lambda a, b, i: (layer, i, 0, b)),
                  side.in_spec],
        out_specs=[pl.BlockSpec((tm, tn), lambda a, b, i: (a, b)), side.out_spec],
        out_shape=[jax.ShapeDtypeStruct((m, d), BF16), side.out_shape],
        scratch_shapes=[pltpu.VMEM((tm, tn), F32)],
        compiler_params=_params("arbitrary", "arbitrary", "arbitrary"),
        name="gated_merge",
    )(xn, ys, w_gate, w_branch, side.w)


def _ffn_up_kernel(x_ref, wg_ref, wv_ref, cw_ref, cb_ref, side_ref, o_ref, side_o_ref, g_s,
                   *, tiles_per_seq):
    i = pl.program_id(1)
    tm = x_ref.shape[0]
    halo = SUBLANES

    @pl.when(i % tiles_per_seq == 0)
    def _():
        g_s[0:halo, :] = jnp.zeros((halo, g_s.shape[1]), F32)

    @pl.when(i % tiles_per_seq != 0)
    def _():
        g_s[0:halo, :] = g_s[tm:tm + halo, :]

    _cast_block(side_ref, side_o_ref)
    x = x_ref[...]
    g_s[halo:tm + halo, :] = _dot(x, wg_ref[...])
    val = _dot(x, wv_ref[...])
    cw = cw_ref[...]
    conv = (cw[0:1] * g_s[halo - 2:tm + halo - 2, :] + cw[1:2] * g_s[halo - 1:tm + halo - 1, :]
            + cw[2:3] * g_s[halo:tm + halo, :] + cb_ref[...])
    o_ref[...] = (conv * jax.nn.sigmoid(conv) * val).astype(o_ref.dtype)


def ffn_up(xn, w_up, conv_w, conv_b, seq_len, layer, side_w, side_block, *, tm, tn):
    m, d = xn.shape
    dff = conv_w.shape[1]
    nblk = dff // tn
    grid = (nblk, m // tm)
    side = SideCast(side_w, layer, side_block, grid)
    return pl.pallas_call(
        functools.partial(_ffn_up_kernel, tiles_per_seq=seq_len // tm),
        grid=grid,
        in_specs=[pl.BlockSpec((tm, d), lambda j, i: (i, 0)),
                  pl.BlockSpec((d, tn), lambda j, i: (0, j)),
                  pl.BlockSpec((d, tn), lambda j, i: (0, nblk + j)),
                  pl.BlockSpec((CONV_W, tn), lambda j, i: (0, j)),
                  pl.BlockSpec((1, tn), lambda j, i: (0, j)),
                  side.in_spec],
        out_specs=[pl.BlockSpec((tm, tn), lambda j, i: (i, j)), side.out_spec],
        out_shape=[jax.ShapeDtypeStruct((m, dff), BF16), side.out_shape],
        scratch_shapes=[pltpu.VMEM((tm + SUBLANES, tn), F32)],
        compiler_params=_params("arbitrary", "arbitrary"),
        name="ffn_up_conv",
    )(xn, w_up, w_up, conv_w, conv_b.reshape(1, dff), side.w)


def kernel(x, positions, norm_mix, w_in, hgrn_lower_bounds, hgrn_out_norm, q_norm, k_norm,
           sg_ln_g, sg_ln_b, sg_w, sg_b, w_gate, w_branch, w_out, norm_ffn, w_up,
           ffn_conv_w, ffn_conv_b, w_down):
    bsz, s, d = x.shape
    m = bsz * s
    depth = w_in.shape[0]
    in_cols = w_in.shape[2]
    a_cols = 4 * BRANCH_WIDTH
    b_cols = len(B_PATTERNS) * 3 * BRANCH_WIDTH
    tables = _rope_tables(positions)
    w_in, w_branch, w_out = (w.astype(BF16) for w in (w_in, w_branch, w_out))
    xf = x.reshape(m, d)
    for l in range(depth):
        xn = rmsnorm(xf, norm_mix[l])
        proj, w_gate_l = matmul(xn, w_in, l, w_gate, (512, 1024), tm=1024, tn=1024)
        proj3 = proj.reshape(bsz, s, in_cols)
        ya = hgrn2_mixer(proj3, hgrn_lower_bounds, hgrn_out_norm[l], l).reshape(m, BRANCH_WIDTH)
        yb = dilated_mixer(proj3, tables, q_norm[l], k_norm[l], a_cols).reshape(m, BRANCH_WIDTH)
        yc = spatial_gating(proj, sg_ln_g[l], sg_ln_b[l], sg_w[l], sg_b[l], a_cols + b_cols)
        ys = jnp.stack([ya, yb, yc], axis=0)
        merged, w_up_l = gated_merge(xn, ys, w_gate_l, w_branch, l, w_up, (512, 1024),
                                     tm=1024, tn=512)
        xf = matmul_residual(merged, w_out, l, xf, tm=1024, tn=1024, tk=d)
        xn = rmsnorm(xf, norm_ffn[l])
        h, w_down_l = ffn_up(xn, w_up_l, ffn_conv_w[l], ffn_conv_b[l], s, l, w_down, (512, 512),
                             tm=1024, tn=512)
        xf = matmul_residual(h, w_down_l, None, xf, tm=1024, tn=1024, tk=d)
    return xf.reshape(bsz, s, d)
```

```python
import functools
import math

import jax
import jax.numpy as jnp
from jax import lax
from jax.experimental import pallas as pl
from jax.experimental.pallas import tpu as pltpu

F32 = jnp.float32
BF16 = jnp.bfloat16

EPS = 1e-6
NEG_BIG = -1e30

LANES = 128
SUBLANES = 8
VMEM_LIMIT = 56 * 1024 * 1024

HEAD_DIM = 128
N_HEADS = 8
BRANCH_WIDTH = N_HEADS * HEAD_DIM
N_BRANCHES = 3

A_CHUNK = 64
A_SUB = 16
A_STEP_ROWS = 512
A_STEP_HEADS = 8
A_UNROLL = 2

B_PATTERNS = ((128, 1), (512, 4), (2048, 16))
B_BLOCK = 128
B_TILE = 2048
B_PREP_ROWS = 256
B_BATCH = 8
ROPE_THETA = 500000.0
ROT_DIM = HEAD_DIM // 4
ROT_HALF = ROT_DIM // 2

C_CHUNK = 128
CONV_W = 3


def _params(*sem):
    return pltpu.CompilerParams(dimension_semantics=sem, vmem_limit_bytes=VMEM_LIMIT)


def _dot(a, b):
    return jnp.dot(a, b, preferred_element_type=F32)


def _dot_nt(a, b):
    return lax.dot_general(a, b, (((1,), (1,)), ((), ())), preferred_element_type=F32)


def _dot_tn(a, b):
    return lax.dot_general(a, b, (((0,), (0,)), ((), ())), preferred_element_type=F32)


def _rmsnorm_kernel(x_ref, g_ref, o_ref):
    x = x_ref[...]
    ms = jnp.mean(x * x, axis=-1, keepdims=True)
    o_ref[...] = (x * lax.rsqrt(ms + EPS) * g_ref[...]).astype(o_ref.dtype)


def rmsnorm(x, g, *, rows=256):
    m, d = x.shape
    return pl.pallas_call(
        _rmsnorm_kernel,
        grid=(m // rows,),
        in_specs=[pl.BlockSpec((rows, d), lambda i: (i, 0)),
                  pl.BlockSpec((1, d), lambda i: (0, 0))],
        out_specs=pl.BlockSpec((rows, d), lambda i: (i, 0)),
        out_shape=jax.ShapeDtypeStruct((m, d), BF16),
        compiler_params=_params("parallel"),
        name="rmsnorm",
    )(x, g.reshape(1, d))


def _weight_spec(w, layer, block, index):
    if layer is None:
        return pl.BlockSpec(block, index)
    return pl.BlockSpec((None,) + block, lambda *ids: (layer,) + index(*ids))


class SideCast:
    def __init__(self, w, layer, block, grid):
        _, r, c = w.shape
        br, bc = block
        n_col = c // bc
        n_blocks = (r // br) * n_col
        assert r % br == 0 and c % bc == 0 and n_blocks <= math.prod(grid)

        def block_index(*ids):
            t = ids[0]
            for extent, i in zip(grid[1:], ids[1:]):
                t = t * extent + i
            t = jnp.minimum(t, n_blocks - 1)
            return t // n_col, t % n_col

        self.w = w
        self.in_spec = pl.BlockSpec((None, br, bc), lambda *ids: (layer,) + block_index(*ids))
        self.out_spec = pl.BlockSpec((br, bc), block_index)
        self.out_shape = jax.ShapeDtypeStruct((r, c), BF16)


def _side_casts(sides, grid):
    return [SideCast(w, layer, block, grid) for w, layer, block in sides]


def _with_side_casts(body, n_in, n_out, n_side):
    def wrapped(*refs):
        ins, rest = refs[:n_in], refs[n_in:]
        side_in, rest = rest[:n_side], rest[n_side:]
        outs, rest = rest[:n_out], rest[n_out:]
        side_out, scratch = rest[:n_side], rest[n_side:]

        def cast_sides():
            for src, dst in zip(side_in, side_out):
                dst[...] = src[...].astype(dst.dtype)

        body(*ins, *outs, *scratch, cast_sides=cast_sides)

    return wrapped


def _matmul_kernel(x_ref, w_ref, o_ref, *, cast_sides):
    cast_sides()
    o_ref[...] = _dot(x_ref[...], w_ref[...]).astype(o_ref.dtype)


def matmul(x, w, layer, sides, *, tm, tn, out_dtype=F32):
    m, k = x.shape
    n = w.shape[-1]
    grid = (m // tm, n // tn)
    side = _side_casts(sides, grid)
    return pl.pallas_call(
        _with_side_casts(_matmul_kernel, 2, 1, len(side)),
        grid=grid,
        in_specs=[pl.BlockSpec((tm, k), lambda i, j: (i, 0)),
                  _weight_spec(w, layer, (k, tn), lambda i, j: (0, j))]
                 + [sc.in_spec for sc in side],
        out_specs=[pl.BlockSpec((tm, tn), lambda i, j: (i, j))] + [sc.out_spec for sc in side],
        out_shape=[jax.ShapeDtypeStruct((m, n), out_dtype)] + [sc.out_shape for sc in side],
        compiler_params=_params("arbitrary", "arbitrary"),
        name="proj_matmul",
    )(x, w, *[sc.w for sc in side])


def _matmul_res_kernel(x_ref, w_ref, r_ref, o_ref):
    k = pl.program_id(2)

    @pl.when(k == 0)
    def _():
        o_ref[...] = r_ref[...]

    o_ref[...] += _dot(x_ref[...], w_ref[...])


def matmul_residual(x, w, layer, res, *, tm, tn, tk):
    m, k = x.shape
    n = w.shape[-1]
    return pl.pallas_call(
        _matmul_res_kernel,
        grid=(m // tm, n // tn, k // tk),
        in_specs=[pl.BlockSpec((tm, tk), lambda i, j, kk: (i, kk)),
                  _weight_spec(w, layer, (tk, tn), lambda i, j, kk: (kk, j)),
                  pl.BlockSpec((tm, tn), lambda i, j, kk: (i, j))],
        out_specs=pl.BlockSpec((tm, tn), lambda i, j, kk: (i, j)),
        out_shape=jax.ShapeDtypeStruct((m, n), F32),
        compiler_params=_params("parallel", "parallel", "arbitrary"),
        name="matmul_residual",
    )(x, w, res)


def _hgrn_kernel(lbp_ref, gain_ref, q_ref, f_ref, i_ref, g_ref, o_ref, st_ref, ck_s, v_s,
                 *, layer, n_chunks, n_heads):
    n = pl.program_id(2)

    @pl.when(n == 0)
    def _():
        st_ref[...] = jnp.zeros_like(st_ref)

    lbp = lbp_ref[...]
    e = jnp.exp(lbp - jnp.max(lbp, axis=0, keepdims=True))
    sm = e / jnp.sum(e, axis=0, keepdims=True)
    cs = sm[0:1]
    for li in range(1, layer + 1):
        cs = cs + sm[li:li + 1]
    lb_all = cs - sm[0:1]

    gain = gain_ref[...]
    n_sub = A_CHUNK // A_SUB
    row = lax.broadcasted_iota(jnp.int32, (A_CHUNK, A_CHUNK), 0)
    col = lax.broadcasted_iota(jnp.int32, (A_CHUNK, A_CHUNK), 1)
    tri = (row >= col).astype(F32)
    off_mask = (row // A_SUB) > (col // A_SUB)
    rblk = lax.broadcasted_iota(jnp.int32, (A_CHUNK, HEAD_DIM), 0) // A_SUB
    half_row = lax.broadcasted_iota(jnp.int32, (SUBLANES, HEAD_DIM), 0)
    scale = HEAD_DIM ** -0.5

    def head_chunk(hh, r0):
        sl = pl.ds(r0, A_CHUNK)
        hs = slice(hh * HEAD_DIM, (hh + 1) * HEAD_DIM)
        lb = lb_all[:, hs]
        q = q_ref[sl, hs] * scale
        v = i_ref[sl, hs]
        f = lb + (1.0 - lb) * jax.nn.sigmoid(f_ref[sl, hs])
        kk = 1.0 - f
        cum = jnp.dot(tri, jnp.log2(f), precision=lax.Precision.HIGHEST,
                      preferred_element_type=F32)
        ends = [cum[(j + 1) * A_SUB - 1:(j + 1) * A_SUB, :] for j in range(n_sub)]
        last = ends[-1]
        eblk = jnp.concatenate([jnp.broadcast_to(ej, (A_SUB, HEAD_DIM)) for ej in ends], axis=0)
        khat = kk * jnp.exp2(eblk - cum)
        st = st_ref[hh]

        o = _dot_nt((q * jnp.exp2(cum)).astype(BF16), st.astype(BF16))

        s_off = jnp.zeros((A_CHUNK, A_CHUNK), F32)
        for j in range(n_sub - 1):
            qj = q * jnp.exp2(jnp.minimum(cum - ends[j], 0.0))
            kj = jnp.where(rblk == j, khat, 0.0)
            s_off = s_off + _dot_nt(qj.astype(BF16), kj.astype(BF16))
        s_off = jnp.where(off_mask, s_off, 0.0)
        o = o + _dot(s_off.astype(BF16), v.astype(BF16))

        ck_s[hh] = cum - jnp.log2(kk)
        v_s[hh] = v
        diag = []
        for b in range(n_sub):
            base = b * A_SUB
            q_lo, q_hi = q[base:base + SUBLANES], q[base + SUBLANES:base + A_SUB]
            c_lo, c_hi = cum[base:base + SUBLANES], cum[base + SUBLANES:base + A_SUB]
            o_lo = jnp.zeros((SUBLANES, HEAD_DIM), F32)
            o_hi = jnp.zeros((SUBLANES, HEAD_DIM), F32)
            for s in range(A_SUB):
                r = base + s
                c_row = ck_s[hh, r:r + 1, :]
                v_row = v_s[hh, r:r + 1, :]
                if s < SUBLANES:
                    dec = jnp.exp2(jnp.where(half_row >= s, c_lo - c_row, NEG_BIG))
                    o_lo = o_lo + jnp.sum(q_lo * dec, axis=-1, keepdims=True) * v_row
                    dec = jnp.exp2(c_hi - c_row)
                else:
                    dec = jnp.exp2(jnp.where(half_row >= s - SUBLANES, c_hi - c_row, NEG_BIG))
                o_hi = o_hi + jnp.sum(q_hi * dec, axis=-1, keepdims=True) * v_row
            diag += [o_lo, o_hi]
        o = o + jnp.concatenate(diag, axis=0)

        ktil = khat * jnp.exp2(last - eblk)
        st_ref[hh] = st * jnp.exp2(last) + _dot_tn(v.astype(BF16), ktil.astype(BF16))

        ms = jnp.mean(o * o, axis=-1, keepdims=True)
        on = o * lax.rsqrt(ms + EPS) * gain
        g = g_ref[sl, hs]
        o_ref[sl, hs] = (on * (g * jax.nn.sigmoid(g))).astype(o_ref.dtype)

    def chunk(c, carry):
        r0 = pl.multiple_of(c * A_CHUNK, A_CHUNK)
        for hh in range(n_heads):
            head_chunk(hh, r0)
        return carry

    lax.fori_loop(0, n_chunks, chunk, 0, unroll=A_UNROLL)


def hgrn2_mixer(proj3, lower_bounds, out_gain, layer):
    bsz, s, _ = proj3.shape
    depth = lower_bounds.shape[0]
    ts = min(A_STEP_ROWS, s)
    hb = A_STEP_HEADS
    n_hg = N_HEADS // hb
    wide = hb * HEAD_DIM
    blk = lambda part: pl.BlockSpec((None, ts, wide), lambda b, h, n: (b, n, part * n_hg + h))
    return pl.pallas_call(
        functools.partial(_hgrn_kernel, layer=layer, n_chunks=ts // A_CHUNK, n_heads=hb),
        grid=(bsz, n_hg, s // ts),
        in_specs=[pl.BlockSpec((depth, wide), lambda b, h, n: (0, h)),
                  pl.BlockSpec((1, HEAD_DIM), lambda b, h, n: (0, 0)),
                  blk(0), blk(1), blk(2), blk(3)],
        out_specs=pl.BlockSpec((None, ts, wide), lambda b, h, n: (b, n, h)),
        out_shape=jax.ShapeDtypeStruct((bsz, s, BRANCH_WIDTH), BF16),
        scratch_shapes=[pltpu.VMEM((hb, HEAD_DIM, HEAD_DIM), F32)]
                       + [pltpu.VMEM((hb, A_CHUNK, HEAD_DIM), F32)] * 2,
        compiler_params=_params("parallel", "parallel", "arbitrary"),
        name="hgrn2",
    )(lower_bounds, out_gain.reshape(1, HEAD_DIM), proj3, proj3, proj3, proj3)


def _rope_tables(positions):
    inv = jnp.power(jnp.float32(ROPE_THETA), -jnp.arange(0, ROT_DIM, 2, dtype=F32) / ROT_DIM)
    ang = positions.astype(F32)[..., None] * inv
    cos = jnp.cos(ang)
    sin = jnp.sin(ang)
    pad = jnp.zeros(ang.shape[:-1] + (HEAD_DIM - ROT_DIM,), F32)
    cos_f = jnp.concatenate([cos, cos, pad + 1.0], axis=-1)
    sin_f = jnp.concatenate([sin, sin, pad], axis=-1)
    return cos_f, sin_f


def _dilated_kernel(qg_ref, kg_ref, cos_ref, sin_ref, *rest):
    n_grp = len(B_PATTERNS)
    qkv = rest[:3 * n_grp]
    y_ref = rest[3 * n_grp]
    scratch = list(rest[3 * n_grp + 1:])
    qn_s, kn_s, kc, vc, o_s, lse_s = (scratch[i * n_grp:(i + 1) * n_grp] for i in range(6))
    s_s, p_s, vb_s = scratch[6 * n_grp:]
    n = pl.program_id(2)

    @pl.when(n == 0)
    def _():
        for g in range(n_grp):
            kc[g][...] = jnp.zeros_like(kc[g])
            vc[g][...] = jnp.zeros_like(vc[g])

    scale = HEAD_DIM ** -0.5 * math.log2(math.e)

    src = lax.broadcasted_iota(jnp.int32, (HEAD_DIM, HEAD_DIM), 0)
    dst = lax.broadcasted_iota(jnp.int32, (HEAD_DIM, HEAD_DIM), 1)
    rot = jnp.where(jnp.logical_and(dst < ROT_HALF, src == dst + ROT_HALF), -1.0,
                    jnp.where(jnp.logical_and(jnp.logical_and(dst >= ROT_HALF, dst < ROT_DIM),
                                              src == dst - ROT_HALF), 1.0, 0.0)).astype(BF16)

    def norm_rope(x, gain, cos, sin):
        ms = jnp.mean(x * x, axis=-1, keepdims=True)
        xn = x * lax.rsqrt(ms + EPS) * gain
        hi = xn.astype(BF16)
        lo = (xn - hi.astype(F32)).astype(BF16)
        return xn * cos + (_dot(hi, rot) + _dot(lo, rot)) * sin

    def prep(c, carry):
        rows = pl.ds(pl.multiple_of(c * B_PREP_ROWS, B_PREP_ROWS), B_PREP_ROWS)
        cos, sin = cos_ref[rows, :], sin_ref[rows, :]
        for g in range(n_grp):
            q_ref, k_ref = qkv[3 * g], qkv[3 * g + 1]
            qn_s[g][rows, :] = norm_rope(q_ref[rows, :], qg_ref[g:g + 1, :], cos, sin) * scale
            kn_s[g][rows, :] = norm_rope(k_ref[rows, :], kg_ref[g:g + 1, :], cos, sin)
        return carry

    lax.fori_loop(0, B_TILE // B_PREP_ROWS, prep, 0)

    qi = lax.broadcasted_iota(jnp.int32, (B_BLOCK, 2 * B_BLOCK), 0)
    ci = lax.broadcasted_iota(jnp.int32, (B_BLOCK, 2 * B_BLOCK), 1)
    cur_ok = jnp.logical_and(ci >= B_BLOCK, ci - B_BLOCK <= qi)
    prev_ok = jnp.logical_and(ci < B_BLOCK, ci >= qi)
    mask_inner = jnp.logical_or(cur_ok, prev_ok)
    mask_first = jnp.logical_or(cur_ok, jnp.logical_and(prev_ok, n > 0))
    ones = jnp.ones((2 * B_BLOCK, HEAD_DIM), BF16)

    for g, (win, dil) in enumerate(B_PATTERNS):
        v_ref = qkv[3 * g + 2]
        n_blk = B_TILE // (B_BLOCK * dil)
        blocks = [(r, j) for r in range(dil) for j in range(n_blk)]
        k_prev = v_prev = None
        for b0 in range(0, len(blocks), B_BATCH):
            batch = blocks[b0:b0 + B_BATCH]
            rows_of = []
            for i, (r, j) in enumerate(batch):
                start = j * B_BLOCK * dil + r
                rows = pl.ds(start, B_BLOCK) if dil == 1 else pl.ds(start, B_BLOCK, stride=dil)
                rows_of.append(rows)
                blk = slice(i * B_BLOCK, (i + 1) * B_BLOCK)
                if j == 0:
                    k_prev, v_prev = kc[g][r], vc[g][r]
                k_cur = kn_s[g][rows, :].astype(BF16)
                v_cur = v_ref[rows, :].astype(BF16)
                q = qn_s[g][rows, :].astype(BF16)
                s = _dot_nt(q, jnp.concatenate([k_prev, k_cur], axis=0))
                s_s[blk, :] = jnp.where(mask_first if j == 0 else mask_inner, s, NEG_BIG)
                vb_s[i, 0:B_BLOCK, :] = v_prev
                vb_s[i, B_BLOCK:2 * B_BLOCK, :] = v_cur
                if j == n_blk - 1:
                    kc[g][r] = k_cur
                    vc[g][r] = v_cur
                k_prev, v_prev = k_cur, v_cur
            live = slice(0, len(batch) * B_BLOCK)
            m = jnp.max(jnp.maximum(s_s[live, 0:B_BLOCK], s_s[live, B_BLOCK:2 * B_BLOCK]),
                        axis=-1, keepdims=True)
            p_s[live, :] = jnp.exp2(s_s[live, :] - m).astype(BF16)
            for i, rows in enumerate(rows_of):
                blk = slice(i * B_BLOCK, (i + 1) * B_BLOCK)
                pv = _dot(p_s[blk, :], jnp.concatenate([vb_s[i], ones], axis=1))
                den = pv[:, HEAD_DIM:]
                o_s[g][rows, :] = pv[:, :HEAD_DIM] / den
                lse_s[g][rows, :] = m[blk] + jnp.log2(den)

    def combine(c, carry):
        rows = pl.ds(pl.multiple_of(c * B_BLOCK, B_BLOCK), B_BLOCK)
        lse = [ref[rows, :] for ref in lse_s]
        m = functools.reduce(jnp.maximum, lse)
        w = [jnp.exp2(x - m) for x in lse]
        num = functools.reduce(jnp.add, [wg * ref[rows, :] for wg, ref in zip(w, o_s)])
        y_ref[rows, :] = (num / functools.reduce(jnp.add, w)).astype(y_ref.dtype)
        return carry

    lax.fori_loop(0, B_TILE // B_BLOCK, combine, 0)


def dilated_mixer(proj3, tables, q_gain, k_gain, col0):
    bsz, s, _ = proj3.shape
    n_grp = len(B_PATTERNS)
    assert s % B_TILE == 0

    def part(g, p):
        base = (col0 + (3 * g + p) * BRANCH_WIDTH) // HEAD_DIM
        return pl.BlockSpec((None, B_TILE, HEAD_DIM), lambda b, h, n: (b, n, base + h))

    tab = pl.BlockSpec((None, B_TILE, HEAD_DIM), lambda b, h, n: (b, n, 0))
    gain = pl.BlockSpec((n_grp, HEAD_DIM), lambda b, h, n: (0, 0))
    for win, dil in B_PATTERNS:
        assert win // dil == B_BLOCK and B_TILE % (B_BLOCK * dil) == 0
    token_f32 = [pltpu.VMEM((B_TILE, HEAD_DIM), F32)] * n_grp
    carry = [pltpu.VMEM((dil, B_BLOCK, HEAD_DIM), BF16) for _, dil in B_PATTERNS]
    scratch = (token_f32 + token_f32 + carry + carry + token_f32 + token_f32
               + [pltpu.VMEM((B_BATCH * B_BLOCK, 2 * B_BLOCK), F32),
                  pltpu.VMEM((B_BATCH * B_BLOCK, 2 * B_BLOCK), BF16),
                  pltpu.VMEM((B_BATCH, 2 * B_BLOCK, HEAD_DIM), BF16)])
    return pl.pallas_call(
        _dilated_kernel,
        grid=(bsz, N_HEADS, s // B_TILE),
        in_specs=[gain, gain, tab, tab]
                 + [part(g, p) for g in range(n_grp) for p in range(3)],
        out_specs=pl.BlockSpec((None, B_TILE, HEAD_DIM), lambda b, h, n: (b, n, h)),
        out_shape=jax.ShapeDtypeStruct((bsz, s, BRANCH_WIDTH), BF16),
        scratch_shapes=scratch,
        compiler_params=_params("parallel", "parallel", "arbitrary"),
        name="dilated_attn",
    )(q_gain, k_gain, *tables, *([proj3] * (3 * n_grp)))


def _spatial_kernel(u_ref, v_ref, lng_ref, lnb_ref, w_ref, b_ref, o_ref):
    inv_sqrt2 = 1.0 / math.sqrt(2.0)

    def gelu(x):
        return 0.5 * x * (1.0 + lax.erf(x * inv_sqrt2))

    v = gelu(v_ref[...])
    mu = jnp.mean(v, axis=-1, keepdims=True)
    vc = v - mu
    var = jnp.mean(vc * vc, axis=-1, keepdims=True)
    vn = vc * lax.rsqrt(var + EPS) * lng_ref[...] + lnb_ref[...]
    row = lax.broadcasted_iota(jnp.int32, (C_CHUNK, C_CHUNK), 0)
    col = lax.broadcasted_iota(jnp.int32, (C_CHUNK, C_CHUNK), 1)
    causal = row >= col
    bias = b_ref[...]
    for g in range(N_HEADS):
        gs = slice(g * HEAD_DIM, (g + 1) * HEAD_DIM)
        wm = jnp.where(causal, w_ref[g], 0.0).astype(BF16)
        mixed = _dot(wm, vn[:, gs].astype(BF16)) + bias[:, g:g + 1]
        o_ref[:, gs] = (gelu(u_ref[:, gs]) * mixed).astype(o_ref.dtype)


def spatial_gating(proj, ln_g, ln_b, w_s, b_s, col0):
    m, _ = proj.shape
    ublk = col0 // BRANCH_WIDTH
    return pl.pallas_call(
        _spatial_kernel,
        grid=(m // C_CHUNK,),
        in_specs=[pl.BlockSpec((C_CHUNK, BRANCH_WIDTH), lambda i: (i, ublk)),
                  pl.BlockSpec((C_CHUNK, BRANCH_WIDTH), lambda i: (i, ublk + 1)),
                  pl.BlockSpec((1, BRANCH_WIDTH), lambda i: (0, 0)),
                  pl.BlockSpec((1, BRANCH_WIDTH), lambda i: (0, 0)),
                  pl.BlockSpec((N_HEADS, C_CHUNK, C_CHUNK), lambda i: (0, 0, 0)),
                  pl.BlockSpec((C_CHUNK, N_HEADS), lambda i: (0, 0))],
        out_specs=pl.BlockSpec((C_CHUNK, BRANCH_WIDTH), lambda i: (i, 0)),
        out_shape=jax.ShapeDtypeStruct((m, BRANCH_WIDTH), BF16),
        compiler_params=_params("parallel"),
        name="spatial_gating",
    )(proj, proj, ln_g.reshape(1, -1), ln_b.reshape(1, -1), w_s, b_s.T)


def _merge_kernel(xn_ref, y_ref, wg_ref, wb_ref, o_ref, acc_ref, *, cast_sides):
    @pl.when(pl.program_id(2) == 0)
    def _():
        acc_ref[...] = jnp.zeros_like(acc_ref)

    cast_sides()
    gate = jax.nn.sigmoid(_dot(xn_ref[...], wg_ref[...]))
    total = acc_ref[...] + gate * _dot(y_ref[...], wb_ref[...])
    acc_ref[...] = total
    o_ref[...] = total.astype(o_ref.dtype)


def gated_merge(xn, ys, w_gate, w_branch, sides, *, tm, tn):
    m, d = xn.shape
    w = ys.shape[2]
    nblk = d // tn
    grid = (m // tm, nblk, N_BRANCHES)
    side = _side_casts(sides, grid)
    return pl.pallas_call(
        _with_side_casts(_merge_kernel, 4, 1, len(side)),
        grid=grid,
        in_specs=[pl.BlockSpec((tm, d), lambda a, b, i: (a, 0)),
                  pl.BlockSpec((None, tm, w), lambda a, b, i: (i, a, 0)),
                  pl.BlockSpec((d, tn), lambda a, b, i: (0, i * nblk + b)),
                  pl.BlockSpec((None, w, tn), lambda a, b, i: (i, 0, b))]
                 + [sc.in_spec for sc in side],
        out_specs=[pl.BlockSpec((tm, tn), lambda a, b, i: (a, b))] + [sc.out_spec for sc in side],
        out_shape=[jax.ShapeDtypeStruct((m, d), BF16)] + [sc.out_shape for sc in side],
        scratch_shapes=[pltpu.VMEM((tm, tn), F32)],
        compiler_params=_params("arbitrary", "arbitrary", "arbitrary"),
        name="gated_merge",
    )(xn, ys, w_gate, w_branch, *[sc.w for sc in side])


def _ffn_up_kernel(x_ref, wg_ref, wv_ref, cw_ref, cb_ref, o_ref, g_s, *, cast_sides, tiles_per_seq):
    i = pl.program_id(1)
    tm = x_ref.shape[0]
    halo = SUBLANES

    @pl.when(i % tiles_per_seq == 0)
    def _():
        g_s[0:halo, :] = jnp.zeros((halo, g_s.shape[1]), F32)

    @pl.when(i % tiles_per_seq != 0)
    def _():
        g_s[0:halo, :] = g_s[tm:tm + halo, :]

    cast_sides()
    x = x_ref[...]
    g_s[halo:tm + halo, :] = _dot(x, wg_ref[...])
    val = _dot(x, wv_ref[...])
    cw = cw_ref[...]
    conv = (cw[0:1] * g_s[halo - 2:tm + halo - 2, :] + cw[1:2] * g_s[halo - 1:tm + halo - 1, :]
            + cw[2:3] * g_s[halo:tm + halo, :] + cb_ref[...])
    o_ref[...] = (conv * jax.nn.sigmoid(conv) * val).astype(o_ref.dtype)


def ffn_up(xn, w_up, conv_w, conv_b, seq_len, sides, *, tm, tn):
    m, d = xn.shape
    dff = conv_w.shape[1]
    nblk = dff // tn
    grid = (nblk, m // tm)
    side = _side_casts(sides, grid)
    body = functools.partial(_ffn_up_kernel, tiles_per_seq=seq_len // tm)
    return pl.pallas_call(
        _with_side_casts(body, 5, 1, len(side)),
        grid=grid,
        in_specs=[pl.BlockSpec((tm, d), lambda j, i: (i, 0)),
                  pl.BlockSpec((d, tn), lambda j, i: (0, j)),
                  pl.BlockSpec((d, tn), lambda j, i: (0, nblk + j)),
                  pl.BlockSpec((CONV_W, tn), lambda j, i: (0, j)),
                  pl.BlockSpec((1, tn), lambda j, i: (0, j))]
                 + [sc.in_spec for sc in side],
        out_specs=[pl.BlockSpec((tm, tn), lambda j, i: (i, j))] + [sc.out_spec for sc in side],
        out_shape=[jax.ShapeDtypeStruct((m, dff), BF16)] + [sc.out_shape for sc in side],
        scratch_shapes=[pltpu.VMEM((tm + SUBLANES, tn), F32)],
        compiler_params=_params("arbitrary", "arbitrary"),
        name="ffn_up_conv",
    )(xn, w_up, w_up, conv_w, conv_b.reshape(1, dff), *[sc.w for sc in side])


def kernel(x, positions, norm_mix, w_in, hgrn_lower_bounds, hgrn_out_norm, q_norm, k_norm,
           sg_ln_g, sg_ln_b, sg_w, sg_b, w_gate, w_branch, w_out, norm_ffn, w_up,
           ffn_conv_w, ffn_conv_b, w_down):
    bsz, s, d = x.shape
    m = bsz * s
    depth = w_in.shape[0]
    in_cols = w_in.shape[2]
    a_cols = 4 * BRANCH_WIDTH
    b_cols = len(B_PATTERNS) * 3 * BRANCH_WIDTH
    tables = _rope_tables(positions)
    w_branch2d = w_branch.reshape(depth, N_BRANCHES * BRANCH_WIDTH, d)
    w_in_l = w_in[0].astype(BF16)
    w_branch_l = w_branch2d[0].astype(BF16)
    xf = x.reshape(m, d)
    for l in range(depth):
        xn = rmsnorm(xf, norm_mix[l])
        proj, w_gate_l = matmul(xn, w_in_l, None, [(w_gate, l, (512, 1024))], tm=1024, tn=1024)
        proj3 = proj.reshape(bsz, s, in_cols)
        ya = hgrn2_mixer(proj3, hgrn_lower_bounds, hgrn_out_norm[l], l).reshape(m, BRANCH_WIDTH)
        yb = dilated_mixer(proj3, tables, q_norm[l], k_norm[l], a_cols).reshape(m, BRANCH_WIDTH)
        yc = spatial_gating(proj, sg_ln_g[l], sg_ln_b[l], sg_w[l], sg_b[l], a_cols + b_cols)
        ys = jnp.stack([ya, yb, yc], axis=0)
        merged, w_up_l, w_out_l = gated_merge(
            xn, ys, w_gate_l, w_branch_l.reshape(N_BRANCHES, BRANCH_WIDTH, d),
            [(w_up, l, (512, 1024)), (w_out, l, (512, 512))], tm=1024, tn=512)
        xf = matmul_residual(merged, w_out_l, None, xf, tm=1024, tn=1024, tk=d)
        xn = rmsnorm(xf, norm_ffn[l])
        sides = [(w_down, l, (512, 512))]
        if l + 1 < depth:
            sides += [(w_in, l + 1, (512, 1024)), (w_branch2d, l + 1, (512, 512))]
        h, w_down_l, *next_layer = ffn_up(xn, w_up_l, ffn_conv_w[l], ffn_conv_b[l], s, sides,
                                          tm=1024, tn=512)
        if next_layer:
            w_in_l, w_branch_l = next_layer
        xf = matmul_residual(h, w_down_l, None, xf, tm=1024, tn=1024, tk=d)
    return xf.reshape(bsz, s, d)
```

```python
import functools
import math

import jax
import jax.numpy as jnp
from jax import lax
from jax.experimental import pallas as pl
from jax.experimental.pallas import tpu as pltpu

F32 = jnp.float32
BF16 = jnp.bfloat16

EPS = 1e-6
NEG_BIG = -1e30

LANES = 128
SUBLANES = 8
VMEM_LIMIT = 56 * 1024 * 1024

HEAD_DIM = 128
N_HEADS = 8
BRANCH_WIDTH = N_HEADS * HEAD_DIM
N_BRANCHES = 3

A_CHUNK = 64
A_SUB = 16
A_STEP_ROWS = 512
A_STEP_HEADS = 8
A_UNROLL = 2

B_PATTERNS = ((128, 1), (512, 4), (2048, 16))
B_BLOCK = 128
B_TILE = 2048
B_PREP_ROWS = 256
B_BATCH = 8
ROPE_THETA = 500000.0
ROT_DIM = HEAD_DIM // 4
ROT_HALF = ROT_DIM // 2

C_CHUNK = 128
C_STEP_ROWS = 512
CONV_W = 3


def _params(*sem):
    return pltpu.CompilerParams(dimension_semantics=sem, vmem_limit_bytes=VMEM_LIMIT)


def _dot(a, b):
    return jnp.dot(a, b, preferred_element_type=F32)


def _dot_nt(a, b):
    return lax.dot_general(a, b, (((1,), (1,)), ((), ())), preferred_element_type=F32)


def _dot_tn(a, b):
    return lax.dot_general(a, b, (((0,), (0,)), ((), ())), preferred_element_type=F32)


def _rmsnorm_kernel(x_ref, g_ref, o_ref):
    x = x_ref[...]
    ms = jnp.mean(x * x, axis=-1, keepdims=True)
    o_ref[...] = (x * lax.rsqrt(ms + EPS) * g_ref[...]).astype(o_ref.dtype)


def rmsnorm(x, g, *, rows=256):
    m, d = x.shape
    return pl.pallas_call(
        _rmsnorm_kernel,
        grid=(m // rows,),
        in_specs=[pl.BlockSpec((rows, d), lambda i: (i, 0)),
                  pl.BlockSpec((1, d), lambda i: (0, 0))],
        out_specs=pl.BlockSpec((rows, d), lambda i: (i, 0)),
        out_shape=jax.ShapeDtypeStruct((m, d), BF16),
        compiler_params=_params("parallel"),
        name="rmsnorm",
    )(x, g.reshape(1, d))


def _weight_spec(w, layer, block, index):
    if layer is None:
        return pl.BlockSpec(block, index)
    return pl.BlockSpec((None,) + block, lambda *ids: (layer,) + index(*ids))


class SideCast:
    def __init__(self, w, layer, block, grid):
        _, r, c = w.shape
        br, bc = block
        n_col = c // bc
        n_blocks = (r // br) * n_col
        assert r % br == 0 and c % bc == 0 and n_blocks <= math.prod(grid)

        def block_index(*ids):
            t = ids[0]
            for extent, i in zip(grid[1:], ids[1:]):
                t = t * extent + i
            t = jnp.minimum(t, n_blocks - 1)
            return t // n_col, t % n_col

        self.w = w
        self.in_spec = pl.BlockSpec((None, br, bc), lambda *ids: (layer,) + block_index(*ids))
        self.out_spec = pl.BlockSpec((br, bc), block_index)
        self.out_shape = jax.ShapeDtypeStruct((r, c), BF16)


def _side_casts(sides, grid):
    return [SideCast(w, layer, block, grid) for w, layer, block in sides]


def _with_side_casts(body, n_in, n_out, n_side):
    def wrapped(*refs):
        ins, rest = refs[:n_in], refs[n_in:]
        side_in, rest = rest[:n_side], rest[n_side:]
        outs, rest = rest[:n_out], rest[n_out:]
        side_out, scratch = rest[:n_side], rest[n_side:]

        def cast_sides():
            for src, dst in zip(side_in, side_out):
                dst[...] = src[...].astype(dst.dtype)

        body(*ins, *outs, *scratch, cast_sides=cast_sides)

    return wrapped


def _matmul_kernel(x_ref, w_ref, o_ref, *, cast_sides):
    cast_sides()
    o_ref[...] = _dot(x_ref[...], w_ref[...]).astype(o_ref.dtype)


def matmul(x, w, layer, sides, *, tm, tn, out_dtype=F32):
    m, k = x.shape
    n = w.shape[-1]
    grid = (m // tm, n // tn)
    side = _side_casts(sides, grid)
    return pl.pallas_call(
        _with_side_casts(_matmul_kernel, 2, 1, len(side)),
        grid=grid,
        in_specs=[pl.BlockSpec((tm, k), lambda i, j: (i, 0)),
                  _weight_spec(w, layer, (k, tn), lambda i, j: (0, j))]
                 + [sc.in_spec for sc in side],
        out_specs=[pl.BlockSpec((tm, tn), lambda i, j: (i, j))] + [sc.out_spec for sc in side],
        out_shape=[jax.ShapeDtypeStruct((m, n), out_dtype)] + [sc.out_shape for sc in side],
        compiler_params=_params("arbitrary", "arbitrary"),
        name="proj_matmul",
    )(x, w, *[sc.w for sc in side])


def _matmul_res_kernel(x_ref, w_ref, r_ref, o_ref):
    k = pl.program_id(2)

    @pl.when(k == 0)
    def _():
        o_ref[...] = r_ref[...]

    o_ref[...] += _dot(x_ref[...], w_ref[...])


def matmul_residual(x, w, layer, res, *, tm, tn, tk):
    m, k = x.shape
    n = w.shape[-1]
    return pl.pallas_call(
        _matmul_res_kernel,
        grid=(m // tm, n // tn, k // tk),
        in_specs=[pl.BlockSpec((tm, tk), lambda i, j, kk: (i, kk)),
                  _weight_spec(w, layer, (tk, tn), lambda i, j, kk: (kk, j)),
                  pl.BlockSpec((tm, tn), lambda i, j, kk: (i, j))],
        out_specs=pl.BlockSpec((tm, tn), lambda i, j, kk: (i, j)),
        out_shape=jax.ShapeDtypeStruct((m, n), F32),
        compiler_params=_params("parallel", "parallel", "arbitrary"),
        name="matmul_residual",
    )(x, w, res)


def _hgrn_kernel(lbp_ref, gain_ref, q_ref, f_ref, i_ref, g_ref, o_ref, st_ref, ck_s, v_s,
                 *, layer, n_chunks, n_heads):
    n = pl.program_id(2)

    @pl.when(n == 0)
    def _():
        st_ref[...] = jnp.zeros_like(st_ref)

    for other in range(1, N_BRANCHES):
        o_ref[other] = jnp.zeros(o_ref.shape[1:], o_ref.dtype)

    lbp = lbp_ref[...]
    e = jnp.exp(lbp - jnp.max(lbp, axis=0, keepdims=True))
    sm = e / jnp.sum(e, axis=0, keepdims=True)
    cs = sm[0:1]
    for li in range(1, layer + 1):
        cs = cs + sm[li:li + 1]
    lb_all = cs - sm[0:1]

    gain = gain_ref[...]
    n_sub = A_CHUNK // A_SUB
    row = lax.broadcasted_iota(jnp.int32, (A_CHUNK, A_CHUNK), 0)
    col = lax.broadcasted_iota(jnp.int32, (A_CHUNK, A_CHUNK), 1)
    tri = (row >= col).astype(F32)
    off_mask = (row // A_SUB) > (col // A_SUB)
    rblk = lax.broadcasted_iota(jnp.int32, (A_CHUNK, HEAD_DIM), 0) // A_SUB
    half_row = lax.broadcasted_iota(jnp.int32, (SUBLANES, HEAD_DIM), 0)
    scale = HEAD_DIM ** -0.5

    def head_chunk(hh, r0):
        sl = pl.ds(r0, A_CHUNK)
        hs = slice(hh * HEAD_DIM, (hh + 1) * HEAD_DIM)
        lb = lb_all[:, hs]
        q = q_ref[sl, hs] * scale
        v = i_ref[sl, hs]
        f = lb + (1.0 - lb) * jax.nn.sigmoid(f_ref[sl, hs])
        kk = 1.0 - f
        cum = jnp.dot(tri, jnp.log2(f), precision=lax.Precision.HIGHEST,
                      preferred_element_type=F32)
        ends = [cum[(j + 1) * A_SUB - 1:(j + 1) * A_SUB, :] for j in range(n_sub)]
        last = ends[-1]
        eblk = jnp.concatenate([jnp.broadcast_to(ej, (A_SUB, HEAD_DIM)) for ej in ends], axis=0)
        khat = kk * jnp.exp2(eblk - cum)
        st = st_ref[hh]

        o = _dot_nt((q * jnp.exp2(cum)).astype(BF16), st.astype(BF16))

        s_off = jnp.zeros((A_CHUNK, A_CHUNK), F32)
        for j in range(n_sub - 1):
            qj = q * jnp.exp2(jnp.minimum(cum - ends[j], 0.0))
            kj = jnp.where(rblk == j, khat, 0.0)
            s_off = s_off + _dot_nt(qj.astype(BF16), kj.astype(BF16))
        s_off = jnp.where(off_mask, s_off, 0.0)
        o = o + _dot(s_off.astype(BF16), v.astype(BF16))

        ck_s[hh] = cum - jnp.log2(kk)
        v_s[hh] = v
        diag = []
        for b in range(n_sub):
            base = b * A_SUB
            q_lo, q_hi = q[base:base + SUBLANES], q[base + SUBLANES:base + A_SUB]
            c_lo, c_hi = cum[base:base + SUBLANES], cum[base + SUBLANES:base + A_SUB]
            o_lo = jnp.zeros((SUBLANES, HEAD_DIM), F32)
            o_hi = jnp.zeros((SUBLANES, HEAD_DIM), F32)
            for s in range(A_SUB):
                r = base + s
                c_row = ck_s[hh, r:r + 1, :]
                v_row = v_s[hh, r:r + 1, :]
                if s < SUBLANES:
                    dec = jnp.exp2(jnp.where(half_row >= s, c_lo - c_row, NEG_BIG))
                    o_lo = o_lo + jnp.sum(q_lo * dec, axis=-1, keepdims=True) * v_row
                    dec = jnp.exp2(c_hi - c_row)
                else:
                    dec = jnp.exp2(jnp.where(half_row >= s - SUBLANES, c_hi - c_row, NEG_BIG))
                o_hi = o_hi + jnp.sum(q_hi * dec, axis=-1, keepdims=True) * v_row
            diag += [o_lo, o_hi]
        o = o + jnp.concatenate(diag, axis=0)

        ktil = khat * jnp.exp2(last - eblk)
        st_ref[hh] = st * jnp.exp2(last) + _dot_tn(v.astype(BF16), ktil.astype(BF16))

        ms = jnp.mean(o * o, axis=-1, keepdims=True)
        on = o * lax.rsqrt(ms + EPS) * gain
        g = g_ref[sl, hs]
        o_ref[0, sl, hs] = (on * (g * jax.nn.sigmoid(g))).astype(o_ref.dtype)

    def chunk(c, carry):
        r0 = pl.multiple_of(c * A_CHUNK, A_CHUNK)
        for hh in range(n_heads):
            head_chunk(hh, r0)
        return carry

    lax.fori_loop(0, n_chunks, chunk, 0, unroll=A_UNROLL)


def hgrn2_mixer(proj3, lower_bounds, out_gain, layer):
    bsz, s, _ = proj3.shape
    depth = lower_bounds.shape[0]
    ts = min(A_STEP_ROWS, s)
    hb = A_STEP_HEADS
    n_hg = N_HEADS // hb
    wide = hb * HEAD_DIM
    blk = lambda part: pl.BlockSpec((None, ts, wide), lambda b, h, n: (b, n, part * n_hg + h))
    return pl.pallas_call(
        functools.partial(_hgrn_kernel, layer=layer, n_chunks=ts // A_CHUNK, n_heads=hb),
        grid=(bsz, n_hg, s // ts),
        in_specs=[pl.BlockSpec((depth, wide), lambda b, h, n: (0, h)),
                  pl.BlockSpec((1, HEAD_DIM), lambda b, h, n: (0, 0)),
                  blk(0), blk(1), blk(2), blk(3)],
        out_specs=pl.BlockSpec((N_BRANCHES, None, ts, wide), lambda b, h, n: (0, b, n, h)),
        out_shape=jax.ShapeDtypeStruct((N_BRANCHES, bsz, s, BRANCH_WIDTH), BF16),
        scratch_shapes=[pltpu.VMEM((hb, HEAD_DIM, HEAD_DIM), F32)]
                       + [pltpu.VMEM((hb, A_CHUNK, HEAD_DIM), F32)] * 2,
        compiler_params=_params("parallel", "parallel", "arbitrary"),
        name="hgrn2",
    )(lower_bounds, out_gain.reshape(1, HEAD_DIM), proj3, proj3, proj3, proj3)


def _rope_tables(positions):
    inv = jnp.power(jnp.float32(ROPE_THETA), -jnp.arange(0, ROT_DIM, 2, dtype=F32) / ROT_DIM)
    ang = positions.astype(F32)[..., None] * inv
    cos = jnp.cos(ang)
    sin = jnp.sin(ang)
    pad = jnp.zeros(ang.shape[:-1] + (HEAD_DIM - ROT_DIM,), F32)
    cos_f = jnp.concatenate([cos, cos, pad + 1.0], axis=-1)
    sin_f = jnp.concatenate([sin, sin, pad], axis=-1)
    return cos_f, sin_f


def _dilated_kernel(qg_ref, kg_ref, cos_ref, sin_ref, *rest):
    n_grp = len(B_PATTERNS)
    qkv = rest[:3 * n_grp]
    y_ref = rest[3 * n_grp + 1]
    scratch = list(rest[3 * n_grp + 2:])
    qn_s, kn_s, kc, vc, num_s, den_s, m_s = (scratch[i * n_grp:(i + 1) * n_grp] for i in range(7))
    s_s, p_s, vb_s = scratch[7 * n_grp:]
    n = pl.program_id(2)

    @pl.when(n == 0)
    def _():
        for g in range(n_grp):
            kc[g][...] = jnp.zeros_like(kc[g])
            vc[g][...] = jnp.zeros_like(vc[g])

    q_gain = qg_ref[...] * (HEAD_DIM ** -0.5 * math.log2(math.e))
    k_gain = kg_ref[...]

    src = lax.broadcasted_iota(jnp.int32, (HEAD_DIM, HEAD_DIM), 0)
    dst = lax.broadcasted_iota(jnp.int32, (HEAD_DIM, HEAD_DIM), 1)
    rot = jnp.where(jnp.logical_and(dst < ROT_HALF, src == dst + ROT_HALF), -1.0,
                    jnp.where(jnp.logical_and(jnp.logical_and(dst >= ROT_HALF, dst < ROT_DIM),
                                              src == dst - ROT_HALF), 1.0, 0.0)).astype(BF16)
    rot2 = jnp.concatenate([rot, rot], axis=0)

    def norm_rope(x, gain, cos, sin):
        ms = jnp.mean(x * x, axis=-1, keepdims=True)
        xn = x * lax.rsqrt(ms + EPS) * gain
        hi = xn.astype(BF16)
        lo = (xn - hi.astype(F32)).astype(BF16)
        return xn * cos + _dot(jnp.concatenate([hi, lo], axis=1), rot2) * sin

    def prep(c, carry):
        rows = pl.ds(pl.multiple_of(c * B_PREP_ROWS, B_PREP_ROWS), B_PREP_ROWS)
        cos, sin = cos_ref[rows, :], sin_ref[rows, :]
        for g in range(n_grp):
            q_ref, k_ref = qkv[3 * g], qkv[3 * g + 1]
            qn_s[g][rows, :] = norm_rope(q_ref[rows, :], q_gain[g:g + 1, :], cos, sin)
            kn_s[g][rows, :] = norm_rope(k_ref[rows, :], k_gain[g:g + 1, :], cos, sin)
        return carry

    lax.fori_loop(0, B_TILE // B_PREP_ROWS, prep, 0)

    qi = lax.broadcasted_iota(jnp.int32, (B_BLOCK, 2 * B_BLOCK), 0)
    ci = lax.broadcasted_iota(jnp.int32, (B_BLOCK, 2 * B_BLOCK), 1)
    cur_ok = jnp.logical_and(ci >= B_BLOCK, ci - B_BLOCK <= qi)
    prev_ok = jnp.logical_and(ci < B_BLOCK, ci >= qi)
    mask_inner = jnp.logical_or(cur_ok, prev_ok)
    mask_first = jnp.logical_or(cur_ok, jnp.logical_and(prev_ok, n > 0))
    ones = jnp.ones((2 * B_BLOCK, HEAD_DIM), BF16)

    for g, (win, dil) in enumerate(B_PATTERNS):
        v_ref = qkv[3 * g + 2]
        n_blk = B_TILE // (B_BLOCK * dil)
        blocks = [(r, j) for r in range(dil) for j in range(n_blk)]
        k_prev = v_prev = None
        for b0 in range(0, len(blocks), B_BATCH):
            batch = blocks[b0:b0 + B_BATCH]
            rows_of = []
            for i, (r, j) in enumerate(batch):
                start = j * B_BLOCK * dil + r
                rows = pl.ds(start, B_BLOCK) if dil == 1 else pl.ds(start, B_BLOCK, stride=dil)
                rows_of.append(rows)
                blk = slice(i * B_BLOCK, (i + 1) * B_BLOCK)
                if j == 0:
                    k_prev, v_prev = kc[g][r], vc[g][r]
                k_cur = kn_s[g][rows, :].astype(BF16)
                v_cur = v_ref[rows, :].astype(BF16)
                q = qn_s[g][rows, :].astype(BF16)
                s = _dot_nt(q, jnp.concatenate([k_prev, k_cur], axis=0))
                s_s[blk, :] = jnp.where(mask_first if j == 0 else mask_inner, s, NEG_BIG)
                vb_s[i, 0:B_BLOCK, :] = v_prev
                vb_s[i, B_BLOCK:2 * B_BLOCK, :] = v_cur
                if j == n_blk - 1:
                    kc[g][r] = k_cur
                    vc[g][r] = v_cur
                k_prev, v_prev = k_cur, v_cur
            live = slice(0, len(batch) * B_BLOCK)
            m = jnp.max(jnp.maximum(s_s[live, 0:B_BLOCK], s_s[live, B_BLOCK:2 * B_BLOCK]),
                        axis=-1, keepdims=True)
            p_s[live, :] = jnp.exp2(s_s[live, :] - m).astype(BF16)
            for i, rows in enumerate(rows_of):
                blk = slice(i * B_BLOCK, (i + 1) * B_BLOCK)
                pv = _dot(p_s[blk, :], jnp.concatenate([vb_s[i], ones], axis=1))
                num_s[g][rows, :] = pv[:, :HEAD_DIM]
                den_s[g][rows, :] = pv[:, HEAD_DIM:]
                m_s[g][rows, :] = jnp.broadcast_to(m[blk], (B_BLOCK, HEAD_DIM))

    def combine(c, carry):
        rows = pl.ds(pl.multiple_of(c * B_BLOCK, B_BLOCK), B_BLOCK)
        ms = [ref[rows, :] for ref in m_s]
        top = functools.reduce(jnp.maximum, ms)
        w = [jnp.exp2(x - top) for x in ms]
        num = functools.reduce(jnp.add, [wg * ref[rows, :] for wg, ref in zip(w, num_s)])
        den = functools.reduce(jnp.add, [wg * ref[rows, :] for wg, ref in zip(w, den_s)])
        y_ref[rows, :] = (num / den).astype(y_ref.dtype)
        return carry

    lax.fori_loop(0, B_TILE // B_BLOCK, combine, 0)


def dilated_mixer(proj3, tables, q_gain, k_gain, col0, ys, branch):
    bsz, s, _ = proj3.shape
    n_grp = len(B_PATTERNS)
    assert s % B_TILE == 0

    def part(g, p):
        base = (col0 + (3 * g + p) * BRANCH_WIDTH) // HEAD_DIM
        return pl.BlockSpec((None, B_TILE, HEAD_DIM), lambda b, h, n: (b, n, base + h))

    tab = pl.BlockSpec((None, B_TILE, HEAD_DIM), lambda b, h, n: (b, n, 0))
    gain = pl.BlockSpec((n_grp, HEAD_DIM), lambda b, h, n: (0, 0))
    for win, dil in B_PATTERNS:
        assert win // dil == B_BLOCK and B_TILE % (B_BLOCK * dil) == 0
    token_f32 = [pltpu.VMEM((B_TILE, HEAD_DIM), F32)] * n_grp
    carry = [pltpu.VMEM((dil, B_BLOCK, HEAD_DIM), BF16) for _, dil in B_PATTERNS]
    scratch = (token_f32 + token_f32 + carry + carry + token_f32 + token_f32 + token_f32
               + [pltpu.VMEM((B_BATCH * B_BLOCK, 2 * B_BLOCK), F32),
                  pltpu.VMEM((B_BATCH * B_BLOCK, 2 * B_BLOCK), BF16),
                  pltpu.VMEM((B_BATCH, 2 * B_BLOCK, HEAD_DIM), BF16)])
    return pl.pallas_call(
        _dilated_kernel,
        grid=(bsz, N_HEADS, s // B_TILE),
        in_specs=[gain, gain, tab, tab]
                 + [part(g, p) for g in range(n_grp) for p in range(3)]
                 + [pl.BlockSpec(memory_space=pl.ANY)],
        out_specs=pl.BlockSpec((None, None, B_TILE, HEAD_DIM), lambda b, h, n: (branch, b, n, h)),
        out_shape=jax.ShapeDtypeStruct(ys.shape, ys.dtype),
        input_output_aliases={4 + 3 * n_grp: 0},
        scratch_shapes=scratch,
        compiler_params=_params("parallel", "parallel", "arbitrary"),
        name="dilated_attn",
    )(q_gain, k_gain, *tables, *([proj3] * (3 * n_grp)), ys)


def _spatial_kernel(u_ref, v_ref, lng_ref, lnb_ref, w_ref, b_ref, ys_ref, o_ref):
    inv_sqrt2 = 1.0 / math.sqrt(2.0)

    def gelu(x):
        return 0.5 * x * (1.0 + lax.erf(x * inv_sqrt2))

    row = lax.broadcasted_iota(jnp.int32, (C_CHUNK, C_CHUNK), 0)
    col = lax.broadcasted_iota(jnp.int32, (C_CHUNK, C_CHUNK), 1)
    causal = row >= col
    bias = b_ref[...]
    w_causal = [jnp.where(causal, w_ref[g], 0.0).astype(BF16) for g in range(N_HEADS)]
    for c in range(u_ref.shape[0] // C_CHUNK):
        rows = slice(c * C_CHUNK, (c + 1) * C_CHUNK)
        v = gelu(v_ref[rows, :])
        mu = jnp.mean(v, axis=-1, keepdims=True)
        vc = v - mu
        var = jnp.mean(vc * vc, axis=-1, keepdims=True)
        vn = vc * lax.rsqrt(var + EPS) * lng_ref[...] + lnb_ref[...]
        for g in range(N_HEADS):
            gs = slice(g * HEAD_DIM, (g + 1) * HEAD_DIM)
            mixed = _dot(w_causal[g], vn[:, gs].astype(BF16)) + bias[:, g:g + 1]
            o_ref[rows, gs] = (gelu(u_ref[rows, gs]) * mixed).astype(o_ref.dtype)


def spatial_gating(proj, ln_g, ln_b, w_s, b_s, col0, ys, branch):
    m, _ = proj.shape
    ublk = col0 // BRANCH_WIDTH
    rows = C_STEP_ROWS
    return pl.pallas_call(
        _spatial_kernel,
        grid=(m // rows,),
        in_specs=[pl.BlockSpec((rows, BRANCH_WIDTH), lambda i: (i, ublk)),
                  pl.BlockSpec((rows, BRANCH_WIDTH), lambda i: (i, ublk + 1)),
                  pl.BlockSpec((1, BRANCH_WIDTH), lambda i: (0, 0)),
                  pl.BlockSpec((1, BRANCH_WIDTH), lambda i: (0, 0)),
                  pl.BlockSpec((N_HEADS, C_CHUNK, C_CHUNK), lambda i: (0, 0, 0)),
                  pl.BlockSpec((C_CHUNK, N_HEADS), lambda i: (0, 0)),
                  pl.BlockSpec(memory_space=pl.ANY)],
        out_specs=pl.BlockSpec((None, rows, BRANCH_WIDTH), lambda i: (branch, i, 0)),
        out_shape=jax.ShapeDtypeStruct(ys.shape, ys.dtype),
        input_output_aliases={6: 0},
        compiler_params=_params("parallel"),
        name="spatial_gating",
    )(proj, proj, ln_g.reshape(1, -1), ln_b.reshape(1, -1), w_s, b_s.T, ys)


def _merge_kernel(xn_ref, y_ref, wg_ref, wb_ref, o_ref, acc_ref, *, cast_sides):
    @pl.when(pl.program_id(2) == 0)
    def _():
        acc_ref[...] = jnp.zeros_like(acc_ref)

    cast_sides()
    gate = jax.nn.sigmoid(_dot(xn_ref[...], wg_ref[...]))
    total = acc_ref[...] + gate * _dot(y_ref[...], wb_ref[...])
    acc_ref[...] = total
    o_ref[...] = total.astype(o_ref.dtype)


def gated_merge(xn, ys, w_gate, w_branch, sides, *, tm, tn):
    m, d = xn.shape
    w = ys.shape[2]
    nblk = d // tn
    grid = (m // tm, nblk, N_BRANCHES)
    side = _side_casts(sides, grid)
    return pl.pallas_call(
        _with_side_casts(_merge_kernel, 4, 1, len(side)),
        grid=grid,
        in_specs=[pl.BlockSpec((tm, d), lambda a, b, i: (a, 0)),
                  pl.BlockSpec((None, tm, w), lambda a, b, i: (i, a, 0)),
                  pl.BlockSpec((d, tn), lambda a, b, i: (0, i * nblk + b)),
                  pl.BlockSpec((None, w, tn), lambda a, b, i: (i, 0, b))]
                 + [sc.in_spec for sc in side],
        out_specs=[pl.BlockSpec((tm, tn), lambda a, b, i: (a, b))] + [sc.out_spec for sc in side],
        out_shape=[jax.ShapeDtypeStruct((m, d), BF16)] + [sc.out_shape for sc in side],
        scratch_shapes=[pltpu.VMEM((tm, tn), F32)],
        compiler_params=_params("arbitrary", "arbitrary", "arbitrary"),
        name="gated_merge",
    )(xn, ys, w_gate, w_branch, *[sc.w for sc in side])


def _ffn_up_kernel(x_ref, wg_ref, wv_ref, cw_ref, cb_ref, o_ref, g_s, *, cast_sides, tiles_per_seq):
    i = pl.program_id(1)
    tm = x_ref.shape[0]
    halo = SUBLANES

    @pl.when(i % tiles_per_seq == 0)
    def _():
        g_s[0:halo, :] = jnp.zeros((halo, g_s.shape[1]), F32)

    @pl.when(i % tiles_per_seq != 0)
    def _():
        g_s[0:halo, :] = g_s[tm:tm + halo, :]

    cast_sides()
    x = x_ref[...]
    g_s[halo:tm + halo, :] = _dot(x, wg_ref[...])
    val = _dot(x, wv_ref[...])
    cw = cw_ref[...]
    conv = (cw[0:1] * g_s[halo - 2:tm + halo - 2, :] + cw[1:2] * g_s[halo - 1:tm + halo - 1, :]
            + cw[2:3] * g_s[halo:tm + halo, :] + cb_ref[...])
    o_ref[...] = (conv * jax.nn.sigmoid(conv) * val).astype(o_ref.dtype)


def ffn_up(xn, w_up, conv_w, conv_b, seq_len, sides, *, tm, tn):
    m, d = xn.shape
    dff = conv_w.shape[1]
    nblk = dff // tn
    grid = (nblk, m // tm)
    side = _side_casts(sides, grid)
    body = functools.partial(_ffn_up_kernel, tiles_per_seq=seq_len // tm)
    return pl.pallas_call(
        _with_side_casts(body, 5, 1, len(side)),
        grid=grid,
        in_specs=[pl.BlockSpec((tm, d), lambda j, i: (i, 0)),
                  pl.BlockSpec((d, tn), lambda j, i: (0, j)),
                  pl.BlockSpec((d, tn), lambda j, i: (0, nblk + j)),
                  pl.BlockSpec((CONV_W, tn), lambda j, i: (0, j)),
                  pl.BlockSpec((1, tn), lambda j, i: (0, j))]
                 + [sc.in_spec for sc in side],
        out_specs=[pl.BlockSpec((tm, tn), lambda j, i: (i, j))] + [sc.out_spec for sc in side],
        out_shape=[jax.ShapeDtypeStruct((m, dff), BF16)] + [sc.out_shape for sc in side],
        scratch_shapes=[pltpu.VMEM((tm + SUBLANES, tn), F32)],
        compiler_params=_params("arbitrary", "arbitrary"),
        name="ffn_up_conv",
    )(xn, w_up, w_up, conv_w, conv_b.reshape(1, dff), *[sc.w for sc in side])


def kernel(x, positions, norm_mix, w_in, hgrn_lower_bounds, hgrn_out_norm, q_norm, k_norm,
           sg_ln_g, sg_ln_b, sg_w, sg_b, w_gate, w_branch, w_out, norm_ffn, w_up,
           ffn_conv_w, ffn_conv_b, w_down):
    bsz, s, d = x.shape
    m = bsz * s
    depth = w_in.shape[0]
    in_cols = w_in.shape[2]
    a_cols = 4 * BRANCH_WIDTH
    b_cols = len(B_PATTERNS) * 3 * BRANCH_WIDTH
    tables = _rope_tables(positions)
    w_branch2d = w_branch.reshape(depth, N_BRANCHES * BRANCH_WIDTH, d)
    w_in_l = w_in[0].astype(BF16)
    w_branch_l = w_branch2d[0].astype(BF16)
    xf = x.reshape(m, d)
    for l in range(depth):
        xn = rmsnorm(xf, norm_mix[l])
        proj, w_gate_l = matmul(xn, w_in_l, None, [(w_gate, l, (512, 1024))], tm=1024, tn=1024)
        proj3 = proj.reshape(bsz, s, in_cols)
        ys = hgrn2_mixer(proj3, hgrn_lower_bounds, hgrn_out_norm[l], l)
        ys = dilated_mixer(proj3, tables, q_norm[l], k_norm[l], a_cols, ys, 1)
        ys = spatial_gating(proj, sg_ln_g[l], sg_ln_b[l], sg_w[l], sg_b[l], a_cols + b_cols,
                            ys.reshape(N_BRANCHES, m, BRANCH_WIDTH), 2)
        merged, w_up_l, w_out_l = gated_merge(
            xn, ys, w_gate_l, w_branch_l.reshape(N_BRANCHES, BRANCH_WIDTH, d),
            [(w_up, l, (512, 1024)), (w_out, l, (512, 512))], tm=1024, tn=512)
        xf = matmul_residual(merged, w_out_l, None, xf, tm=1024, tn=1024, tk=d)
        xn = rmsnorm(xf, norm_ffn[l])
        sides = [(w_down, l, (512, 512))]
        if l + 1 < depth:
            sides += [(w_in, l + 1, (512, 1024)), (w_branch2d, l + 1, (512, 512))]
        h, w_down_l, *next_layer = ffn_up(xn, w_up_l, ffn_conv_w[l], ffn_conv_b[l], s, sides,
                                          tm=1024, tn=512)
        if next_layer:
            w_in_l, w_branch_l = next_layer
        xf = matmul_residual(h, w_down_l, None, xf, tm=1024, tn=1024, tk=d)
    return xf.reshape(bsz, s, d)
```

```python
import functools
import math

import jax
import jax.numpy as jnp
from jax import lax
from jax.experimental import pallas as pl
from jax.experimental.pallas import tpu as pltpu

F32 = jnp.float32
BF16 = jnp.bfloat16

EPS = 1e-6
NEG_BIG = -1e30

LANES = 128
SUBLANES = 8
VMEM_LIMIT = 56 * 1024 * 1024

HEAD_DIM = 128
N_HEADS = 8
BRANCH_WIDTH = N_HEADS * HEAD_DIM
N_BRANCHES = 3

A_CHUNK = 64
A_SUB = 16
A_STEP_ROWS = 512
A_STEP_HEADS = 8
A_UNROLL = 2

B_PATTERNS = ((128, 1), (512, 4), (2048, 16))
B_BLOCK = 128
B_TILE = 2048
B_PREP_ROWS = 256
B_BATCH = 8
ROPE_THETA = 500000.0
ROT_DIM = HEAD_DIM // 4
ROT_HALF = ROT_DIM // 2

C_CHUNK = 128
C_STEP_ROWS = 512
CONV_W = 3


def _params(*sem):
    return pltpu.CompilerParams(dimension_semantics=sem, vmem_limit_bytes=VMEM_LIMIT)


def _dot(a, b):
    return jnp.dot(a, b, preferred_element_type=F32)


def _dot_nt(a, b):
    return lax.dot_general(a, b, (((1,), (1,)), ((), ())), preferred_element_type=F32)


def _dot_tn(a, b):
    return lax.dot_general(a, b, (((0,), (0,)), ((), ())), preferred_element_type=F32)


def _rmsnorm_kernel(x_ref, g_ref, o_ref):
    x = x_ref[...]
    ms = jnp.mean(x * x, axis=-1, keepdims=True)
    o_ref[...] = (x * lax.rsqrt(ms + EPS) * g_ref[...]).astype(o_ref.dtype)


def rmsnorm(x, g, *, rows=256):
    m, d = x.shape
    return pl.pallas_call(
        _rmsnorm_kernel,
        grid=(m // rows,),
        in_specs=[pl.BlockSpec((rows, d), lambda i: (i, 0)),
                  pl.BlockSpec((1, d), lambda i: (0, 0))],
        out_specs=pl.BlockSpec((rows, d), lambda i: (i, 0)),
        out_shape=jax.ShapeDtypeStruct((m, d), BF16),
        compiler_params=_params("parallel"),
        name="rmsnorm",
    )(x, g.reshape(1, d))


def _weight_spec(w, layer, block, index):
    if layer is None:
        return pl.BlockSpec(block, index)
    return pl.BlockSpec((None,) + block, lambda *ids: (layer,) + index(*ids))


class SideCast:
    def __init__(self, w, layer, block, grid):
        _, r, c = w.shape
        br, bc = block
        n_col = c // bc
        n_blocks = (r // br) * n_col
        assert r % br == 0 and c % bc == 0 and n_blocks <= math.prod(grid)

        def block_index(*ids):
            t = ids[0]
            for extent, i in zip(grid[1:], ids[1:]):
                t = t * extent + i
            t = jnp.minimum(t, n_blocks - 1)
            return t // n_col, t % n_col

        self.w = w
        self.in_spec = pl.BlockSpec((None, br, bc), lambda *ids: (layer,) + block_index(*ids))
        self.out_spec = pl.BlockSpec((br, bc), block_index)
        self.out_shape = jax.ShapeDtypeStruct((r, c), BF16)


def _side_casts(sides, grid):
    return [SideCast(w, layer, block, grid) for w, layer, block in sides]


def _with_side_casts(body, n_in, n_out, n_side):
    def wrapped(*refs):
        ins, rest = refs[:n_in], refs[n_in:]
        side_in, rest = rest[:n_side], rest[n_side:]
        outs, rest = rest[:n_out], rest[n_out:]
        side_out, scratch = rest[:n_side], rest[n_side:]

        def cast_sides():
            for src, dst in zip(side_in, side_out):
                dst[...] = src[...].astype(dst.dtype)

        body(*ins, *outs, *scratch, cast_sides=cast_sides)

    return wrapped


def _matmul_planes_kernel(x_ref, w_ref, o_ref, *, cast_sides):
    cast_sides()
    acc = _dot(x_ref[...], w_ref[...])
    for p in range(o_ref.shape[0]):
        o_ref[p] = acc[:, p * LANES:(p + 1) * LANES]


def matmul_planes(x, w, layer, sides, *, tm, tn):
    m, k = x.shape
    n = w.shape[-1]
    grid = (m // tm, n // tn)
    side = _side_casts(sides, grid)
    return pl.pallas_call(
        _with_side_casts(_matmul_planes_kernel, 2, 1, len(side)),
        grid=grid,
        in_specs=[pl.BlockSpec((tm, k), lambda i, j: (i, 0)),
                  _weight_spec(w, layer, (k, tn), lambda i, j: (0, j))]
                 + [sc.in_spec for sc in side],
        out_specs=[pl.BlockSpec((tn // LANES, tm, LANES), lambda i, j: (j, i, 0))]
                  + [sc.out_spec for sc in side],
        out_shape=[jax.ShapeDtypeStruct((n // LANES, m, LANES), F32)] + [sc.out_shape for sc in side],
        compiler_params=_params("arbitrary", "arbitrary"),
        name="proj_matmul",
    )(x, w, *[sc.w for sc in side])


def _matmul_res_kernel(x_ref, w_ref, r_ref, o_ref):
    k = pl.program_id(2)

    @pl.when(k == 0)
    def _():
        o_ref[...] = r_ref[...]

    o_ref[...] += _dot(x_ref[...], w_ref[...])


def matmul_residual(x, w, layer, res, *, tm, tn, tk):
    m, k = x.shape
    n = w.shape[-1]
    return pl.pallas_call(
        _matmul_res_kernel,
        grid=(m // tm, n // tn, k // tk),
        in_specs=[pl.BlockSpec((tm, tk), lambda i, j, kk: (i, kk)),
                  _weight_spec(w, layer, (tk, tn), lambda i, j, kk: (kk, j)),
                  pl.BlockSpec((tm, tn), lambda i, j, kk: (i, j))],
        out_specs=pl.BlockSpec((tm, tn), lambda i, j, kk: (i, j)),
        out_shape=jax.ShapeDtypeStruct((m, n), F32),
        compiler_params=_params("parallel", "parallel", "arbitrary"),
        name="matmul_residual",
    )(x, w, res)


def _hgrn_kernel(lbp_ref, gain_ref, q_ref, f_ref, i_ref, g_ref, o_ref, st_ref, ck_s, v_s,
                 *, layer, n_chunks, n_heads):
    n = pl.program_id(2)

    @pl.when(n == 0)
    def _():
        st_ref[...] = jnp.zeros_like(st_ref)

    for other in range(1, N_BRANCHES):
        o_ref[other] = jnp.zeros(o_ref.shape[1:], o_ref.dtype)

    lbp = lbp_ref[...]
    e = jnp.exp(lbp - jnp.max(lbp, axis=0, keepdims=True))
    sm = e / jnp.sum(e, axis=0, keepdims=True)
    cs = sm[0:1]
    for li in range(1, layer + 1):
        cs = cs + sm[li:li + 1]
    lb_all = cs - sm[0:1]

    gain = gain_ref[...]
    n_sub = A_CHUNK // A_SUB
    row = lax.broadcasted_iota(jnp.int32, (A_CHUNK, A_CHUNK), 0)
    col = lax.broadcasted_iota(jnp.int32, (A_CHUNK, A_CHUNK), 1)
    tri = (row >= col).astype(F32)
    off_mask = (row // A_SUB) > (col // A_SUB)
    rblk = lax.broadcasted_iota(jnp.int32, (A_CHUNK, HEAD_DIM), 0) // A_SUB
    half_row = lax.broadcasted_iota(jnp.int32, (SUBLANES, HEAD_DIM), 0)
    scale = HEAD_DIM ** -0.5

    def head_chunk(hh, r0):
        sl = pl.ds(r0, A_CHUNK)
        hs = slice(hh * HEAD_DIM, (hh + 1) * HEAD_DIM)
        lb = lb_all[:, hs]
        q = q_ref[hh, sl, :] * scale
        v = i_ref[hh, sl, :]
        f = lb + (1.0 - lb) * jax.nn.sigmoid(f_ref[hh, sl, :])
        kk = 1.0 - f
        cum = jnp.dot(tri, jnp.log2(f), precision=lax.Precision.HIGHEST,
                      preferred_element_type=F32)
        ends = [cum[(j + 1) * A_SUB - 1:(j + 1) * A_SUB, :] for j in range(n_sub)]
        last = ends[-1]
        eblk = jnp.concatenate([jnp.broadcast_to(ej, (A_SUB, HEAD_DIM)) for ej in ends], axis=0)
        khat = kk * jnp.exp2(eblk - cum)
        st = st_ref[hh]

        o = _dot_nt((q * jnp.exp2(cum)).astype(BF16), st.astype(BF16))

        s_off = jnp.zeros((A_CHUNK, A_CHUNK), F32)
        for j in range(n_sub - 1):
            qj = q * jnp.exp2(jnp.minimum(cum - ends[j], 0.0))
            kj = jnp.where(rblk == j, khat, 0.0)
            s_off = s_off + _dot_nt(qj.astype(BF16), kj.astype(BF16))
        s_off = jnp.where(off_mask, s_off, 0.0)
        o = o + _dot(s_off.astype(BF16), v.astype(BF16))

        ck_s[hh] = cum - jnp.log2(kk)
        v_s[hh] = v
        diag = []
        for b in range(n_sub):
            base = b * A_SUB
            q_lo, q_hi = q[base:base + SUBLANES], q[base + SUBLANES:base + A_SUB]
            c_lo, c_hi = cum[base:base + SUBLANES], cum[base + SUBLANES:base + A_SUB]
            o_lo = jnp.zeros((SUBLANES, HEAD_DIM), F32)
            o_hi = jnp.zeros((SUBLANES, HEAD_DIM), F32)
            for s in range(A_SUB):
                r = base + s
                c_row = ck_s[hh, r:r + 1, :]
                v_row = v_s[hh, r:r + 1, :]
                if s < SUBLANES:
                    dec = jnp.exp2(jnp.where(half_row >= s, c_lo - c_row, NEG_BIG))
                    o_lo = o_lo + jnp.sum(q_lo * dec, axis=-1, keepdims=True) * v_row
                    dec = jnp.exp2(c_hi - c_row)
                else:
                    dec = jnp.exp2(jnp.where(half_row >= s - SUBLANES, c_hi - c_row, NEG_BIG))
                o_hi = o_hi + jnp.sum(q_hi * dec, axis=-1, keepdims=True) * v_row
            diag += [o_lo, o_hi]
        o = o + jnp.concatenate(diag, axis=0)

        ktil = khat * jnp.exp2(last - eblk)
        st_ref[hh] = st * jnp.exp2(last) + _dot_tn(v.astype(BF16), ktil.astype(BF16))

        ms = jnp.mean(o * o, axis=-1, keepdims=True)
        on = o * lax.rsqrt(ms + EPS) * gain
        g = g_ref[hh, sl, :]
        o_ref[0, sl, hs] = (on * (g * jax.nn.sigmoid(g))).astype(o_ref.dtype)

    def chunk(c, carry):
        r0 = pl.multiple_of(c * A_CHUNK, A_CHUNK)
        for hh in range(n_heads):
            head_chunk(hh, r0)
        return carry

    lax.fori_loop(0, n_chunks, chunk, 0, unroll=A_UNROLL)


def hgrn2_mixer(planes, lower_bounds, out_gain, layer):
    _, bsz, s, _ = planes.shape
    depth = lower_bounds.shape[0]
    ts = min(A_STEP_ROWS, s)
    hb = A_STEP_HEADS
    n_hg = N_HEADS // hb
    wide = hb * HEAD_DIM
    blk = lambda part: pl.BlockSpec((hb, None, ts, HEAD_DIM),
                                    lambda b, h, n: (part * n_hg + h, b, n, 0))
    return pl.pallas_call(
        functools.partial(_hgrn_kernel, layer=layer, n_chunks=ts // A_CHUNK, n_heads=hb),
        grid=(bsz, n_hg, s // ts),
        in_specs=[pl.BlockSpec((depth, wide), lambda b, h, n: (0, h)),
                  pl.BlockSpec((1, HEAD_DIM), lambda b, h, n: (0, 0)),
                  blk(0), blk(1), blk(2), blk(3)],
        out_specs=pl.BlockSpec((N_BRANCHES, None, ts, wide), lambda b, h, n: (0, b, n, h)),
        out_shape=jax.ShapeDtypeStruct((N_BRANCHES, bsz, s, BRANCH_WIDTH), BF16),
        scratch_shapes=[pltpu.VMEM((hb, HEAD_DIM, HEAD_DIM), F32)]
                       + [pltpu.VMEM((hb, A_CHUNK, HEAD_DIM), F32)] * 2,
        compiler_params=_params("parallel", "parallel", "arbitrary"),
        name="hgrn2",
    )(lower_bounds, out_gain.reshape(1, HEAD_DIM), planes, planes, planes, planes)


def _rope_tables(positions):
    inv = jnp.power(jnp.float32(ROPE_THETA), -jnp.arange(0, ROT_DIM, 2, dtype=F32) / ROT_DIM)
    ang = positions.astype(F32)[..., None] * inv
    cos = jnp.cos(ang)
    sin = jnp.sin(ang)
    pad = jnp.zeros(ang.shape[:-1] + (HEAD_DIM - ROT_DIM,), F32)
    cos_f = jnp.concatenate([cos, cos, pad + 1.0], axis=-1)
    sin_f = jnp.concatenate([sin, sin, pad], axis=-1)
    return cos_f, sin_f


def _dilated_kernel(qg_ref, kg_ref, cos_ref, sin_ref, *rest):
    n_grp = len(B_PATTERNS)
    qkv = rest[:3 * n_grp]
    y_ref = rest[3 * n_grp + 1]
    scratch = list(rest[3 * n_grp + 2:])
    qn_s, kn_s, kc, vc, num_s, den_s, m_s = (scratch[i * n_grp:(i + 1) * n_grp] for i in range(7))
    s_s, p_s, vb_s = scratch[7 * n_grp:]
    n = pl.program_id(2)

    @pl.when(n == 0)
    def _():
        for g in range(n_grp):
            kc[g][...] = jnp.zeros_like(kc[g])
            vc[g][...] = jnp.zeros_like(vc[g])

    q_gain = qg_ref[...] * (HEAD_DIM ** -0.5 * math.log2(math.e))
    k_gain = kg_ref[...]

    src = lax.broadcasted_iota(jnp.int32, (HEAD_DIM, HEAD_DIM), 0)
    dst = lax.broadcasted_iota(jnp.int32, (HEAD_DIM, HEAD_DIM), 1)
    rot = jnp.where(jnp.logical_and(dst < ROT_HALF, src == dst + ROT_HALF), -1.0,
                    jnp.where(jnp.logical_and(jnp.logical_and(dst >= ROT_HALF, dst < ROT_DIM),
                                              src == dst - ROT_HALF), 1.0, 0.0)).astype(BF16)
    rot2 = jnp.concatenate([rot, rot], axis=0)

    def norm_rope(x, gain, cos, sin):
        ms = jnp.mean(x * x, axis=-1, keepdims=True)
        xn = x * lax.rsqrt(ms + EPS) * gain
        hi = xn.astype(BF16)
        lo = (xn - hi.astype(F32)).astype(BF16)
        return xn * cos + _dot(jnp.concatenate([hi, lo], axis=1), rot2) * sin

    def prep(c, carry):
        rows = pl.ds(pl.multiple_of(c * B_PREP_ROWS, B_PREP_ROWS), B_PREP_ROWS)
        cos, sin = cos_ref[rows, :], sin_ref[rows, :]
        for g in range(n_grp):
            q_ref, k_ref = qkv[3 * g], qkv[3 * g + 1]
            qn_s[g][rows, :] = norm_rope(q_ref[rows, :], q_gain[g:g + 1, :], cos, sin)
            kn_s[g][rows, :] = norm_rope(k_ref[rows, :], k_gain[g:g + 1, :], cos, sin)
        return carry

    lax.fori_loop(0, B_TILE // B_PREP_ROWS, prep, 0)

    qi = lax.broadcasted_iota(jnp.int32, (B_BLOCK, 2 * B_BLOCK), 0)
    ci = lax.broadcasted_iota(jnp.int32, (B_BLOCK, 2 * B_BLOCK), 1)
    cur_ok = jnp.logical_and(ci >= B_BLOCK, ci - B_BLOCK <= qi)
    prev_ok = jnp.logical_and(ci < B_BLOCK, ci >= qi)
    mask_inner = jnp.logical_or(cur_ok, prev_ok)
    mask_first = jnp.logical_or(cur_ok, jnp.logical_and(prev_ok, n > 0))
    ones = jnp.ones((2 * B_BLOCK, HEAD_DIM), BF16)

    for g, (win, dil) in enumerate(B_PATTERNS):
        v_ref = qkv[3 * g + 2]
        n_blk = B_TILE // (B_BLOCK * dil)
        blocks = [(r, j) for r in range(dil) for j in range(n_blk)]
        k_prev = v_prev = None
        for b0 in range(0, len(blocks), B_BATCH):
            batch = blocks[b0:b0 + B_BATCH]
            rows_of = []
            for i, (r, j) in enumerate(batch):
                start = j * B_BLOCK * dil + r
                rows = pl.ds(start, B_BLOCK) if dil == 1 else pl.ds(start, B_BLOCK, stride=dil)
                rows_of.append(rows)
                blk = slice(i * B_BLOCK, (i + 1) * B_BLOCK)
                if j == 0:
                    k_prev, v_prev = kc[g][r], vc[g][r]
                k_cur = kn_s[g][rows, :].astype(BF16)
                v_cur = v_ref[rows, :].astype(BF16)
                q = qn_s[g][rows, :].astype(BF16)
                s = _dot_nt(q, jnp.concatenate([k_prev, k_cur], axis=0))
                s_s[blk, :] = jnp.where(mask_first if j == 0 else mask_inner, s, NEG_BIG)
                vb_s[i, 0:B_BLOCK, :] = v_prev
                vb_s[i, B_BLOCK:2 * B_BLOCK, :] = v_cur
                if j == n_blk - 1:
                    kc[g][r] = k_cur
                    vc[g][r] = v_cur
                k_prev, v_prev = k_cur, v_cur
            live = slice(0, len(batch) * B_BLOCK)
            m = jnp.max(jnp.maximum(s_s[live, 0:B_BLOCK], s_s[live, B_BLOCK:2 * B_BLOCK]),
                        axis=-1, keepdims=True)
            p_s[live, :] = jnp.exp2(s_s[live, :] - m).astype(BF16)
            for i, rows in enumerate(rows_of):
                blk = slice(i * B_BLOCK, (i + 1) * B_BLOCK)
                pv = _dot(p_s[blk, :], jnp.concatenate([vb_s[i], ones], axis=1))
                num_s[g][rows, :] = pv[:, :HEAD_DIM]
                den_s[g][rows, :] = pv[:, HEAD_DIM:]
                m_s[g][rows, :] = jnp.broadcast_to(m[blk], (B_BLOCK, HEAD_DIM))

    def combine(c, carry):
        rows = pl.ds(pl.multiple_of(c * B_BLOCK, B_BLOCK), B_BLOCK)
        ms = [ref[rows, :] for ref in m_s]
        top = functools.reduce(jnp.maximum, ms)
        w = [jnp.exp2(x - top) for x in ms]
        num = functools.reduce(jnp.add, [wg * ref[rows, :] for wg, ref in zip(w, num_s)])
        den = functools.reduce(jnp.add, [wg * ref[rows, :] for wg, ref in zip(w, den_s)])
        y_ref[rows, :] = (num / den).astype(y_ref.dtype)
        return carry

    lax.fori_loop(0, B_TILE // B_BLOCK, combine, 0)


def dilated_mixer(planes, tables, q_gain, k_gain, col0, ys, branch):
    _, bsz, s, _ = planes.shape
    n_grp = len(B_PATTERNS)
    assert s % B_TILE == 0

    def part(g, p):
        base = (col0 + (3 * g + p) * BRANCH_WIDTH) // HEAD_DIM
        return pl.BlockSpec((None, None, B_TILE, HEAD_DIM), lambda b, h, n: (base + h, b, n, 0))

    tab = pl.BlockSpec((None, B_TILE, HEAD_DIM), lambda b, h, n: (b, n, 0))
    gain = pl.BlockSpec((n_grp, HEAD_DIM), lambda b, h, n: (0, 0))
    for win, dil in B_PATTERNS:
        assert win // dil == B_BLOCK and B_TILE % (B_BLOCK * dil) == 0
    token_f32 = [pltpu.VMEM((B_TILE, HEAD_DIM), F32)] * n_grp
    carry = [pltpu.VMEM((dil, B_BLOCK, HEAD_DIM), BF16) for _, dil in B_PATTERNS]
    scratch = (token_f32 + token_f32 + carry + carry + token_f32 + token_f32 + token_f32
               + [pltpu.VMEM((B_BATCH * B_BLOCK, 2 * B_BLOCK), F32),
                  pltpu.VMEM((B_BATCH * B_BLOCK, 2 * B_BLOCK), BF16),
                  pltpu.VMEM((B_BATCH, 2 * B_BLOCK, HEAD_DIM), BF16)])
    return pl.pallas_call(
        _dilated_kernel,
        grid=(bsz, N_HEADS, s // B_TILE),
        in_specs=[gain, gain, tab, tab]
                 + [part(g, p) for g in range(n_grp) for p in range(3)]
                 + [pl.BlockSpec(memory_space=pl.ANY)],
        out_specs=pl.BlockSpec((None, None, B_TILE, HEAD_DIM), lambda b, h, n: (branch, b, n, h)),
        out_shape=jax.ShapeDtypeStruct(ys.shape, ys.dtype),
        input_output_aliases={4 + 3 * n_grp: 0},
        scratch_shapes=scratch,
        compiler_params=_params("parallel", "parallel", "arbitrary"),
        name="dilated_attn",
    )(q_gain, k_gain, *tables, *([planes] * (3 * n_grp)), ys)


def _spatial_kernel(u_ref, v_ref, lng_ref, lnb_ref, w_ref, b_ref, ys_ref, o_ref):
    inv_sqrt2 = 1.0 / math.sqrt(2.0)

    def gelu(x):
        return 0.5 * x * (1.0 + lax.erf(x * inv_sqrt2))

    row = lax.broadcasted_iota(jnp.int32, (C_CHUNK, C_CHUNK), 0)
    col = lax.broadcasted_iota(jnp.int32, (C_CHUNK, C_CHUNK), 1)
    causal = row >= col
    bias = b_ref[...]
    w_causal = [jnp.where(causal, w_ref[g], 0.0).astype(BF16) for g in range(N_HEADS)]
    for c in range(u_ref.shape[1] // C_CHUNK):
        rows = slice(c * C_CHUNK, (c + 1) * C_CHUNK)
        v = gelu(jnp.concatenate([v_ref[g, rows, :] for g in range(N_HEADS)], axis=1))
        mu = jnp.mean(v, axis=-1, keepdims=True)
        vc = v - mu
        var = jnp.mean(vc * vc, axis=-1, keepdims=True)
        vn = vc * lax.rsqrt(var + EPS) * lng_ref[...] + lnb_ref[...]
        for g in range(N_HEADS):
            gs = slice(g * HEAD_DIM, (g + 1) * HEAD_DIM)
            mixed = _dot(w_causal[g], vn[:, gs].astype(BF16)) + bias[:, g:g + 1]
            o_ref[rows, gs] = (gelu(u_ref[g, rows, :]) * mixed).astype(o_ref.dtype)


def spatial_gating(planes, ln_g, ln_b, w_s, b_s, col0, ys, branch):
    _, m, _ = planes.shape
    ublk = col0 // BRANCH_WIDTH
    rows = C_STEP_ROWS
    return pl.pallas_call(
        _spatial_kernel,
        grid=(m // rows,),
        in_specs=[pl.BlockSpec((N_HEADS, rows, HEAD_DIM), lambda i: (ublk, i, 0)),
                  pl.BlockSpec((N_HEADS, rows, HEAD_DIM), lambda i: (ublk + 1, i, 0)),
                  pl.BlockSpec((1, BRANCH_WIDTH), lambda i: (0, 0)),
                  pl.BlockSpec((1, BRANCH_WIDTH), lambda i: (0, 0)),
                  pl.BlockSpec((N_HEADS, C_CHUNK, C_CHUNK), lambda i: (0, 0, 0)),
                  pl.BlockSpec((C_CHUNK, N_HEADS), lambda i: (0, 0)),
                  pl.BlockSpec(memory_space=pl.ANY)],
        out_specs=pl.BlockSpec((None, rows, BRANCH_WIDTH), lambda i: (branch, i, 0)),
        out_shape=jax.ShapeDtypeStruct(ys.shape, ys.dtype),
        input_output_aliases={6: 0},
        compiler_params=_params("parallel"),
        name="spatial_gating",
    )(planes, planes, ln_g.reshape(1, -1), ln_b.reshape(1, -1), w_s, b_s.T, ys)


def _merge_kernel(xn_ref, y_ref, wg_ref, wb_ref, o_ref, acc_ref, *, cast_sides):
    @pl.when(pl.program_id(2) == 0)
    def _():
        acc_ref[...] = jnp.zeros_like(acc_ref)

    cast_sides()
    gate = jax.nn.sigmoid(_dot(xn_ref[...], wg_ref[...]))
    total = acc_ref[...] + gate * _dot(y_ref[...], wb_ref[...])
    acc_ref[...] = total
    o_ref[...] = total.astype(o_ref.dtype)


def gated_merge(xn, ys, w_gate, w_branch, sides, *, tm, tn):
    m, d = xn.shape
    w = ys.shape[2]
    nblk = d // tn
    grid = (m // tm, nblk, N_BRANCHES)
    side = _side_casts(sides, grid)
    return pl.pallas_call(
        _with_side_casts(_merge_kernel, 4, 1, len(side)),
        grid=grid,
        in_specs=[pl.BlockSpec((tm, d), lambda a, b, i: (a, 0)),
                  pl.BlockSpec((None, tm, w), lambda a, b, i: (i, a, 0)),
                  pl.BlockSpec((d, tn), lambda a, b, i: (0, i * nblk + b)),
                  pl.BlockSpec((None, w, tn), lambda a, b, i: (i, 0, b))]
                 + [sc.in_spec for sc in side],
        out_specs=[pl.BlockSpec((tm, tn), lambda a, b, i: (a, b))] + [sc.out_spec for sc in side],
        out_shape=[jax.ShapeDtypeStruct((m, d), BF16)] + [sc.out_shape for sc in side],
        scratch_shapes=[pltpu.VMEM((tm, tn), F32)],
        compiler_params=_params("arbitrary", "arbitrary", "arbitrary"),
        name="gated_merge",
    )(xn, ys, w_gate, w_branch, *[sc.w for sc in side])


def _ffn_up_kernel(x_ref, wg_ref, wv_ref, cw_ref, cb_ref, o_ref, g_s, *, cast_sides, tiles_per_seq):
    i = pl.program_id(1)
    tm = x_ref.shape[0]
    halo = SUBLANES

    @pl.when(i % tiles_per_seq == 0)
    def _():
        g_s[0:halo, :] = jnp.zeros((halo, g_s.shape[1]), F32)

    @pl.when(i % tiles_per_seq != 0)
    def _():
        g_s[0:halo, :] = g_s[tm:tm + halo, :]

    cast_sides()
    x = x_ref[...]
    g_s[halo:tm + halo, :] = _dot(x, wg_ref[...])
    val = _dot(x, wv_ref[...])
    cw = cw_ref[...]
    conv = (cw[0:1] * g_s[halo - 2:tm + halo - 2, :] + cw[1:2] * g_s[halo - 1:tm + halo - 1, :]
            + cw[2:3] * g_s[halo:tm + halo, :] + cb_ref[...])
    o_ref[...] = (conv * jax.nn.sigmoid(conv) * val).astype(o_ref.dtype)


def ffn_up(xn, w_up, conv_w, conv_b, seq_len, sides, *, tm, tn):
    m, d = xn.shape
    dff = conv_w.shape[1]
    nblk = dff // tn
    grid = (nblk, m // tm)
    side = _side_casts(sides, grid)
    body = functools.partial(_ffn_up_kernel, tiles_per_seq=seq_len // tm)
    return pl.pallas_call(
        _with_side_casts(body, 5, 1, len(side)),
        grid=grid,
        in_specs=[pl.BlockSpec((tm, d), lambda j, i: (i, 0)),
                  pl.BlockSpec((d, tn), lambda j, i: (0, j)),
                  pl.BlockSpec((d, tn), lambda j, i: (0, nblk + j)),
                  pl.BlockSpec((CONV_W, tn), lambda j, i: (0, j)),
                  pl.BlockSpec((1, tn), lambda j, i: (0, j))]
                 + [sc.in_spec for sc in side],
        out_specs=[pl.BlockSpec((tm, tn), lambda j, i: (i, j))] + [sc.out_spec for sc in side],
        out_shape=[jax.ShapeDtypeStruct((m, dff), BF16)] + [sc.out_shape for sc in side],
        scratch_shapes=[pltpu.VMEM((tm + SUBLANES, tn), F32)],
        compiler_params=_params("arbitrary", "arbitrary"),
        name="ffn_up_conv",
    )(xn, w_up, w_up, conv_w, conv_b.reshape(1, dff), *[sc.w for sc in side])


def kernel(x, positions, norm_mix, w_in, hgrn_lower_bounds, hgrn_out_norm, q_norm, k_norm,
           sg_ln_g, sg_ln_b, sg_w, sg_b, w_gate, w_branch, w_out, norm_ffn, w_up,
           ffn_conv_w, ffn_conv_b, w_down):
    bsz, s, d = x.shape
    m = bsz * s
    depth = w_in.shape[0]
    in_cols = w_in.shape[2]
    a_cols = 4 * BRANCH_WIDTH
    b_cols = len(B_PATTERNS) * 3 * BRANCH_WIDTH
    tables = _rope_tables(positions)
    w_branch2d = w_branch.reshape(depth, N_BRANCHES * BRANCH_WIDTH, d)
    w_in_l = w_in[0].astype(BF16)
    w_branch_l = w_branch2d[0].astype(BF16)
    xf = x.reshape(m, d)
    for l in range(depth):
        xn = rmsnorm(xf, norm_mix[l])
        planes, w_gate_l = matmul_planes(xn, w_in_l, None, [(w_gate, l, (512, 1024))],
                                         tm=1024, tn=1024)
        planes4 = planes.reshape(in_cols // LANES, bsz, s, LANES)
        ys = hgrn2_mixer(planes4, hgrn_lower_bounds, hgrn_out_norm[l], l)
        ys = dilated_mixer(planes4, tables, q_norm[l], k_norm[l], a_cols, ys, 1)
        ys = spatial_gating(planes, sg_ln_g[l], sg_ln_b[l], sg_w[l], sg_b[l], a_cols + b_cols,
                            ys.reshape(N_BRANCHES, m, BRANCH_WIDTH), 2)
        merged, w_up_l, w_out_l = gated_merge(
            xn, ys, w_gate_l, w_branch_l.reshape(N_BRANCHES, BRANCH_WIDTH, d),
            [(w_up, l, (512, 1024)), (w_out, l, (512, 512))], tm=1024, tn=512)
        xf = matmul_residual(merged, w_out_l, None, xf, tm=1024, tn=1024, tk=d)
        xn = rmsnorm(xf, norm_ffn[l])
        sides = [(w_down, l, (512, 512))]
        if l + 1 < depth:
            sides += [(w_in, l + 1, (512, 1024)), (w_branch2d, l + 1, (512, 512))]
        h, w_down_l, *next_layer = ffn_up(xn, w_up_l, ffn_conv_w[l], ffn_conv_b[l], s, sides,
                                          tm=1024, tn=512)
        if next_layer:
            w_in_l, w_branch_l = next_layer
        xf = matmul_residual(h, w_down_l, None, xf, tm=1024, tn=1024, tk=d)
    return xf.reshape(bsz, s, d)
```

```python
import functools
import math

import jax
import jax.numpy as jnp
from jax import lax
from jax.experimental import pallas as pl
from jax.experimental.pallas import tpu as pltpu

F32 = jnp.float32
BF16 = jnp.bfloat16

EPS = 1e-6
NEG_BIG = -1e30

LANES = 128
SUBLANES = 8
VMEM_LIMIT = 56 * 1024 * 1024

HEAD_DIM = 128
N_HEADS = 8
BRANCH_WIDTH = N_HEADS * HEAD_DIM
N_BRANCHES = 3

A_CHUNK = 64
A_SUB = 16
A_STEP_ROWS = 512
A_STEP_HEADS = 8
A_UNROLL = 2

B_PATTERNS = ((128, 1), (512, 4), (2048, 16))
B_BLOCK = 128
B_TILE = 2048
B_PREP_ROWS = 256
B_BATCH = 8
ROPE_THETA = 500000.0
ROT_DIM = HEAD_DIM // 4
ROT_HALF = ROT_DIM // 2

C_CHUNK = 128
C_STEP_ROWS = 512
CONV_W = 3

NORM_ROWS = 512
MM_TM = 1024
MM_TN = 1024
MERGE_TN = 512
FFN_TN = 512
CAST_BLOCK = (512, 512)
CAST_BLOCK_WIDE = (512, 1024)


def _params(*sem):
    return pltpu.CompilerParams(dimension_semantics=sem, vmem_limit_bytes=VMEM_LIMIT)


def _dot(a, b):
    return jnp.dot(a, b, preferred_element_type=F32)


def _dot_nt(a, b):
    return lax.dot_general(a, b, (((1,), (1,)), ((), ())), preferred_element_type=F32)


def _dot_tn(a, b):
    return lax.dot_general(a, b, (((0,), (0,)), ((), ())), preferred_element_type=F32)


def _rmsnorm_kernel(x_ref, g_ref, o_ref):
    x = x_ref[...]
    ms = jnp.mean(x * x, axis=-1, keepdims=True)
    o_ref[...] = (x * lax.rsqrt(ms + EPS) * g_ref[...]).astype(o_ref.dtype)


def rmsnorm(x, g, *, rows=NORM_ROWS):
    m, d = x.shape
    return pl.pallas_call(
        _rmsnorm_kernel,
        grid=(m // rows,),
        in_specs=[pl.BlockSpec((rows, d), lambda i: (i, 0)),
                  pl.BlockSpec((1, d), lambda i: (0, 0))],
        out_specs=pl.BlockSpec((rows, d), lambda i: (i, 0)),
        out_shape=jax.ShapeDtypeStruct((m, d), BF16),
        compiler_params=_params("parallel"),
        name="rmsnorm",
    )(x, g.reshape(1, d))


def _weight_spec(w, layer, block, index):
    if layer is None:
        return pl.BlockSpec(block, index)
    return pl.BlockSpec((None,) + block, lambda *ids: (layer,) + index(*ids))


class SideCast:
    def __init__(self, w, layer, block, grid):
        _, r, c = w.shape
        br, bc = block
        n_col = c // bc
        n_blocks = (r // br) * n_col
        assert r % br == 0 and c % bc == 0 and n_blocks <= math.prod(grid)

        def block_index(*ids):
            t = ids[0]
            for extent, i in zip(grid[1:], ids[1:]):
                t = t * extent + i
            t = jnp.minimum(t, n_blocks - 1)
            return t // n_col, t % n_col

        self.w = w
        self.in_spec = pl.BlockSpec((None, br, bc), lambda *ids: (layer,) + block_index(*ids))
        self.out_spec = pl.BlockSpec((br, bc), block_index)
        self.out_shape = jax.ShapeDtypeStruct((r, c), BF16)


def _side_casts(sides, grid):
    return [SideCast(w, layer, block, grid) for w, layer, block in sides]


def _with_side_casts(body, n_in, n_out, n_side):
    def wrapped(*refs):
        ins, rest = refs[:n_in], refs[n_in:]
        side_in, rest = rest[:n_side], rest[n_side:]
        outs, rest = rest[:n_out], rest[n_out:]
        side_out, scratch = rest[:n_side], rest[n_side:]

        def cast_sides():
            for src, dst in zip(side_in, side_out):
                dst[...] = src[...].astype(dst.dtype)

        body(*ins, *outs, *scratch, cast_sides=cast_sides)

    return wrapped


def _matmul_kernel(x_ref, w_ref, o_ref, *, cast_sides):
    cast_sides()
    o_ref[...] = _dot(x_ref[...], w_ref[...])


def matmul(x, w, layer, sides, *, tm, tn):
    m, k = x.shape
    n = w.shape[-1]
    grid = (m // tm, n // tn)
    side = _side_casts(sides, grid)
    return pl.pallas_call(
        _with_side_casts(_matmul_kernel, 2, 1, len(side)),
        grid=grid,
        in_specs=[pl.BlockSpec((tm, k), lambda i, j: (i, 0)),
                  _weight_spec(w, layer, (k, tn), lambda i, j: (0, j))]
                 + [sc.in_spec for sc in side],
        out_specs=[pl.BlockSpec((tm, tn), lambda i, j: (i, j))] + [sc.out_spec for sc in side],
        out_shape=[jax.ShapeDtypeStruct((m, n), F32)] + [sc.out_shape for sc in side],
        compiler_params=_params("arbitrary", "arbitrary"),
        name="proj_matmul",
    )(x, w, *[sc.w for sc in side])


def _matmul_res_kernel(x_ref, w_ref, r_ref, o_ref):
    k = pl.program_id(2)

    @pl.when(k == 0)
    def _():
        o_ref[...] = r_ref[...]

    o_ref[...] += _dot(x_ref[...], w_ref[...])


def matmul_residual(x, w, layer, res, *, tm, tn, tk):
    m, k = x.shape
    n = w.shape[-1]
    return pl.pallas_call(
        _matmul_res_kernel,
        grid=(m // tm, n // tn, k // tk),
        in_specs=[pl.BlockSpec((tm, tk), lambda i, j, kk: (i, kk)),
                  _weight_spec(w, layer, (tk, tn), lambda i, j, kk: (kk, j)),
                  pl.BlockSpec((tm, tn), lambda i, j, kk: (i, j))],
        out_specs=pl.BlockSpec((tm, tn), lambda i, j, kk: (i, j)),
        out_shape=jax.ShapeDtypeStruct((m, n), F32),
        compiler_params=_params("parallel", "parallel", "arbitrary"),
        name="matmul_residual",
    )(x, w, res)


def _hgrn_kernel(lbp_ref, gain_ref, q_ref, f_ref, i_ref, g_ref, o_ref, st_ref, ck_s, v_s,
                 *, layer, n_chunks, n_heads):
    n = pl.program_id(2)

    @pl.when(n == 0)
    def _():
        st_ref[...] = jnp.zeros_like(st_ref)

    for other in range(1, N_BRANCHES):
        o_ref[other] = jnp.zeros(o_ref.shape[1:], o_ref.dtype)

    lbp = lbp_ref[...]
    e = jnp.exp(lbp - jnp.max(lbp, axis=0, keepdims=True))
    sm = e / jnp.sum(e, axis=0, keepdims=True)
    cs = sm[0:1]
    for li in range(1, layer + 1):
        cs = cs + sm[li:li + 1]
    lb_all = cs - sm[0:1]

    gain = gain_ref[...]
    n_sub = A_CHUNK // A_SUB
    row = lax.broadcasted_iota(jnp.int32, (A_CHUNK, A_CHUNK), 0)
    col = lax.broadcasted_iota(jnp.int32, (A_CHUNK, A_CHUNK), 1)
    tri = (row >= col).astype(F32)
    off_mask = (row // A_SUB) > (col // A_SUB)
    rblk = lax.broadcasted_iota(jnp.int32, (A_CHUNK, HEAD_DIM), 0) // A_SUB
    half_row = lax.broadcasted_iota(jnp.int32, (SUBLANES, HEAD_DIM), 0)
    scale = HEAD_DIM ** -0.5

    def head_chunk(hh, r0):
        sl = pl.ds(r0, A_CHUNK)
        hs = slice(hh * HEAD_DIM, (hh + 1) * HEAD_DIM)
        lb = lb_all[:, hs]
        q = q_ref[sl, hs] * scale
        v = i_ref[sl, hs]
        f = lb + (1.0 - lb) * jax.nn.sigmoid(f_ref[sl, hs])
        kk = 1.0 - f
        cum = jnp.dot(tri, jnp.log2(f), precision=lax.Precision.HIGHEST,
                      preferred_element_type=F32)
        ends = [cum[(j + 1) * A_SUB - 1:(j + 1) * A_SUB, :] for j in range(n_sub)]
        last = ends[-1]
        eblk = jnp.concatenate([jnp.broadcast_to(ej, (A_SUB, HEAD_DIM)) for ej in ends], axis=0)
        khat = kk * jnp.exp2(eblk - cum)
        st = st_ref[hh]

        o = _dot_nt((q * jnp.exp2(cum)).astype(BF16), st.astype(BF16))

        s_off = jnp.zeros((A_CHUNK, A_CHUNK), F32)
        for j in range(n_sub - 1):
            qj = q * jnp.exp2(jnp.minimum(cum - ends[j], 0.0))
            kj = jnp.where(rblk == j, khat, 0.0)
            s_off = s_off + _dot_nt(qj.astype(BF16), kj.astype(BF16))
        s_off = jnp.where(off_mask, s_off, 0.0)
        o = o + _dot(s_off.astype(BF16), v.astype(BF16))

        ck_s[hh] = cum - jnp.log2(kk)
        v_s[hh] = v
        diag = []
        for b in range(n_sub):
            base = b * A_SUB
            q_lo, q_hi = q[base:base + SUBLANES], q[base + SUBLANES:base + A_SUB]
            c_lo, c_hi = cum[base:base + SUBLANES], cum[base + SUBLANES:base + A_SUB]
            o_lo = jnp.zeros((SUBLANES, HEAD_DIM), F32)
            o_hi = jnp.zeros((SUBLANES, HEAD_DIM), F32)
            for s in range(A_SUB):
                r = base + s
                c_row = ck_s[hh, r:r + 1, :]
                v_row = v_s[hh, r:r + 1, :]
                if s < SUBLANES:
                    dec = jnp.exp2(jnp.where(half_row >= s, c_lo - c_row, NEG_BIG))
                    o_lo = o_lo + jnp.sum(q_lo * dec, axis=-1, keepdims=True) * v_row
                    dec = jnp.exp2(c_hi - c_row)
                else:
                    dec = jnp.exp2(jnp.where(half_row >= s - SUBLANES, c_hi - c_row, NEG_BIG))
                o_hi = o_hi + jnp.sum(q_hi * dec, axis=-1, keepdims=True) * v_row
            diag += [o_lo, o_hi]
        o = o + jnp.concatenate(diag, axis=0)

        ktil = khat * jnp.exp2(last - eblk)
        st_ref[hh] = st * jnp.exp2(last) + _dot_tn(v.astype(BF16), ktil.astype(BF16))

        ms = jnp.mean(o * o, axis=-1, keepdims=True)
        on = o * lax.rsqrt(ms + EPS) * gain
        g = g_ref[sl, hs]
        o_ref[0, sl, hs] = (on * (g * jax.nn.sigmoid(g))).astype(o_ref.dtype)

    def chunk(c, carry):
        r0 = pl.multiple_of(c * A_CHUNK, A_CHUNK)
        for hh in range(n_heads):
            head_chunk(hh, r0)
        return carry

    lax.fori_loop(0, n_chunks, chunk, 0, unroll=A_UNROLL)


def hgrn2_mixer(proj3, lower_bounds, out_gain, layer):
    bsz, s, _ = proj3.shape
    depth = lower_bounds.shape[0]
    ts = min(A_STEP_ROWS, s)
    hb = A_STEP_HEADS
    n_hg = N_HEADS // hb
    wide = hb * HEAD_DIM
    blk = lambda part: pl.BlockSpec((None, ts, wide), lambda b, h, n: (b, n, part * n_hg + h))
    return pl.pallas_call(
        functools.partial(_hgrn_kernel, layer=layer, n_chunks=ts // A_CHUNK, n_heads=hb),
        grid=(bsz, n_hg, s // ts),
        in_specs=[pl.BlockSpec((depth, wide), lambda b, h, n: (0, h)),
                  pl.BlockSpec((1, HEAD_DIM), lambda b, h, n: (0, 0)),
                  blk(0), blk(1), blk(2), blk(3)],
        out_specs=pl.BlockSpec((N_BRANCHES, None, ts, wide), lambda b, h, n: (0, b, n, h)),
        out_shape=jax.ShapeDtypeStruct((N_BRANCHES, bsz, s, BRANCH_WIDTH), BF16),
        scratch_shapes=[pltpu.VMEM((hb, HEAD_DIM, HEAD_DIM), F32)]
                       + [pltpu.VMEM((hb, A_CHUNK, HEAD_DIM), F32)] * 2,
        compiler_params=_params("parallel", "parallel", "arbitrary"),
        name="hgrn2",
    )(lower_bounds, out_gain.reshape(1, HEAD_DIM), proj3, proj3, proj3, proj3)


def _rope_tables(positions):
    inv = jnp.power(jnp.float32(ROPE_THETA), -jnp.arange(0, ROT_DIM, 2, dtype=F32) / ROT_DIM)
    ang = positions.astype(F32)[..., None] * inv
    cos = jnp.cos(ang)
    sin = jnp.sin(ang)
    pad = jnp.zeros(ang.shape[:-1] + (HEAD_DIM - ROT_DIM,), F32)
    cos_f = jnp.concatenate([cos, cos, pad + 1.0], axis=-1)
    sin_f = jnp.concatenate([sin, sin, pad], axis=-1)
    return cos_f, sin_f


def _dilated_kernel(qg_ref, kg_ref, cos_ref, sin_ref, *rest):
    n_grp = len(B_PATTERNS)
    qkv = rest[:3 * n_grp]
    y_ref = rest[3 * n_grp + 1]
    scratch = list(rest[3 * n_grp + 2:])
    qn_s, kn_s, kc, vc, num_s, den_s, m_s = (scratch[i * n_grp:(i + 1) * n_grp] for i in range(7))
    s_s, p_s, vb_s = scratch[7 * n_grp:]
    n = pl.program_id(2)

    @pl.when(n == 0)
    def _():
        for g in range(n_grp):
            kc[g][...] = jnp.zeros_like(kc[g])
            vc[g][...] = jnp.zeros_like(vc[g])

    q_gain = qg_ref[...] * (HEAD_DIM ** -0.5 * math.log2(math.e))
    k_gain = kg_ref[...]

    src = lax.broadcasted_iota(jnp.int32, (HEAD_DIM, HEAD_DIM), 0)
    dst = lax.broadcasted_iota(jnp.int32, (HEAD_DIM, HEAD_DIM), 1)
    rot = jnp.where(jnp.logical_and(dst < ROT_HALF, src == dst + ROT_HALF), -1.0,
                    jnp.where(jnp.logical_and(jnp.logical_and(dst >= ROT_HALF, dst < ROT_DIM),
                                              src == dst - ROT_HALF), 1.0, 0.0)).astype(BF16)
    rot2 = jnp.concatenate([rot, rot], axis=0)

    def norm_rope(x, gain, cos, sin):
        ms = jnp.mean(x * x, axis=-1, keepdims=True)
        xn = x * lax.rsqrt(ms + EPS) * gain
        hi = xn.astype(BF16)
        lo = (xn - hi.astype(F32)).astype(BF16)
        return xn * cos + _dot(jnp.concatenate([hi, lo], axis=1), rot2) * sin

    def prep(c, carry):
        rows = pl.ds(pl.multiple_of(c * B_PREP_ROWS, B_PREP_ROWS), B_PREP_ROWS)
        cos, sin = cos_ref[rows, :], sin_ref[rows, :]
        for g in range(n_grp):
            q_ref, k_ref = qkv[3 * g], qkv[3 * g + 1]
            qn_s[g][rows, :] = norm_rope(q_ref[rows, :], q_gain[g:g + 1, :], cos, sin)
            kn_s[g][rows, :] = norm_rope(k_ref[rows, :], k_gain[g:g + 1, :], cos, sin)
        return carry

    lax.fori_loop(0, B_TILE // B_PREP_ROWS, prep, 0)

    qi = lax.broadcasted_iota(jnp.int32, (B_BLOCK, 2 * B_BLOCK), 0)
    ci = lax.broadcasted_iota(jnp.int32, (B_BLOCK, 2 * B_BLOCK), 1)
    cur_ok = jnp.logical_and(ci >= B_BLOCK, ci - B_BLOCK <= qi)
    prev_ok = jnp.logical_and(ci < B_BLOCK, ci >= qi)
    mask_inner = jnp.logical_or(cur_ok, prev_ok)
    mask_first = jnp.logical_or(cur_ok, jnp.logical_and(prev_ok, n > 0))
    ones = jnp.ones((2 * B_BLOCK, HEAD_DIM), BF16)

    for g, (win, dil) in enumerate(B_PATTERNS):
        v_ref = qkv[3 * g + 2]
        n_blk = B_TILE // (B_BLOCK * dil)
        blocks = [(r, j) for r in range(dil) for j in range(n_blk)]
        k_prev = v_prev = None
        for b0 in range(0, len(blocks), B_BATCH):
            batch = blocks[b0:b0 + B_BATCH]
            rows_of = []
            for i, (r, j) in enumerate(batch):
                start = j * B_BLOCK * dil + r
                rows = pl.ds(start, B_BLOCK) if dil == 1 else pl.ds(start, B_BLOCK, stride=dil)
                rows_of.append(rows)
                blk = slice(i * B_BLOCK, (i + 1) * B_BLOCK)
                if j == 0:
                    k_prev, v_prev = kc[g][r], vc[g][r]
                k_cur = kn_s[g][rows, :].astype(BF16)
                v_cur = v_ref[rows, :].astype(BF16)
                q = qn_s[g][rows, :].astype(BF16)
                s = _dot_nt(q, jnp.concatenate([k_prev, k_cur], axis=0))
                s_s[blk, :] = jnp.where(mask_first if j == 0 else mask_inner, s, NEG_BIG)
                vb_s[i, 0:B_BLOCK, :] = v_prev
                vb_s[i, B_BLOCK:2 * B_BLOCK, :] = v_cur
                if j == n_blk - 1:
                    kc[g][r] = k_cur
                    vc[g][r] = v_cur
                k_prev, v_prev = k_cur, v_cur
            live = slice(0, len(batch) * B_BLOCK)
            m = jnp.max(jnp.maximum(s_s[live, 0:B_BLOCK], s_s[live, B_BLOCK:2 * B_BLOCK]),
                        axis=-1, keepdims=True)
            p_s[live, :] = jnp.exp2(s_s[live, :] - m).astype(BF16)
            for i, rows in enumerate(rows_of):
                blk = slice(i * B_BLOCK, (i + 1) * B_BLOCK)
                pv = _dot(p_s[blk, :], jnp.concatenate([vb_s[i], ones], axis=1))
                num_s[g][rows, :] = pv[:, :HEAD_DIM]
                den_s[g][rows, :] = pv[:, HEAD_DIM:]
                m_s[g][rows, :] = jnp.broadcast_to(m[blk], (B_BLOCK, HEAD_DIM))

    def combine(c, carry):
        rows = pl.ds(pl.multiple_of(c * B_BLOCK, B_BLOCK), B_BLOCK)
        ms = [ref[rows, :] for ref in m_s]
        top = functools.reduce(jnp.maximum, ms)
        w = [jnp.exp2(x - top) for x in ms]
        num = functools.reduce(jnp.add, [wg * ref[rows, :] for wg, ref in zip(w, num_s)])
        den = functools.reduce(jnp.add, [wg * ref[rows, :] for wg, ref in zip(w, den_s)])
        y_ref[rows, :] = (num / den).astype(y_ref.dtype)
        return carry

    lax.fori_loop(0, B_TILE // B_BLOCK, combine, 0)


def dilated_mixer(proj3, tables, q_gain, k_gain, col0, ys, branch):
    bsz, s, _ = proj3.shape
    n_grp = len(B_PATTERNS)
    assert s % B_TILE == 0

    def part(g, p):
        base = (col0 + (3 * g + p) * BRANCH_WIDTH) // HEAD_DIM
        return pl.BlockSpec((None, B_TILE, HEAD_DIM), lambda b, h, n: (b, n, base + h))

    tab = pl.BlockSpec((None, B_TILE, HEAD_DIM), lambda b, h, n: (b, n, 0))
    gain = pl.BlockSpec((n_grp, HEAD_DIM), lambda b, h, n: (0, 0))
    for win, dil in B_PATTERNS:
        assert win // dil == B_BLOCK and B_TILE % (B_BLOCK * dil) == 0
    token_f32 = [pltpu.VMEM((B_TILE, HEAD_DIM), F32)] * n_grp
    carry = [pltpu.VMEM((dil, B_BLOCK, HEAD_DIM), BF16) for _, dil in B_PATTERNS]
    scratch = (token_f32 + token_f32 + carry + carry + token_f32 + token_f32 + token_f32
               + [pltpu.VMEM((B_BATCH * B_BLOCK, 2 * B_BLOCK), F32),
                  pltpu.VMEM((B_BATCH * B_BLOCK, 2 * B_BLOCK), BF16),
                  pltpu.VMEM((B_BATCH, 2 * B_BLOCK, HEAD_DIM), BF16)])
    return pl.pallas_call(
        _dilated_kernel,
        grid=(bsz, N_HEADS, s // B_TILE),
        in_specs=[gain, gain, tab, tab]
                 + [part(g, p) for g in range(n_grp) for p in range(3)]
                 + [pl.BlockSpec(memory_space=pl.ANY)],
        out_specs=pl.BlockSpec((None, None, B_TILE, HEAD_DIM), lambda b, h, n: (branch, b, n, h)),
        out_shape=jax.ShapeDtypeStruct(ys.shape, ys.dtype),
        input_output_aliases={4 + 3 * n_grp: 0},
        scratch_shapes=scratch,
        compiler_params=_params("parallel", "parallel", "arbitrary"),
        name="dilated_attn",
    )(q_gain, k_gain, *tables, *([proj3] * (3 * n_grp)), ys)


def _spatial_kernel(u_ref, v_ref, lng_ref, lnb_ref, w_ref, b_ref, ys_ref, o_ref):
    inv_sqrt2 = 1.0 / math.sqrt(2.0)

    def gelu(x):
        return 0.5 * x * (1.0 + lax.erf(x * inv_sqrt2))

    row = lax.broadcasted_iota(jnp.int32, (C_CHUNK, C_CHUNK), 0)
    col = lax.broadcasted_iota(jnp.int32, (C_CHUNK, C_CHUNK), 1)
    causal = row >= col
    bias = b_ref[...]
    w_causal = [jnp.where(causal, w_ref[g], 0.0).astype(BF16) for g in range(N_HEADS)]
    for c in range(u_ref.shape[0] // C_CHUNK):
        rows = slice(c * C_CHUNK, (c + 1) * C_CHUNK)
        v = gelu(v_ref[rows, :])
        mu = jnp.mean(v, axis=-1, keepdims=True)
        vc = v - mu
        var = jnp.mean(vc * vc, axis=-1, keepdims=True)
        vn = vc * lax.rsqrt(var + EPS) * lng_ref[...] + lnb_ref[...]
        for g in range(N_HEADS):
            gs = slice(g * HEAD_DIM, (g + 1) * HEAD_DIM)
            mixed = _dot(w_causal[g], vn[:, gs].astype(BF16)) + bias[:, g:g + 1]
            o_ref[rows, gs] = (gelu(u_ref[rows, gs]) * mixed).astype(o_ref.dtype)


def spatial_gating(proj, ln_g, ln_b, w_s, b_s, col0, ys, branch):
    m, _ = proj.shape
    ublk = col0 // BRANCH_WIDTH
    rows = C_STEP_ROWS
    return pl.pallas_call(
        _spatial_kernel,
        grid=(m // rows,),
        in_specs=[pl.BlockSpec((rows, BRANCH_WIDTH), lambda i: (i, ublk)),
                  pl.BlockSpec((rows, BRANCH_WIDTH), lambda i: (i, ublk + 1)),
                  pl.BlockSpec((1, BRANCH_WIDTH), lambda i: (0, 0)),
                  pl.BlockSpec((1, BRANCH_WIDTH), lambda i: (0, 0)),
                  pl.BlockSpec((N_HEADS, C_CHUNK, C_CHUNK), lambda i: (0, 0, 0)),
                  pl.BlockSpec((C_CHUNK, N_HEADS), lambda i: (0, 0)),
                  pl.BlockSpec(memory_space=pl.ANY)],
        out_specs=pl.BlockSpec((None, rows, BRANCH_WIDTH), lambda i: (branch, i, 0)),
        out_shape=jax.ShapeDtypeStruct(ys.shape, ys.dtype),
        input_output_aliases={6: 0},
        compiler_params=_params("parallel"),
        name="spatial_gating",
    )(proj, proj, ln_g.reshape(1, -1), ln_b.reshape(1, -1), w_s, b_s.T, ys)


def _merge_kernel(xn_ref, y_ref, wg_ref, wb_ref, o_ref, acc_ref, *, cast_sides):
    @pl.when(pl.program_id(2) == 0)
    def _():
        acc_ref[...] = jnp.zeros_like(acc_ref)

    cast_sides()
    gate = jax.nn.sigmoid(_dot(xn_ref[...], wg_ref[...]))
    total = acc_ref[...] + gate * _dot(y_ref[...], wb_ref[...])
    acc_ref[...] = total
    o_ref[...] = total.astype(o_ref.dtype)


def gated_merge(xn, ys, w_gate, w_branch, sides, *, tm, tn):
    m, d = xn.shape
    w = ys.shape[2]
    nblk = d // tn
    grid = (m // tm, nblk, N_BRANCHES)
    side = _side_casts(sides, grid)
    return pl.pallas_call(
        _with_side_casts(_merge_kernel, 4, 1, len(side)),
        grid=grid,
        in_specs=[pl.BlockSpec((tm, d), lambda a, b, i: (a, 0)),
                  pl.BlockSpec((None, tm, w), lambda a, b, i: (i, a, 0)),
                  pl.BlockSpec((d, tn), lambda a, b, i: (0, i * nblk + b)),
                  pl.BlockSpec((None, w, tn), lambda a, b, i: (i, 0, b))]
                 + [sc.in_spec for sc in side],
        out_specs=[pl.BlockSpec((tm, tn), lambda a, b, i: (a, b))] + [sc.out_spec for sc in side],
        out_shape=[jax.ShapeDtypeStruct((m, d), BF16)] + [sc.out_shape for sc in side],
        scratch_shapes=[pltpu.VMEM((tm, tn), F32)],
        compiler_params=_params("arbitrary", "arbitrary", "arbitrary"),
        name="gated_merge",
    )(xn, ys, w_gate, w_branch, *[sc.w for sc in side])


def _ffn_up_kernel(x_ref, wg_ref, wv_ref, cw_ref, cb_ref, o_ref, g_s, *, cast_sides, tiles_per_seq):
    i = pl.program_id(1)
    tm = x_ref.shape[0]
    halo = SUBLANES

    @pl.when(i % tiles_per_seq == 0)
    def _():
        g_s[0:halo, :] = jnp.zeros((halo, g_s.shape[1]), F32)

    @pl.when(i % tiles_per_seq != 0)
    def _():
        g_s[0:halo, :] = g_s[tm:tm + halo, :]

    cast_sides()
    x = x_ref[...]
    g_s[halo:tm + halo, :] = _dot(x, wg_ref[...])
    val = _dot(x, wv_ref[...])
    cw = cw_ref[...]
    conv = (cw[0:1] * g_s[halo - 2:tm + halo - 2, :] + cw[1:2] * g_s[halo - 1:tm + halo - 1, :]
            + cw[2:3] * g_s[halo:tm + halo, :] + cb_ref[...])
    o_ref[...] = (conv * jax.nn.sigmoid(conv) * val).astype(o_ref.dtype)


def ffn_up(xn, w_up, conv_w, conv_b, seq_len, sides, *, tm, tn):
    m, d = xn.shape
    dff = conv_w.shape[1]
    nblk = dff // tn
    grid = (nblk, m // tm)
    side = _side_casts(sides, grid)
    body = functools.partial(_ffn_up_kernel, tiles_per_seq=seq_len // tm)
    return pl.pallas_call(
        _with_side_casts(body, 5, 1, len(side)),
        grid=grid,
        in_specs=[pl.BlockSpec((tm, d), lambda j, i: (i, 0)),
                  pl.BlockSpec((d, tn), lambda j, i: (0, j)),
                  pl.BlockSpec((d, tn), lambda j, i: (0, nblk + j)),
                  pl.BlockSpec((CONV_W, tn), lambda j, i: (0, j)),
                  pl.BlockSpec((1, tn), lambda j, i: (0, j))]
                 + [sc.in_spec for sc in side],
        out_specs=[pl.BlockSpec((tm, tn), lambda j, i: (i, j))] + [sc.out_spec for sc in side],
        out_shape=[jax.ShapeDtypeStruct((m, dff), BF16)] + [sc.out_shape for sc in side],
        scratch_shapes=[pltpu.VMEM((tm + SUBLANES, tn), F32)],
        compiler_params=_params("arbitrary", "arbitrary"),
        name="ffn_up_conv",
    )(xn, w_up, w_up, conv_w, conv_b.reshape(1, dff), *[sc.w for sc in side])


def kernel(x, positions, norm_mix, w_in, hgrn_lower_bounds, hgrn_out_norm, q_norm, k_norm,
           sg_ln_g, sg_ln_b, sg_w, sg_b, w_gate, w_branch, w_out, norm_ffn, w_up,
           ffn_conv_w, ffn_conv_b, w_down):
    bsz, s, d = x.shape
    m = bsz * s
    depth = w_in.shape[0]
    in_cols = w_in.shape[2]
    a_cols = 4 * BRANCH_WIDTH
    b_cols = len(B_PATTERNS) * 3 * BRANCH_WIDTH
    tables = _rope_tables(positions)
    w_branch2d = w_branch.reshape(depth, N_BRANCHES * BRANCH_WIDTH, d)
    w_in_l = w_in[0].astype(BF16)
    w_branch_l = w_branch2d[0].astype(BF16)
    xf = x.reshape(m, d)
    for l in range(depth):
        xn = rmsnorm(xf, norm_mix[l])
        proj, w_gate_l = matmul(xn, w_in_l, None, [(w_gate, l, CAST_BLOCK_WIDE)], tm=MM_TM, tn=MM_TN)
        proj3 = proj.reshape(bsz, s, in_cols)
        ys = hgrn2_mixer(proj3, hgrn_lower_bounds, hgrn_out_norm[l], l)
        ys = dilated_mixer(proj3, tables, q_norm[l], k_norm[l], a_cols, ys, 1)
        ys = spatial_gating(proj, sg_ln_g[l], sg_ln_b[l], sg_w[l], sg_b[l], a_cols + b_cols,
                            ys.reshape(N_BRANCHES, m, BRANCH_WIDTH), 2)
        merged, w_up_l, w_out_l = gated_merge(
            xn, ys, w_gate_l, w_branch_l.reshape(N_BRANCHES, BRANCH_WIDTH, d),
            [(w_up, l, CAST_BLOCK_WIDE), (w_out, l, CAST_BLOCK)], tm=MM_TM, tn=MERGE_TN)
        xf = matmul_residual(merged, w_out_l, None, xf, tm=MM_TM, tn=MM_TN, tk=d)
        xn = rmsnorm(xf, norm_ffn[l])
        sides = [(w_down, l, CAST_BLOCK)]
        if l + 1 < depth:
            sides += [(w_in, l + 1, CAST_BLOCK_WIDE), (w_branch2d, l + 1, CAST_BLOCK)]
        h, w_down_l, *next_layer = ffn_up(xn, w_up_l, ffn_conv_w[l], ffn_conv_b[l], s, sides,
                                          tm=MM_TM, tn=FFN_TN)
        if next_layer:
            w_in_l, w_branch_l = next_layer
        xf = matmul_residual(h, w_down_l, None, xf, tm=MM_TM, tn=MM_TN, tk=d)
    return xf.reshape(bsz, s, d)
```

```python
import functools
import math

import jax
import jax.numpy as jnp
from jax import lax
from jax.experimental import pallas as pl
from jax.experimental.pallas import tpu as pltpu

F32 = jnp.float32
BF16 = jnp.bfloat16

EPS = 1e-6
NEG_BIG = -1e30

LANES = 128
SUBLANES = 8
VMEM_LIMIT = 56 * 1024 * 1024

HEAD_DIM = 128
N_HEADS = 8
BRANCH_WIDTH = N_HEADS * HEAD_DIM
N_BRANCHES = 3

A_CHUNK = 64
A_SUB = 16
A_STEP_ROWS = 512
A_STEP_HEADS = 8
A_UNROLL = 2

B_PATTERNS = ((128, 1), (512, 4), (2048, 16))
B_BLOCK = 128
B_TILE = 2048
B_PREP_ROWS = 256
B_BATCH = 8
ROPE_THETA = 500000.0
ROT_DIM = HEAD_DIM // 4
ROT_HALF = ROT_DIM // 2

C_CHUNK = 128
C_STEP_ROWS = 512
CONV_W = 3

NORM_ROWS = 512
MM_TM = 1024
MM_TN = 1024
MERGE_TN = 512
FFN_TN = 512
CAST_BLOCK = (512, 512)
CAST_BLOCK_WIDE = (512, 1024)


def _params(*sem):
    return pltpu.CompilerParams(dimension_semantics=sem, vmem_limit_bytes=VMEM_LIMIT)


def _dot(a, b):
    return jnp.dot(a, b, preferred_element_type=F32)


def _dot_nt(a, b):
    return lax.dot_general(a, b, (((1,), (1,)), ((), ())), preferred_element_type=F32)


def _dot_tn(a, b):
    return lax.dot_general(a, b, (((0,), (0,)), ((), ())), preferred_element_type=F32)


def _rmsnorm_kernel(x_ref, g_ref, o_ref):
    x = x_ref[...]
    ms = jnp.mean(x * x, axis=-1, keepdims=True)
    o_ref[...] = (x * lax.rsqrt(ms + EPS) * g_ref[...]).astype(o_ref.dtype)


def rmsnorm(x, g, *, rows=NORM_ROWS):
    m, d = x.shape
    return pl.pallas_call(
        _rmsnorm_kernel,
        grid=(m // rows,),
        in_specs=[pl.BlockSpec((rows, d), lambda i: (i, 0)),
                  pl.BlockSpec((1, d), lambda i: (0, 0))],
        out_specs=pl.BlockSpec((rows, d), lambda i: (i, 0)),
        out_shape=jax.ShapeDtypeStruct((m, d), BF16),
        compiler_params=_params("parallel"),
        name="rmsnorm",
    )(x, g.reshape(1, d))


def _weight_spec(w, layer, block, index):
    if layer is None:
        return pl.BlockSpec(block, index)
    return pl.BlockSpec((None,) + block, lambda *ids: (layer,) + index(*ids))


class SideCast:
    def __init__(self, w, layer, block, grid):
        _, r, c = w.shape
        br, bc = block
        n_col = c // bc
        n_blocks = (r // br) * n_col
        assert r % br == 0 and c % bc == 0 and n_blocks <= math.prod(grid)

        def block_index(*ids):
            t = ids[0]
            for extent, i in zip(grid[1:], ids[1:]):
                t = t * extent + i
            t = jnp.minimum(t, n_blocks - 1)
            return t // n_col, t % n_col

        self.w = w
        self.in_spec = pl.BlockSpec((None, br, bc), lambda *ids: (layer,) + block_index(*ids))
        self.out_spec = pl.BlockSpec((br, bc), block_index)
        self.out_shape = jax.ShapeDtypeStruct((r, c), BF16)


def _side_casts(sides, grid):
    return [SideCast(w, layer, block, grid) for w, layer, block in sides]


def _with_side_casts(body, n_in, n_out, n_side):
    def wrapped(*refs):
        ins, rest = refs[:n_in], refs[n_in:]
        side_in, rest = rest[:n_side], rest[n_side:]
        outs, rest = rest[:n_out], rest[n_out:]
        side_out, scratch = rest[:n_side], rest[n_side:]

        def cast_sides():
            for src, dst in zip(side_in, side_out):
                dst[...] = src[...].astype(dst.dtype)

        body(*ins, *outs, *scratch, cast_sides=cast_sides)

    return wrapped


def _matmul_kernel(x_ref, w_ref, o_ref, *, cast_sides):
    cast_sides()
    o_ref[...] = _dot(x_ref[...], w_ref[...])


def matmul(x, w, layer, sides, *, tm, tn):
    m, k = x.shape
    n = w.shape[-1]
    grid = (m // tm, n // tn)
    side = _side_casts(sides, grid)
    return pl.pallas_call(
        _with_side_casts(_matmul_kernel, 2, 1, len(side)),
        grid=grid,
        in_specs=[pl.BlockSpec((tm, k), lambda i, j: (i, 0)),
                  _weight_spec(w, layer, (k, tn), lambda i, j: (0, j))]
                 + [sc.in_spec for sc in side],
        out_specs=[pl.BlockSpec((tm, tn), lambda i, j: (i, j))] + [sc.out_spec for sc in side],
        out_shape=[jax.ShapeDtypeStruct((m, n), F32)] + [sc.out_shape for sc in side],
        compiler_params=_params("arbitrary", "arbitrary"),
        name="proj_matmul",
    )(x, w, *[sc.w for sc in side])


def _matmul_res_kernel(x_ref, w_ref, r_ref, o_ref):
    k = pl.program_id(2)

    @pl.when(k == 0)
    def _():
        o_ref[...] = r_ref[...] + _dot(x_ref[...], w_ref[...])

    @pl.when(k > 0)
    def _():
        o_ref[...] += _dot(x_ref[...], w_ref[...])


def matmul_residual(x, w, layer, res, *, tm, tn, tk):
    m, k = x.shape
    n = w.shape[-1]
    return pl.pallas_call(
        _matmul_res_kernel,
        grid=(m // tm, n // tn, k // tk),
        in_specs=[pl.BlockSpec((tm, tk), lambda i, j, kk: (i, kk)),
                  _weight_spec(w, layer, (tk, tn), lambda i, j, kk: (kk, j)),
                  pl.BlockSpec((tm, tn), lambda i, j, kk: (i, j))],
        out_specs=pl.BlockSpec((tm, tn), lambda i, j, kk: (i, j)),
        out_shape=jax.ShapeDtypeStruct((m, n), F32),
        compiler_params=_params("parallel", "parallel", "arbitrary"),
        name="matmul_residual",
    )(x, w, res)


def _hgrn_kernel(lbp_ref, gain_ref, q_ref, f_ref, i_ref, g_ref, o_ref, st_ref, ck_s, v_s,
                 *, layer, n_chunks, n_heads):
    n = pl.program_id(2)

    @pl.when(n == 0)
    def _():
        st_ref[...] = jnp.zeros_like(st_ref)

    for other in range(1, N_BRANCHES):
        o_ref[other] = jnp.zeros(o_ref.shape[1:], o_ref.dtype)

    lbp = lbp_ref[...]
    e = jnp.exp(lbp - jnp.max(lbp, axis=0, keepdims=True))
    sm = e / jnp.sum(e, axis=0, keepdims=True)
    cs = sm[0:1]
    for li in range(1, layer + 1):
        cs = cs + sm[li:li + 1]
    lb_all = cs - sm[0:1]

    gain = gain_ref[...]
    n_sub = A_CHUNK // A_SUB
    row = lax.broadcasted_iota(jnp.int32, (A_CHUNK, A_CHUNK), 0)
    col = lax.broadcasted_iota(jnp.int32, (A_CHUNK, A_CHUNK), 1)
    tri = (row >= col).astype(F32)
    off_mask = (row // A_SUB) > (col // A_SUB)
    rblk = lax.broadcasted_iota(jnp.int32, (A_CHUNK, HEAD_DIM), 0) // A_SUB
    half_row = lax.broadcasted_iota(jnp.int32, (SUBLANES, HEAD_DIM), 0)
    scale = HEAD_DIM ** -0.5

    def head_chunk(hh, r0):
        sl = pl.ds(r0, A_CHUNK)
        hs = slice(hh * HEAD_DIM, (hh + 1) * HEAD_DIM)
        lb = lb_all[:, hs]
        q = q_ref[sl, hs] * scale
        v = i_ref[sl, hs]
        f = lb + (1.0 - lb) * jax.nn.sigmoid(f_ref[sl, hs])
        kk = 1.0 - f
        cum = jnp.dot(tri, jnp.log2(f), precision=lax.Precision.HIGHEST,
                      preferred_element_type=F32)
        ends = [cum[(j + 1) * A_SUB - 1:(j + 1) * A_SUB, :] for j in range(n_sub)]
        last = ends[-1]
        eblk = jnp.concatenate([jnp.broadcast_to(ej, (A_SUB, HEAD_DIM)) for ej in ends], axis=0)
        khat = kk * jnp.exp2(eblk - cum)
        st = st_ref[hh]

        o = _dot_nt((q * jnp.exp2(cum)).astype(BF16), st.astype(BF16))

        s_off = jnp.zeros((A_CHUNK, A_CHUNK), F32)
        for j in range(n_sub - 1):
            qj = q * jnp.exp2(jnp.minimum(cum - ends[j], 0.0))
            kj = jnp.where(rblk == j, khat, 0.0)
            s_off = s_off + _dot_nt(qj.astype(BF16), kj.astype(BF16))
        s_off = jnp.where(off_mask, s_off, 0.0)
        o = o + _dot(s_off.astype(BF16), v.astype(BF16))

        ck_s[hh] = cum - jnp.log2(kk)
        v_s[hh] = v
        diag = []
        for b in range(n_sub):
            base = b * A_SUB
            q_lo, q_hi = q[base:base + SUBLANES], q[base + SUBLANES:base + A_SUB]
            c_lo, c_hi = cum[base:base + SUBLANES], cum[base + SUBLANES:base + A_SUB]
            o_lo = jnp.zeros((SUBLANES, HEAD_DIM), F32)
            o_hi = jnp.zeros((SUBLANES, HEAD_DIM), F32)
            for s in range(A_SUB):
                r = base + s
                c_row = ck_s[hh, r:r + 1, :]
                v_row = v_s[hh, r:r + 1, :]
                if s < SUBLANES:
                    dec = jnp.exp2(jnp.where(half_row >= s, c_lo - c_row, NEG_BIG))
                    o_lo = o_lo + jnp.sum(q_lo * dec, axis=-1, keepdims=True) * v_row
                    dec = jnp.exp2(c_hi - c_row)
                else:
                    dec = jnp.exp2(jnp.where(half_row >= s - SUBLANES, c_hi - c_row, NEG_BIG))
                o_hi = o_hi + jnp.sum(q_hi * dec, axis=-1, keepdims=True) * v_row
            diag += [o_lo, o_hi]
        o = o + jnp.concatenate(diag, axis=0)

        ktil = khat * jnp.exp2(last - eblk)
        st_ref[hh] = st * jnp.exp2(last) + _dot_tn(v.astype(BF16), ktil.astype(BF16))

        ms = jnp.mean(o * o, axis=-1, keepdims=True)
        on = o * lax.rsqrt(ms + EPS) * gain
        g = g_ref[sl, hs]
        o_ref[0, sl, hs] = (on * (g * jax.nn.sigmoid(g))).astype(o_ref.dtype)

    def chunk(c, carry):
        r0 = pl.multiple_of(c * A_CHUNK, A_CHUNK)
        for hh in range(n_heads):
            head_chunk(hh, r0)
        return carry

    lax.fori_loop(0, n_chunks, chunk, 0, unroll=A_UNROLL)


def hgrn2_mixer(proj3, lower_bounds, out_gain, layer):
    bsz, s, _ = proj3.shape
    depth = lower_bounds.shape[0]
    ts = min(A_STEP_ROWS, s)
    hb = A_STEP_HEADS
    n_hg = N_HEADS // hb
    wide = hb * HEAD_DIM
    blk = lambda part: pl.BlockSpec((None, ts, wide), lambda b, h, n: (b, n, part * n_hg + h))
    return pl.pallas_call(
        functools.partial(_hgrn_kernel, layer=layer, n_chunks=ts // A_CHUNK, n_heads=hb),
        grid=(bsz, n_hg, s // ts),
        in_specs=[pl.BlockSpec((depth, wide), lambda b, h, n: (0, h)),
                  pl.BlockSpec((1, HEAD_DIM), lambda b, h, n: (0, 0)),
                  blk(0), blk(1), blk(2), blk(3)],
        out_specs=pl.BlockSpec((N_BRANCHES, None, ts, wide), lambda b, h, n: (0, b, n, h)),
        out_shape=jax.ShapeDtypeStruct((N_BRANCHES, bsz, s, BRANCH_WIDTH), BF16),
        scratch_shapes=[pltpu.VMEM((hb, HEAD_DIM, HEAD_DIM), F32)]
                       + [pltpu.VMEM((hb, A_CHUNK, HEAD_DIM), F32)] * 2,
        compiler_params=_params("parallel", "parallel", "arbitrary"),
        name="hgrn2",
    )(lower_bounds, out_gain.reshape(1, HEAD_DIM), proj3, proj3, proj3, proj3)


def _rope_tables(positions):
    inv = jnp.power(jnp.float32(ROPE_THETA), -jnp.arange(0, ROT_DIM, 2, dtype=F32) / ROT_DIM)
    ang = positions.astype(F32)[..., None] * inv
    cos = jnp.cos(ang)
    sin = jnp.sin(ang)
    pad = jnp.zeros(ang.shape[:-1] + (HEAD_DIM - ROT_DIM,), F32)
    cos_f = jnp.concatenate([cos, cos, pad + 1.0], axis=-1)
    sin_f = jnp.concatenate([sin, sin, pad], axis=-1)
    return cos_f, sin_f


def _dilated_kernel(qg_ref, kg_ref, cos_ref, sin_ref, *rest):
    n_grp = len(B_PATTERNS)
    qkv = rest[:3 * n_grp]
    y_ref = rest[3 * n_grp + 1]
    scratch = list(rest[3 * n_grp + 2:])
    qn_s, kn_s, kc, vc, num_s, den_s, m_s = (scratch[i * n_grp:(i + 1) * n_grp] for i in range(7))
    s_s, p_s, vb_s = scratch[7 * n_grp:]
    n = pl.program_id(2)

    @pl.when(n == 0)
    def _():
        for g in range(n_grp):
            kc[g][...] = jnp.zeros_like(kc[g])
            vc[g][...] = jnp.zeros_like(vc[g])

    q_gain = qg_ref[...] * (HEAD_DIM ** -0.5 * math.log2(math.e))
    k_gain = kg_ref[...]

    src = lax.broadcasted_iota(jnp.int32, (HEAD_DIM, HEAD_DIM), 0)
    dst = lax.broadcasted_iota(jnp.int32, (HEAD_DIM, HEAD_DIM), 1)
    rot = jnp.where(jnp.logical_and(dst < ROT_HALF, src == dst + ROT_HALF), -1.0,
                    jnp.where(jnp.logical_and(jnp.logical_and(dst >= ROT_HALF, dst < ROT_DIM),
                                              src == dst - ROT_HALF), 1.0, 0.0)).astype(BF16)
    rot2 = jnp.concatenate([rot, rot], axis=0)

    def norm_rope(x, gain, cos, sin):
        ms = jnp.mean(x * x, axis=-1, keepdims=True)
        xn = x * lax.rsqrt(ms + EPS) * gain
        hi = xn.astype(BF16)
        lo = (xn - hi.astype(F32)).astype(BF16)
        return xn * cos + _dot(jnp.concatenate([hi, lo], axis=1), rot2) * sin

    def prep(c, carry):
        rows = pl.ds(pl.multiple_of(c * B_PREP_ROWS, B_PREP_ROWS), B_PREP_ROWS)
        cos, sin = cos_ref[rows, :], sin_ref[rows, :]
        for g in range(n_grp):
            q_ref, k_ref = qkv[3 * g], qkv[3 * g + 1]
            qn_s[g][rows, :] = norm_rope(q_ref[rows, :], q_gain[g:g + 1, :], cos, sin)
            kn_s[g][rows, :] = norm_rope(k_ref[rows, :], k_gain[g:g + 1, :], cos, sin)
        return carry

    lax.fori_loop(0, B_TILE // B_PREP_ROWS, prep, 0)

    qi = lax.broadcasted_iota(jnp.int32, (B_BLOCK, 2 * B_BLOCK), 0)
    ci = lax.broadcasted_iota(jnp.int32, (B_BLOCK, 2 * B_BLOCK), 1)
    cur_ok = jnp.logical_and(ci >= B_BLOCK, ci - B_BLOCK <= qi)
    prev_ok = jnp.logical_and(ci < B_BLOCK, ci >= qi)
    mask_inner = jnp.logical_or(cur_ok, prev_ok)
    mask_first = jnp.logical_or(cur_ok, jnp.logical_and(prev_ok, n > 0))
    ones = jnp.ones((2 * B_BLOCK, HEAD_DIM), BF16)

    for g, (win, dil) in enumerate(B_PATTERNS):
        v_ref = qkv[3 * g + 2]
        n_blk = B_TILE // (B_BLOCK * dil)
        blocks = [(r, j) for r in range(dil) for j in range(n_blk)]
        k_prev = v_prev = None
        for b0 in range(0, len(blocks), B_BATCH):
            batch = blocks[b0:b0 + B_BATCH]
            rows_of = []
            for i, (r, j) in enumerate(batch):
                start = j * B_BLOCK * dil + r
                rows = pl.ds(start, B_BLOCK) if dil == 1 else pl.ds(start, B_BLOCK, stride=dil)
                rows_of.append(rows)
                blk = slice(i * B_BLOCK, (i + 1) * B_BLOCK)
                if j == 0:
                    k_prev, v_prev = kc[g][r], vc[g][r]
                k_cur = kn_s[g][rows, :].astype(BF16)
                v_cur = v_ref[rows, :].astype(BF16)
                q = qn_s[g][rows, :].astype(BF16)
                s = _dot_nt(q, jnp.concatenate([k_prev, k_cur], axis=0))
                s_s[blk, :] = jnp.where(mask_first if j == 0 else mask_inner, s, NEG_BIG)
                vb_s[i, 0:B_BLOCK, :] = v_prev
                vb_s[i, B_BLOCK:2 * B_BLOCK, :] = v_cur
                if j == n_blk - 1:
                    kc[g][r] = k_cur
                    vc[g][r] = v_cur
                k_prev, v_prev = k_cur, v_cur
            live = slice(0, len(batch) * B_BLOCK)
            m = jnp.max(jnp.maximum(s_s[live, 0:B_BLOCK], s_s[live, B_BLOCK:2 * B_BLOCK]),
                        axis=-1, keepdims=True)
            p_s[live, :] = jnp.exp2(s_s[live, :] - m).astype(BF16)
            for i, rows in enumerate(rows_of):
                blk = slice(i * B_BLOCK, (i + 1) * B_BLOCK)
                pv = _dot(p_s[blk, :], jnp.concatenate([vb_s[i], ones], axis=1))
                num_s[g][rows, :] = pv[:, :HEAD_DIM]
                den_s[g][rows, :] = pv[:, HEAD_DIM:]
                m_s[g][rows, :] = jnp.broadcast_to(m[blk], (B_BLOCK, HEAD_DIM))

    def combine(c, carry):
        rows = pl.ds(pl.multiple_of(c * B_BLOCK, B_BLOCK), B_BLOCK)
        ms = [ref[rows, :] for ref in m_s]
        top = functools.reduce(jnp.maximum, ms)
        w = [jnp.exp2(x - top) for x in ms]
        num = functools.reduce(jnp.add, [wg * ref[rows, :] for wg, ref in zip(w, num_s)])
        den = functools.reduce(jnp.add, [wg * ref[rows, :] for wg, ref in zip(w, den_s)])
        y_ref[rows, :] = (num / den).astype(y_ref.dtype)
        return carry

    lax.fori_loop(0, B_TILE // B_BLOCK, combine, 0)


def dilated_mixer(proj3, tables, q_gain, k_gain, col0, ys, branch):
    bsz, s, _ = proj3.shape
    n_grp = len(B_PATTERNS)
    assert s % B_TILE == 0

    def part(g, p):
        base = (col0 + (3 * g + p) * BRANCH_WIDTH) // HEAD_DIM
        return pl.BlockSpec((None, B_TILE, HEAD_DIM), lambda b, h, n: (b, n, base + h))

    tab = pl.BlockSpec((None, B_TILE, HEAD_DIM), lambda b, h, n: (b, n, 0))
    gain = pl.BlockSpec((n_grp, HEAD_DIM), lambda b, h, n: (0, 0))
    for win, dil in B_PATTERNS:
        assert win // dil == B_BLOCK and B_TILE % (B_BLOCK * dil) == 0
    token_f32 = [pltpu.VMEM((B_TILE, HEAD_DIM), F32)] * n_grp
    carry = [pltpu.VMEM((dil, B_BLOCK, HEAD_DIM), BF16) for _, dil in B_PATTERNS]
    scratch = (token_f32 + token_f32 + carry + carry + token_f32 + token_f32 + token_f32
               + [pltpu.VMEM((B_BATCH * B_BLOCK, 2 * B_BLOCK), F32),
                  pltpu.VMEM((B_BATCH * B_BLOCK, 2 * B_BLOCK), BF16),
                  pltpu.VMEM((B_BATCH, 2 * B_BLOCK, HEAD_DIM), BF16)])
    return pl.pallas_call(
        _dilated_kernel,
        grid=(bsz, N_HEADS, s // B_TILE),
        in_specs=[gain, gain, tab, tab]
                 + [part(g, p) for g in range(n_grp) for p in range(3)]
                 + [pl.BlockSpec(memory_space=pl.ANY)],
        out_specs=pl.BlockSpec((None, None, B_TILE, HEAD_DIM), lambda b, h, n: (branch, b, n, h)),
        out_shape=jax.ShapeDtypeStruct(ys.shape, ys.dtype),
        input_output_aliases={4 + 3 * n_grp: 0},
        scratch_shapes=scratch,
        compiler_params=_params("parallel", "parallel", "arbitrary"),
        name="dilated_attn",
    )(q_gain, k_gain, *tables, *([proj3] * (3 * n_grp)), ys)


def _spatial_kernel(u_ref, v_ref, lng_ref, lnb_ref, w_ref, b_ref, ys_ref, o_ref):
    inv_sqrt2 = 1.0 / math.sqrt(2.0)

    def gelu(x):
        return 0.5 * x * (1.0 + lax.erf(x * inv_sqrt2))

    row = lax.broadcasted_iota(jnp.int32, (C_CHUNK, C_CHUNK), 0)
    col = lax.broadcasted_iota(jnp.int32, (C_CHUNK, C_CHUNK), 1)
    causal = row >= col
    bias = b_ref[...]
    w_causal = [jnp.where(causal, w_ref[g], 0.0).astype(BF16) for g in range(N_HEADS)]
    for c in range(u_ref.shape[0] // C_CHUNK):
        rows = slice(c * C_CHUNK, (c + 1) * C_CHUNK)
        v = gelu(v_ref[rows, :])
        mu = jnp.mean(v, axis=-1, keepdims=True)
        vc = v - mu
        var = jnp.mean(vc * vc, axis=-1, keepdims=True)
        vn = vc * lax.rsqrt(var + EPS) * lng_ref[...] + lnb_ref[...]
        for g in range(N_HEADS):
            gs = slice(g * HEAD_DIM, (g + 1) * HEAD_DIM)
            mixed = _dot(w_causal[g], vn[:, gs].astype(BF16)) + bias[:, g:g + 1]
            o_ref[rows, gs] = (gelu(u_ref[rows, gs]) * mixed).astype(o_ref.dtype)


def spatial_gating(proj, ln_g, ln_b, w_s, b_s, col0, ys, branch):
    m, _ = proj.shape
    ublk = col0 // BRANCH_WIDTH
    rows = C_STEP_ROWS
    return pl.pallas_call(
        _spatial_kernel,
        grid=(m // rows,),
        in_specs=[pl.BlockSpec((rows, BRANCH_WIDTH), lambda i: (i, ublk)),
                  pl.BlockSpec((rows, BRANCH_WIDTH), lambda i: (i, ublk + 1)),
                  pl.BlockSpec((1, BRANCH_WIDTH), lambda i: (0, 0)),
                  pl.BlockSpec((1, BRANCH_WIDTH), lambda i: (0, 0)),
                  pl.BlockSpec((N_HEADS, C_CHUNK, C_CHUNK), lambda i: (0, 0, 0)),
                  pl.BlockSpec((C_CHUNK, N_HEADS), lambda i: (0, 0)),
                  pl.BlockSpec(memory_space=pl.ANY)],
        out_specs=pl.BlockSpec((None, rows, BRANCH_WIDTH), lambda i: (branch, i, 0)),
        out_shape=jax.ShapeDtypeStruct(ys.shape, ys.dtype),
        input_output_aliases={6: 0},
        compiler_params=_params("parallel"),
        name="spatial_gating",
    )(proj, proj, ln_g.reshape(1, -1), ln_b.reshape(1, -1), w_s, b_s.T, ys)


def _merge_kernel(xn_ref, y_ref, wg_ref, wb_ref, o_ref, acc_ref, *, cast_sides):
    @pl.when(pl.program_id(2) == 0)
    def _():
        acc_ref[...] = jnp.zeros_like(acc_ref)

    cast_sides()
    gate = jax.nn.sigmoid(_dot(xn_ref[...], wg_ref[...]))
    total = acc_ref[...] + gate * _dot(y_ref[...], wb_ref[...])
    acc_ref[...] = total
    o_ref[...] = total.astype(o_ref.dtype)


def gated_merge(xn, ys, w_gate, w_branch, sides, *, tm, tn):
    m, d = xn.shape
    w = ys.shape[2]
    nblk = d // tn
    grid = (m // tm, nblk, N_BRANCHES)
    side = _side_casts(sides, grid)
    return pl.pallas_call(
        _with_side_casts(_merge_kernel, 4, 1, len(side)),
        grid=grid,
        in_specs=[pl.BlockSpec((tm, d), lambda a, b, i: (a, 0)),
                  pl.BlockSpec((None, tm, w), lambda a, b, i: (i, a, 0)),
                  pl.BlockSpec((d, tn), lambda a, b, i: (0, i * nblk + b)),
                  pl.BlockSpec((None, w, tn), lambda a, b, i: (i, 0, b))]
                 + [sc.in_spec for sc in side],
        out_specs=[pl.BlockSpec((tm, tn), lambda a, b, i: (a, b))] + [sc.out_spec for sc in side],
        out_shape=[jax.ShapeDtypeStruct((m, d), BF16)] + [sc.out_shape for sc in side],
        scratch_shapes=[pltpu.VMEM((tm, tn), F32)],
        compiler_params=_params("arbitrary", "arbitrary", "arbitrary"),
        name="gated_merge",
    )(xn, ys, w_gate, w_branch, *[sc.w for sc in side])


def _ffn_up_kernel(x_ref, wg_ref, wv_ref, cw_ref, cb_ref, o_ref, g_s, *, cast_sides, tiles_per_seq):
    i = pl.program_id(1)
    tm = x_ref.shape[0]

    @pl.when(i % tiles_per_seq == 0)
    def _():
        g_s[...] = jnp.zeros_like(g_s)

    cast_sides()
    x = x_ref[...]
    gate = _dot(x, wg_ref[...])
    val = _dot(x, wv_ref[...])
    prev = g_s[...]
    head = lax.broadcasted_iota(jnp.int32, prev.shape, 0)
    roll1 = pltpu.roll(gate, 1, 0)
    roll2 = pltpu.roll(gate, 2, 0)
    head1 = jnp.where(head == 0, prev[SUBLANES - 1:SUBLANES], roll1[:SUBLANES])
    head2 = jnp.where(head == 0, prev[SUBLANES - 2:SUBLANES - 1],
                      jnp.where(head == 1, prev[SUBLANES - 1:SUBLANES], roll2[:SUBLANES]))
    back1 = jnp.concatenate([head1, roll1[SUBLANES:]], axis=0)
    back2 = jnp.concatenate([head2, roll2[SUBLANES:]], axis=0)
    g_s[...] = gate[tm - SUBLANES:]
    cw = cw_ref[...]
    conv = cw[0:1] * back2 + cw[1:2] * back1 + cw[2:3] * gate + cb_ref[...]
    o_ref[...] = (conv * jax.nn.sigmoid(conv) * val).astype(o_ref.dtype)


def ffn_up(xn, w_up, conv_w, conv_b, seq_len, sides, *, tm, tn):
    m, d = xn.shape
    dff = conv_w.shape[1]
    nblk = dff // tn
    grid = (nblk, m // tm)
    side = _side_casts(sides, grid)
    body = functools.partial(_ffn_up_kernel, tiles_per_seq=seq_len // tm)
    return pl.pallas_call(
        _with_side_casts(body, 5, 1, len(side)),
        grid=grid,
        in_specs=[pl.BlockSpec((tm, d), lambda j, i: (i, 0)),
                  pl.BlockSpec((d, tn), lambda j, i: (0, j)),
                  pl.BlockSpec((d, tn), lambda j, i: (0, nblk + j)),
                  pl.BlockSpec((CONV_W, tn), lambda j, i: (0, j)),
                  pl.BlockSpec((1, tn), lambda j, i: (0, j))]
                 + [sc.in_spec for sc in side],
        out_specs=[pl.BlockSpec((tm, tn), lambda j, i: (i, j))] + [sc.out_spec for sc in side],
        out_shape=[jax.ShapeDtypeStruct((m, dff), BF16)] + [sc.out_shape for sc in side],
        scratch_shapes=[pltpu.VMEM((SUBLANES, tn), F32)],
        compiler_params=_params("arbitrary", "arbitrary"),
        name="ffn_up_conv",
    )(xn, w_up, w_up, conv_w, conv_b.reshape(1, dff), *[sc.w for sc in side])


def kernel(x, positions, norm_mix, w_in, hgrn_lower_bounds, hgrn_out_norm, q_norm, k_norm,
           sg_ln_g, sg_ln_b, sg_w, sg_b, w_gate, w_branch, w_out, norm_ffn, w_up,
           ffn_conv_w, ffn_conv_b, w_down):
    bsz, s, d = x.shape
    m = bsz * s
    depth = w_in.shape[0]
    in_cols = w_in.shape[2]
    a_cols = 4 * BRANCH_WIDTH
    b_cols = len(B_PATTERNS) * 3 * BRANCH_WIDTH
    tables = _rope_tables(positions)
    w_branch2d = w_branch.reshape(depth, N_BRANCHES * BRANCH_WIDTH, d)
    w_in_l = w_in[0].astype(BF16)
    w_branch_l = w_branch2d[0].astype(BF16)
    xf = x.reshape(m, d)
    for l in range(depth):
        xn = rmsnorm(xf, norm_mix[l])
        proj, w_gate_l = matmul(xn, w_in_l, None, [(w_gate, l, CAST_BLOCK_WIDE)], tm=MM_TM, tn=MM_TN)
        proj3 = proj.reshape(bsz, s, in_cols)
        ys = hgrn2_mixer(proj3, hgrn_lower_bounds, hgrn_out_norm[l], l)
        ys = dilated_mixer(proj3, tables, q_norm[l], k_norm[l], a_cols, ys, 1)
        ys = spatial_gating(proj, sg_ln_g[l], sg_ln_b[l], sg_w[l], sg_b[l], a_cols + b_cols,
                            ys.reshape(N_BRANCHES, m, BRANCH_WIDTH), 2)
        merged, w_up_l, w_out_l = gated_merge(
            xn, ys, w_gate_l, w_branch_l.reshape(N_BRANCHES, BRANCH_WIDTH, d),
            [(w_up, l, CAST_BLOCK_WIDE), (w_out, l, CAST_BLOCK)], tm=MM_TM, tn=MERGE_TN)
        xf = matmul_residual(merged, w_out_l, None, xf, tm=MM_TM, tn=MM_TN, tk=d)
        xn = rmsnorm(xf, norm_ffn[l])
        sides = [(w_down, l, CAST_BLOCK)]
        if l + 1 < depth:
            sides += [(w_in, l + 1, CAST_BLOCK_WIDE), (w_branch2d, l + 1, CAST_BLOCK)]
        h, w_down_l, *next_layer = ffn_up(xn, w_up_l, ffn_conv_w[l], ffn_conv_b[l], s, sides,
                                          tm=MM_TM, tn=FFN_TN)
        if next_layer:
            w_in_l, w_branch_l = next_layer
        xf = matmul_residual(h, w_down_l, None, xf, tm=MM_TM, tn=MM_TN, tk=d)
    return xf.reshape(bsz, s, d)
```

```python
import functools
import math

import jax
import jax.numpy as jnp
from jax import lax
from jax.experimental import pallas as pl
from jax.experimental.pallas import tpu as pltpu

F32 = jnp.float32
BF16 = jnp.bfloat16

EPS = 1e-6
NEG_BIG = -1e30

LANES = 128
SUBLANES = 8
VMEM_LIMIT = 56 * 1024 * 1024

HEAD_DIM = 128
N_HEADS = 8
BRANCH_WIDTH = N_HEADS * HEAD_DIM
N_BRANCHES = 3

A_CHUNK = 64
A_SUB = 16
A_STEP_ROWS = 512
A_STEP_HEADS = 8
A_UNROLL = 2

B_PATTERNS = ((128, 1), (512, 4), (2048, 16))
B_BLOCK = 128
B_TILE = 2048
B_PREP_ROWS = 2048
B_BATCH = 8
ROPE_THETA = 500000.0
ROT_DIM = HEAD_DIM // 4
ROT_HALF = ROT_DIM // 2

C_CHUNK = 128
C_STEP_ROWS = 512
CONV_W = 3

NORM_ROWS = 512
MM_TM = 1024
MM_TN = 1024
MERGE_TN = 512
FFN_TN = 512
CAST_BLOCK = (512, 512)
CAST_BLOCK_WIDE = (512, 1024)


def _params(*sem):
    return pltpu.CompilerParams(dimension_semantics=sem, vmem_limit_bytes=VMEM_LIMIT)


def _dot(a, b):
    return jnp.dot(a, b, preferred_element_type=F32)


def _dot_nt(a, b):
    return lax.dot_general(a, b, (((1,), (1,)), ((), ())), preferred_element_type=F32)


def _dot_tn(a, b):
    return lax.dot_general(a, b, (((0,), (0,)), ((), ())), preferred_element_type=F32)


def _rmsnorm_kernel(x_ref, g_ref, o_ref):
    x = x_ref[...]
    ms = jnp.mean(x * x, axis=-1, keepdims=True)
    o_ref[...] = (x * lax.rsqrt(ms + EPS) * g_ref[...]).astype(o_ref.dtype)


def rmsnorm(x, g, *, rows=NORM_ROWS):
    m, d = x.shape
    return pl.pallas_call(
        _rmsnorm_kernel,
        grid=(m // rows,),
        in_specs=[pl.BlockSpec((rows, d), lambda i: (i, 0)),
                  pl.BlockSpec((1, d), lambda i: (0, 0))],
        out_specs=pl.BlockSpec((rows, d), lambda i: (i, 0)),
        out_shape=jax.ShapeDtypeStruct((m, d), BF16),
        compiler_params=_params("parallel"),
        name="rmsnorm",
    )(x, g.reshape(1, d))


def _weight_spec(w, layer, block, index):
    if layer is None:
        return pl.BlockSpec(block, index)
    return pl.BlockSpec((None,) + block, lambda *ids: (layer,) + index(*ids))


class SideCast:
    def __init__(self, w, layer, block, grid):
        _, r, c = w.shape
        br, bc = block
        n_col = c // bc
        n_blocks = (r // br) * n_col
        assert r % br == 0 and c % bc == 0 and n_blocks <= math.prod(grid)

        def block_index(*ids):
            t = ids[0]
            for extent, i in zip(grid[1:], ids[1:]):
                t = t * extent + i
            t = jnp.minimum(t, n_blocks - 1)
            return t // n_col, t % n_col

        self.w = w
        self.in_spec = pl.BlockSpec((None, br, bc), lambda *ids: (layer,) + block_index(*ids))
        self.out_spec = pl.BlockSpec((br, bc), block_index)
        self.out_shape = jax.ShapeDtypeStruct((r, c), BF16)


def _side_casts(sides, grid):
    return [SideCast(w, layer, block, grid) for w, layer, block in sides]


def _with_side_casts(body, n_in, n_out, n_side):
    def wrapped(*refs):
        ins, rest = refs[:n_in], refs[n_in:]
        side_in, rest = rest[:n_side], rest[n_side:]
        outs, rest = rest[:n_out], rest[n_out:]
        side_out, scratch = rest[:n_side], rest[n_side:]

        def cast_sides():
            for src, dst in zip(side_in, side_out):
                dst[...] = src[...].astype(dst.dtype)

        body(*ins, *outs, *scratch, cast_sides=cast_sides)

    return wrapped


def _matmul_kernel(x_ref, w_ref, o_ref, *, cast_sides):
    cast_sides()
    o_ref[...] = _dot(x_ref[...], w_ref[...])


def matmul(x, w, layer, sides, *, tm, tn):
    m, k = x.shape
    n = w.shape[-1]
    grid = (m // tm, n // tn)
    side = _side_casts(sides, grid)
    return pl.pallas_call(
        _with_side_casts(_matmul_kernel, 2, 1, len(side)),
        grid=grid,
        in_specs=[pl.BlockSpec((tm, k), lambda i, j: (i, 0)),
                  _weight_spec(w, layer, (k, tn), lambda i, j: (0, j))]
                 + [sc.in_spec for sc in side],
        out_specs=[pl.BlockSpec((tm, tn), lambda i, j: (i, j))] + [sc.out_spec for sc in side],
        out_shape=[jax.ShapeDtypeStruct((m, n), F32)] + [sc.out_shape for sc in side],
        compiler_params=_params("arbitrary", "arbitrary"),
        name="proj_matmul",
    )(x, w, *[sc.w for sc in side])


def _matmul_res_kernel(x_ref, w_ref, r_ref, o_ref):
    k = pl.program_id(2)

    @pl.when(k == 0)
    def _():
        o_ref[...] = r_ref[...] + _dot(x_ref[...], w_ref[...])

    @pl.when(k > 0)
    def _():
        o_ref[...] += _dot(x_ref[...], w_ref[...])


def matmul_residual(x, w, layer, res, *, tm, tn, tk):
    m, k = x.shape
    n = w.shape[-1]
    return pl.pallas_call(
        _matmul_res_kernel,
        grid=(m // tm, n // tn, k // tk),
        in_specs=[pl.BlockSpec((tm, tk), lambda i, j, kk: (i, kk)),
                  _weight_spec(w, layer, (tk, tn), lambda i, j, kk: (kk, j)),
                  pl.BlockSpec((tm, tn), lambda i, j, kk: (i, j))],
        out_specs=pl.BlockSpec((tm, tn), lambda i, j, kk: (i, j)),
        out_shape=jax.ShapeDtypeStruct((m, n), F32),
        compiler_params=_params("parallel", "parallel", "arbitrary"),
        name="matmul_residual",
    )(x, w, res)


def _hgrn_kernel(lbp_ref, gain_ref, q_ref, f_ref, i_ref, g_ref, o_ref, st_ref, ck_s, v_s,
                 *, layer, n_chunks, n_heads):
    n = pl.program_id(2)

    @pl.when(n == 0)
    def _():
        st_ref[...] = jnp.zeros_like(st_ref)

    for other in range(1, N_BRANCHES):
        o_ref[other] = jnp.zeros(o_ref.shape[1:], o_ref.dtype)

    lbp = lbp_ref[...]
    e = jnp.exp(lbp - jnp.max(lbp, axis=0, keepdims=True))
    sm = e / jnp.sum(e, axis=0, keepdims=True)
    cs = sm[0:1]
    for li in range(1, layer + 1):
        cs = cs + sm[li:li + 1]
    lb_all = cs - sm[0:1]

    gain = gain_ref[...]
    n_sub = A_CHUNK // A_SUB
    row = lax.broadcasted_iota(jnp.int32, (A_CHUNK, A_CHUNK), 0)
    col = lax.broadcasted_iota(jnp.int32, (A_CHUNK, A_CHUNK), 1)
    tri = (row >= col).astype(F32)
    off_mask = (row // A_SUB) > (col // A_SUB)
    rblk = lax.broadcasted_iota(jnp.int32, (A_CHUNK, HEAD_DIM), 0) // A_SUB
    half_row = lax.broadcasted_iota(jnp.int32, (SUBLANES, HEAD_DIM), 0)
    scale = HEAD_DIM ** -0.5

    def head_chunk(hh, r0):
        sl = pl.ds(r0, A_CHUNK)
        hs = slice(hh * HEAD_DIM, (hh + 1) * HEAD_DIM)
        lb = lb_all[:, hs]
        q = q_ref[sl, hs] * scale
        v = i_ref[sl, hs]
        f = lb + (1.0 - lb) * jax.nn.sigmoid(f_ref[sl, hs])
        kk = 1.0 - f
        cum = jnp.dot(tri, jnp.log2(f), precision=lax.Precision.HIGHEST,
                      preferred_element_type=F32)
        ends = [cum[(j + 1) * A_SUB - 1:(j + 1) * A_SUB, :] for j in range(n_sub)]
        last = ends[-1]
        eblk = jnp.concatenate([jnp.broadcast_to(ej, (A_SUB, HEAD_DIM)) for ej in ends], axis=0)
        khat = kk * jnp.exp2(eblk - cum)
        st = st_ref[hh]

        o = _dot_nt((q * jnp.exp2(cum)).astype(BF16), st.astype(BF16))

        s_off = jnp.zeros((A_CHUNK, A_CHUNK), F32)
        for j in range(n_sub - 1):
            qj = q * jnp.exp2(jnp.minimum(cum - ends[j], 0.0))
            kj = jnp.where(rblk == j, khat, 0.0)
            s_off = s_off + _dot_nt(qj.astype(BF16), kj.astype(BF16))
        s_off = jnp.where(off_mask, s_off, 0.0)
        o = o + _dot(s_off.astype(BF16), v.astype(BF16))

        ck_s[hh] = cum - jnp.log2(kk)
        v_s[hh] = v
        diag = []
        for b in range(n_sub):
            base = b * A_SUB
            q_lo, q_hi = q[base:base + SUBLANES], q[base + SUBLANES:base + A_SUB]
            c_lo, c_hi = cum[base:base + SUBLANES], cum[base + SUBLANES:base + A_SUB]
            o_lo = jnp.zeros((SUBLANES, HEAD_DIM), F32)
            o_hi = jnp.zeros((SUBLANES, HEAD_DIM), F32)
            for s in range(A_SUB):
                r = base + s
                c_row = ck_s[hh, r:r + 1, :]
                v_row = v_s[hh, r:r + 1, :]
                if s < SUBLANES:
                    dec = jnp.exp2(jnp.where(half_row >= s, c_lo - c_row, NEG_BIG))
                    o_lo = o_lo + jnp.sum(q_lo * dec, axis=-1, keepdims=True) * v_row
                    dec = jnp.exp2(c_hi - c_row)
                else:
                    dec = jnp.exp2(jnp.where(half_row >= s - SUBLANES, c_hi - c_row, NEG_BIG))
                o_hi = o_hi + jnp.sum(q_hi * dec, axis=-1, keepdims=True) * v_row
            diag += [o_lo, o_hi]
        o = o + jnp.concatenate(diag, axis=0)

        ktil = khat * jnp.exp2(last - eblk)
        st_ref[hh] = st * jnp.exp2(last) + _dot_tn(v.astype(BF16), ktil.astype(BF16))

        ms = jnp.mean(o * o, axis=-1, keepdims=True)
        on = o * lax.rsqrt(ms + EPS) * gain
        g = g_ref[sl, hs]
        o_ref[0, sl, hs] = (on * (g * jax.nn.sigmoid(g))).astype(o_ref.dtype)

    def chunk(c, carry):
        r0 = pl.multiple_of(c * A_CHUNK, A_CHUNK)
        for hh in range(n_heads):
            head_chunk(hh, r0)
        return carry

    lax.fori_loop(0, n_chunks, chunk, 0, unroll=A_UNROLL)


def hgrn2_mixer(proj3, lower_bounds, out_gain, layer):
    bsz, s, _ = proj3.shape
    depth = lower_bounds.shape[0]
    ts = min(A_STEP_ROWS, s)
    hb = A_STEP_HEADS
    n_hg = N_HEADS // hb
    wide = hb * HEAD_DIM
    blk = lambda part: pl.BlockSpec((None, ts, wide), lambda b, h, n: (b, n, part * n_hg + h))
    return pl.pallas_call(
        functools.partial(_hgrn_kernel, layer=layer, n_chunks=ts // A_CHUNK, n_heads=hb),
        grid=(bsz, n_hg, s // ts),
        in_specs=[pl.BlockSpec((depth, wide), lambda b, h, n: (0, h)),
                  pl.BlockSpec((1, HEAD_DIM), lambda b, h, n: (0, 0)),
                  blk(0), blk(1), blk(2), blk(3)],
        out_specs=pl.BlockSpec((N_BRANCHES, None, ts, wide), lambda b, h, n: (0, b, n, h)),
        out_shape=jax.ShapeDtypeStruct((N_BRANCHES, bsz, s, BRANCH_WIDTH), BF16),
        scratch_shapes=[pltpu.VMEM((hb, HEAD_DIM, HEAD_DIM), F32)]
                       + [pltpu.VMEM((hb, A_CHUNK, HEAD_DIM), F32)] * 2,
        compiler_params=_params("parallel", "parallel", "arbitrary"),
        name="hgrn2",
    )(lower_bounds, out_gain.reshape(1, HEAD_DIM), proj3, proj3, proj3, proj3)


def _rope_tables(positions):
    inv = jnp.power(jnp.float32(ROPE_THETA), -jnp.arange(0, ROT_DIM, 2, dtype=F32) / ROT_DIM)
    ang = positions.astype(F32)[..., None] * inv
    cos = jnp.cos(ang)
    sin = jnp.sin(ang)
    pad = jnp.zeros(ang.shape[:-1] + (HEAD_DIM - ROT_DIM,), F32)
    cos_f = jnp.concatenate([cos, cos, pad + 1.0], axis=-1)
    sin_f = jnp.concatenate([sin, sin, pad], axis=-1)
    return cos_f, sin_f


def _dilated_kernel(qg_ref, kg_ref, cos_ref, sin_ref, *rest):
    n_grp = len(B_PATTERNS)
    qkv = rest[:3 * n_grp]
    y_ref = rest[3 * n_grp + 1]
    scratch = list(rest[3 * n_grp + 2:])
    qn_s, kn_s, kc, vc, num_s, den_s, m_s = (scratch[i * n_grp:(i + 1) * n_grp] for i in range(7))
    s_s, p_s, vb_s = scratch[7 * n_grp:]
    n = pl.program_id(2)

    @pl.when(n == 0)
    def _():
        for g in range(n_grp):
            kc[g][...] = jnp.zeros_like(kc[g])
            vc[g][...] = jnp.zeros_like(vc[g])

    q_gain = qg_ref[...] * (HEAD_DIM ** -0.5 * math.log2(math.e))
    k_gain = kg_ref[...]

    src = lax.broadcasted_iota(jnp.int32, (HEAD_DIM, HEAD_DIM), 0)
    dst = lax.broadcasted_iota(jnp.int32, (HEAD_DIM, HEAD_DIM), 1)
    rot = jnp.where(jnp.logical_and(dst < ROT_HALF, src == dst + ROT_HALF), -1.0,
                    jnp.where(jnp.logical_and(jnp.logical_and(dst >= ROT_HALF, dst < ROT_DIM),
                                              src == dst - ROT_HALF), 1.0, 0.0)).astype(BF16)
    rot2 = jnp.concatenate([rot, rot], axis=0)

    def norm_rope(x, gain, cos, sin):
        ms = jnp.mean(x * x, axis=-1, keepdims=True)
        xn = x * lax.rsqrt(ms + EPS) * gain
        hi = xn.astype(BF16)
        lo = (xn - hi.astype(F32)).astype(BF16)
        return xn * cos + _dot(jnp.concatenate([hi, lo], axis=1), rot2) * sin

    def prep(c, carry):
        rows = pl.ds(pl.multiple_of(c * B_PREP_ROWS, B_PREP_ROWS), B_PREP_ROWS)
        cos, sin = cos_ref[rows, :], sin_ref[rows, :]
        for g in range(n_grp):
            q_ref, k_ref = qkv[3 * g], qkv[3 * g + 1]
            qn_s[g][rows, :] = norm_rope(q_ref[rows, :], q_gain[g:g + 1, :], cos, sin)
            kn_s[g][rows, :] = norm_rope(k_ref[rows, :], k_gain[g:g + 1, :], cos, sin)
        return carry

    lax.fori_loop(0, B_TILE // B_PREP_ROWS, prep, 0)

    qi = lax.broadcasted_iota(jnp.int32, (B_BLOCK, 2 * B_BLOCK), 0)
    ci = lax.broadcasted_iota(jnp.int32, (B_BLOCK, 2 * B_BLOCK), 1)
    cur_ok = jnp.logical_and(ci >= B_BLOCK, ci - B_BLOCK <= qi)
    prev_ok = jnp.logical_and(ci < B_BLOCK, ci >= qi)
    mask_inner = jnp.logical_or(cur_ok, prev_ok)
    mask_first = jnp.logical_or(cur_ok, jnp.logical_and(prev_ok, n > 0))
    ones = jnp.ones((2 * B_BLOCK, HEAD_DIM), BF16)

    for g, (win, dil) in enumerate(B_PATTERNS):
        v_ref = qkv[3 * g + 2]
        n_blk = B_TILE // (B_BLOCK * dil)
        blocks = [(r, j) for r in range(dil) for j in range(n_blk)]
        k_prev = v_prev = None
        for b0 in range(0, len(blocks), B_BATCH):
            batch = blocks[b0:b0 + B_BATCH]
            rows_of = []
            for i, (r, j) in enumerate(batch):
                start = j * B_BLOCK * dil + r
                rows = pl.ds(start, B_BLOCK) if dil == 1 else pl.ds(start, B_BLOCK, stride=dil)
                rows_of.append(rows)
                blk = slice(i * B_BLOCK, (i + 1) * B_BLOCK)
                if j == 0:
                    k_prev, v_prev = kc[g][r], vc[g][r]
                k_cur = kn_s[g][rows, :].astype(BF16)
                v_cur = v_ref[rows, :].astype(BF16)
                q = qn_s[g][rows, :].astype(BF16)
                s = _dot_nt(q, jnp.concatenate([k_prev, k_cur], axis=0))
                s_s[blk, :] = jnp.where(mask_first if j == 0 else mask_inner, s, NEG_BIG)
                vb_s[i, 0:B_BLOCK, :] = v_prev
                vb_s[i, B_BLOCK:2 * B_BLOCK, :] = v_cur
                if j == n_blk - 1:
                    kc[g][r] = k_cur
                    vc[g][r] = v_cur
                k_prev, v_prev = k_cur, v_cur
            live = slice(0, len(batch) * B_BLOCK)
            m = jnp.max(jnp.maximum(s_s[live, 0:B_BLOCK], s_s[live, B_BLOCK:2 * B_BLOCK]),
                        axis=-1, keepdims=True)
            p_s[live, :] = jnp.exp2(s_s[live, :] - m).astype(BF16)
            for i, rows in enumerate(rows_of):
                blk = slice(i * B_BLOCK, (i + 1) * B_BLOCK)
                pv = _dot(p_s[blk, :], jnp.concatenate([vb_s[i], ones], axis=1))
                num_s[g][rows, :] = pv[:, :HEAD_DIM]
                den_s[g][rows, :] = pv[:, HEAD_DIM:]
                m_s[g][rows, :] = jnp.broadcast_to(m[blk], (B_BLOCK, HEAD_DIM))

    def combine(c, carry):
        rows = pl.ds(pl.multiple_of(c * B_BLOCK, B_BLOCK), B_BLOCK)
        ms = [ref[rows, :] for ref in m_s]
        top = functools.reduce(jnp.maximum, ms)
        w = [jnp.exp2(x - top) for x in ms]
        num = functools.reduce(jnp.add, [wg * ref[rows, :] for wg, ref in zip(w, num_s)])
        den = functools.reduce(jnp.add, [wg * ref[rows, :] for wg, ref in zip(w, den_s)])
        y_ref[rows, :] = (num / den).astype(y_ref.dtype)
        return carry

    lax.fori_loop(0, B_TILE // B_BLOCK, combine, 0)


def dilated_mixer(proj3, tables, q_gain, k_gain, col0, ys, branch):
    bsz, s, _ = proj3.shape
    n_grp = len(B_PATTERNS)
    assert s % B_TILE == 0

    def part(g, p):
        base = (col0 + (3 * g + p) * BRANCH_WIDTH) // HEAD_DIM
        return pl.BlockSpec((None, B_TILE, HEAD_DIM), lambda b, h, n: (b, n, base + h))

    tab = pl.BlockSpec((None, B_TILE, HEAD_DIM), lambda b, h, n: (b, n, 0))
    gain = pl.BlockSpec((n_grp, HEAD_DIM), lambda b, h, n: (0, 0))
    for win, dil in B_PATTERNS:
        assert win // dil == B_BLOCK and B_TILE % (B_BLOCK * dil) == 0
    token_f32 = [pltpu.VMEM((B_TILE, HEAD_DIM), F32)] * n_grp
    carry = [pltpu.VMEM((dil, B_BLOCK, HEAD_DIM), BF16) for _, dil in B_PATTERNS]
    scratch = (token_f32 + token_f32 + carry + carry + token_f32 + token_f32 + token_f32
               + [pltpu.VMEM((B_BATCH * B_BLOCK, 2 * B_BLOCK), F32),
                  pltpu.VMEM((B_BATCH * B_BLOCK, 2 * B_BLOCK), BF16),
                  pltpu.VMEM((B_BATCH, 2 * B_BLOCK, HEAD_DIM), BF16)])
    return pl.pallas_call(
        _dilated_kernel,
        grid=(bsz, N_HEADS, s // B_TILE),
        in_specs=[gain, gain, tab, tab]
                 + [part(g, p) for g in range(n_grp) for p in range(3)]
                 + [pl.BlockSpec(memory_space=pl.ANY)],
        out_specs=pl.BlockSpec((None, None, B_TILE, HEAD_DIM), lambda b, h, n: (branch, b, n, h)),
        out_shape=jax.ShapeDtypeStruct(ys.shape, ys.dtype),
        input_output_aliases={4 + 3 * n_grp: 0},
        scratch_shapes=scratch,
        compiler_params=_params("parallel", "parallel", "arbitrary"),
        name="dilated_attn",
    )(q_gain, k_gain, *tables, *([proj3] * (3 * n_grp)), ys)


def _spatial_kernel(u_ref, v_ref, lng_ref, lnb_ref, w_ref, b_ref, ys_ref, o_ref):
    inv_sqrt2 = 1.0 / math.sqrt(2.0)

    def gelu(x):
        return 0.5 * x * (1.0 + lax.erf(x * inv_sqrt2))

    row = lax.broadcasted_iota(jnp.int32, (C_CHUNK, C_CHUNK), 0)
    col = lax.broadcasted_iota(jnp.int32, (C_CHUNK, C_CHUNK), 1)
    causal = row >= col
    bias = b_ref[...]
    w_causal = [jnp.where(causal, w_ref[g], 0.0).astype(BF16) for g in range(N_HEADS)]
    for c in range(u_ref.shape[0] // C_CHUNK):
        rows = slice(c * C_CHUNK, (c + 1) * C_CHUNK)
        v = gelu(v_ref[rows, :])
        mu = jnp.mean(v, axis=-1, keepdims=True)
        vc = v - mu
        var = jnp.mean(vc * vc, axis=-1, keepdims=True)
        vn = vc * lax.rsqrt(var + EPS) * lng_ref[...] + lnb_ref[...]
        for g in range(N_HEADS):
            gs = slice(g * HEAD_DIM, (g + 1) * HEAD_DIM)
            mixed = _dot(w_causal[g], vn[:, gs].astype(BF16)) + bias[:, g:g + 1]
            o_ref[rows, gs] = (gelu(u_ref[rows, gs]) * mixed).astype(o_ref.dtype)


def spatial_gating(proj, ln_g, ln_b, w_s, b_s, col0, ys, branch):
    m, _ = proj.shape
    ublk = col0 // BRANCH_WIDTH
    rows = C_STEP_ROWS
    return pl.pallas_call(
        _spatial_kernel,
        grid=(m // rows,),
        in_specs=[pl.BlockSpec((rows, BRANCH_WIDTH), lambda i: (i, ublk)),
                  pl.BlockSpec((rows, BRANCH_WIDTH), lambda i: (i, ublk + 1)),
                  pl.BlockSpec((1, BRANCH_WIDTH), lambda i: (0, 0)),
                  pl.BlockSpec((1, BRANCH_WIDTH), lambda i: (0, 0)),
                  pl.BlockSpec((N_HEADS, C_CHUNK, C_CHUNK), lambda i: (0, 0, 0)),
                  pl.BlockSpec((C_CHUNK, N_HEADS), lambda i: (0, 0)),
                  pl.BlockSpec(memory_space=pl.ANY)],
        out_specs=pl.BlockSpec((None, rows, BRANCH_WIDTH), lambda i: (branch, i, 0)),
        out_shape=jax.ShapeDtypeStruct(ys.shape, ys.dtype),
        input_output_aliases={6: 0},
        compiler_params=_params("parallel"),
        name="spatial_gating",
    )(proj, proj, ln_g.reshape(1, -1), ln_b.reshape(1, -1), w_s, b_s.T, ys)


def _merge_kernel(xn_ref, y_ref, wg_ref, wb_ref, o_ref, acc_ref, *, cast_sides):
    @pl.when(pl.program_id(2) == 0)
    def _():
        acc_ref[...] = jnp.zeros_like(acc_ref)

    cast_sides()
    gate = jax.nn.sigmoid(_dot(xn_ref[...], wg_ref[...]))
    total = acc_ref[...] + gate * _dot(y_ref[...], wb_ref[...])
    acc_ref[...] = total
    o_ref[...] = total.astype(o_ref.dtype)


def gated_merge(xn, ys, w_gate, w_branch, sides, *, tm, tn):
    m, d = xn.shape
    w = ys.shape[2]
    nblk = d // tn
    grid = (m // tm, nblk, N_BRANCHES)
    side = _side_casts(sides, grid)
    return pl.pallas_call(
        _with_side_casts(_merge_kernel, 4, 1, len(side)),
        grid=grid,
        in_specs=[pl.BlockSpec((tm, d), lambda a, b, i: (a, 0)),
                  pl.BlockSpec((None, tm, w), lambda a, b, i: (i, a, 0)),
                  pl.BlockSpec((d, tn), lambda a, b, i: (0, i * nblk + b)),
                  pl.BlockSpec((None, w, tn), lambda a, b, i: (i, 0, b))]
                 + [sc.in_spec for sc in side],
        out_specs=[pl.BlockSpec((tm, tn), lambda a, b, i: (a, b))] + [sc.out_spec for sc in side],
        out_shape=[jax.ShapeDtypeStruct((m, d), BF16)] + [sc.out_shape for sc in side],
        scratch_shapes=[pltpu.VMEM((tm, tn), F32)],
        compiler_params=_params("arbitrary", "arbitrary", "arbitrary"),
        name="gated_merge",
    )(xn, ys, w_gate, w_branch, *[sc.w for sc in side])


def _ffn_up_kernel(x_ref, wg_ref, wv_ref, cw_ref, cb_ref, o_ref, g_s, *, cast_sides, tiles_per_seq):
    i = pl.program_id(1)
    tm = x_ref.shape[0]

    @pl.when(i % tiles_per_seq == 0)
    def _():
        g_s[...] = jnp.zeros_like(g_s)

    cast_sides()
    x = x_ref[...]
    gate = _dot(x, wg_ref[...])
    val = _dot(x, wv_ref[...])
    prev = g_s[...]
    head = lax.broadcasted_iota(jnp.int32, prev.shape, 0)
    roll1 = pltpu.roll(gate, 1, 0)
    roll2 = pltpu.roll(gate, 2, 0)
    head1 = jnp.where(head == 0, prev[SUBLANES - 1:SUBLANES], roll1[:SUBLANES])
    head2 = jnp.where(head == 0, prev[SUBLANES - 2:SUBLANES - 1],
                      jnp.where(head == 1, prev[SUBLANES - 1:SUBLANES], roll2[:SUBLANES]))
    back1 = jnp.concatenate([head1, roll1[SUBLANES:]], axis=0)
    back2 = jnp.concatenate([head2, roll2[SUBLANES:]], axis=0)
    g_s[...] = gate[tm - SUBLANES:]
    cw = cw_ref[...]
    conv = cw[0:1] * back2 + cw[1:2] * back1 + cw[2:3] * gate + cb_ref[...]
    o_ref[...] = (conv * jax.nn.sigmoid(conv) * val).astype(o_ref.dtype)


def ffn_up(xn, w_up, conv_w, conv_b, seq_len, sides, *, tm, tn):
    m, d = xn.shape
    dff = conv_w.shape[1]
    nblk = dff // tn
    grid = (nblk, m // tm)
    side = _side_casts(sides, grid)
    body = functools.partial(_ffn_up_kernel, tiles_per_seq=seq_len // tm)
    return pl.pallas_call(
        _with_side_casts(body, 5, 1, len(side)),
        grid=grid,
        in_specs=[pl.BlockSpec((tm, d), lambda j, i: (i, 0)),
                  pl.BlockSpec((d, tn), lambda j, i: (0, j)),
                  pl.BlockSpec((d, tn), lambda j, i: (0, nblk + j)),
                  pl.BlockSpec((CONV_W, tn), lambda j, i: (0, j)),
                  pl.BlockSpec((1, tn), lambda j, i: (0, j))]
                 + [sc.in_spec for sc in side],
        out_specs=[pl.BlockSpec((tm, tn), lambda j, i: (i, j))] + [sc.out_spec for sc in side],
        out_shape=[jax.ShapeDtypeStruct((m, dff), BF16)] + [sc.out_shape for sc in side],
        scratch_shapes=[pltpu.VMEM((SUBLANES, tn), F32)],
        compiler_params=_params("arbitrary", "arbitrary"),
        name="ffn_up_conv",
    )(xn, w_up, w_up, conv_w, conv_b.reshape(1, dff), *[sc.w for sc in side])


def kernel(x, positions, norm_mix, w_in, hgrn_lower_bounds, hgrn_out_norm, q_norm, k_norm,
           sg_ln_g, sg_ln_b, sg_w, sg_b, w_gate, w_branch, w_out, norm_ffn, w_up,
           ffn_conv_w, ffn_conv_b, w_down):
    bsz, s, d = x.shape
    m = bsz * s
    depth = w_in.shape[0]
    in_cols = w_in.shape[2]
    a_cols = 4 * BRANCH_WIDTH
    b_cols = len(B_PATTERNS) * 3 * BRANCH_WIDTH
    tables = _rope_tables(positions)
    w_branch2d = w_branch.reshape(depth, N_BRANCHES * BRANCH_WIDTH, d)
    w_in_l = w_in[0].astype(BF16)
    w_branch_l = w_branch2d[0].astype(BF16)
    xf = x.reshape(m, d)
    for l in range(depth):
        xn = rmsnorm(xf, norm_mix[l])
        proj, w_gate_l = matmul(xn, w_in_l, None, [(w_gate, l, CAST_BLOCK_WIDE)], tm=MM_TM, tn=MM_TN)
        proj3 = proj.reshape(bsz, s, in_cols)
        ys = hgrn2_mixer(proj3, hgrn_lower_bounds, hgrn_out_norm[l], l)
        ys = dilated_mixer(proj3, tables, q_norm[l], k_norm[l], a_cols, ys, 1)
        ys = spatial_gating(proj, sg_ln_g[l], sg_ln_b[l], sg_w[l], sg_b[l], a_cols + b_cols,
                            ys.reshape(N_BRANCHES, m, BRANCH_WIDTH), 2)
        merged, w_up_l, w_out_l = gated_merge(
            xn, ys, w_gate_l, w_branch_l.reshape(N_BRANCHES, BRANCH_WIDTH, d),
            [(w_up, l, CAST_BLOCK_WIDE), (w_out, l, CAST_BLOCK)], tm=MM_TM, tn=MERGE_TN)
        xf = matmul_residual(merged, w_out_l, None, xf, tm=MM_TM, tn=MM_TN, tk=d)
        xn = rmsnorm(xf, norm_ffn[l])
        sides = [(w_down, l, CAST_BLOCK)]
        if l + 1 < depth:
            sides += [(w_in, l + 1, CAST_BLOCK_WIDE), (w_branch2d, l + 1, CAST_BLOCK)]
        h, w_down_l, *next_layer = ffn_up(xn, w_up_l, ffn_conv_w[l], ffn_conv_b[l], s, sides,
                                          tm=MM_TM, tn=FFN_TN)
        if next_layer:
            w_in_l, w_branch_l = next_layer
        xf = matmul_residual(h, w_down_l, None, xf, tm=MM_TM, tn=MM_TN, tk=d)
    return xf.reshape(bsz, s, d)
```

```python
import functools
import math

import jax
import jax.numpy as jnp
from jax import lax
from jax.experimental import pallas as pl
from jax.experimental.pallas import tpu as pltpu

F32 = jnp.float32
BF16 = jnp.bfloat16

EPS = 1e-6
NEG_BIG = -1e30

LANES = 128
SUBLANES = 8
VMEM_LIMIT = 56 * 1024 * 1024

HEAD_DIM = 128
N_HEADS = 8
BRANCH_WIDTH = N_HEADS * HEAD_DIM
N_BRANCHES = 3

A_CHUNK = 64
A_SUB = 16
A_STEP_ROWS = 512
A_STEP_HEADS = 8
A_UNROLL = 2

B_PATTERNS = ((128, 1), (512, 4), (2048, 16))
B_BLOCK = 128
B_TILE = 2048
B_BATCH = 8
ROPE_THETA = 500000.0
ROT_DIM = HEAD_DIM // 4
ROT_HALF = ROT_DIM // 2

C_CHUNK = 128
C_STEP_ROWS = 512
CONV_W = 3

NORM_ROWS = 512
MM_TM = 1024
MM_TN = 1024
MERGE_TN = 512
FFN_TN = 512
CAST_BLOCK = (512, 512)
CAST_BLOCK_WIDE = (512, 1024)


def _params(*sem):
    return pltpu.CompilerParams(dimension_semantics=sem, vmem_limit_bytes=VMEM_LIMIT)


def _dot(a, b):
    return jnp.dot(a, b, preferred_element_type=F32)


def _dot_nt(a, b):
    return lax.dot_general(a, b, (((1,), (1,)), ((), ())), preferred_element_type=F32)


def _dot_tn(a, b):
    return lax.dot_general(a, b, (((0,), (0,)), ((), ())), preferred_element_type=F32)


def _rmsnorm_kernel(x_ref, g_ref, o_ref):
    x = x_ref[...]
    ms = jnp.mean(x * x, axis=-1, keepdims=True)
    o_ref[...] = (x * lax.rsqrt(ms + EPS) * g_ref[...]).astype(o_ref.dtype)


def rmsnorm(x, g, *, rows=NORM_ROWS):
    m, d = x.shape
    return pl.pallas_call(
        _rmsnorm_kernel,
        grid=(m // rows,),
        in_specs=[pl.BlockSpec((rows, d), lambda i: (i, 0)),
                  pl.BlockSpec((1, d), lambda i: (0, 0))],
        out_specs=pl.BlockSpec((rows, d), lambda i: (i, 0)),
        out_shape=jax.ShapeDtypeStruct((m, d), BF16),
        compiler_params=_params("parallel"),
        name="rmsnorm",
    )(x, g.reshape(1, d))


def _weight_spec(w, layer, block, index):
    if layer is None:
        return pl.BlockSpec(block, index)
    return pl.BlockSpec((None,) + block, lambda *ids: (layer,) + index(*ids))


class SideCast:
    def __init__(self, w, layer, block, grid):
        _, r, c = w.shape
        br, bc = block
        n_col = c // bc
        n_blocks = (r // br) * n_col
        assert r % br == 0 and c % bc == 0 and n_blocks <= math.prod(grid)

        def block_index(*ids):
            t = ids[0]
            for extent, i in zip(grid[1:], ids[1:]):
                t = t * extent + i
            t = jnp.minimum(t, n_blocks - 1)
            return t // n_col, t % n_col

        self.w = w
        self.in_spec = pl.BlockSpec((None, br, bc), lambda *ids: (layer,) + block_index(*ids))
        self.out_spec = pl.BlockSpec((br, bc), block_index)
        self.out_shape = jax.ShapeDtypeStruct((r, c), BF16)


def _side_casts(sides, grid):
    return [SideCast(w, layer, block, grid) for w, layer, block in sides]


def _with_side_casts(body, n_in, n_out, n_side):
    def wrapped(*refs):
        ins, rest = refs[:n_in], refs[n_in:]
        side_in, rest = rest[:n_side], rest[n_side:]
        outs, rest = rest[:n_out], rest[n_out:]
        side_out, scratch = rest[:n_side], rest[n_side:]

        def cast_sides():
            for src, dst in zip(side_in, side_out):
                dst[...] = src[...].astype(dst.dtype)

        body(*ins, *outs, *scratch, cast_sides=cast_sides)

    return wrapped


def _matmul_kernel(x_ref, w_ref, o_ref, *, cast_sides):
    cast_sides()
    o_ref[...] = _dot(x_ref[...], w_ref[...])


def matmul(x, w, layer, sides, *, tm, tn):
    m, k = x.shape
    n = w.shape[-1]
    grid = (m // tm, n // tn)
    side = _side_casts(sides, grid)
    return pl.pallas_call(
        _with_side_casts(_matmul_kernel, 2, 1, len(side)),
        grid=grid,
        in_specs=[pl.BlockSpec((tm, k), lambda i, j: (i, 0)),
                  _weight_spec(w, layer, (k, tn), lambda i, j: (0, j))]
                 + [sc.in_spec for sc in side],
        out_specs=[pl.BlockSpec((tm, tn), lambda i, j: (i, j))] + [sc.out_spec for sc in side],
        out_shape=[jax.ShapeDtypeStruct((m, n), F32)] + [sc.out_shape for sc in side],
        compiler_params=_params("arbitrary", "arbitrary"),
        name="proj_matmul",
    )(x, w, *[sc.w for sc in side])


def _matmul_res_kernel(x_ref, w_ref, r_ref, o_ref):
    k = pl.program_id(2)

    @pl.when(k == 0)
    def _():
        o_ref[...] = r_ref[...] + _dot(x_ref[...], w_ref[...])

    @pl.when(k > 0)
    def _():
        o_ref[...] += _dot(x_ref[...], w_ref[...])


def matmul_residual(x, w, layer, res, *, tm, tn, tk):
    m, k = x.shape
    n = w.shape[-1]
    return pl.pallas_call(
        _matmul_res_kernel,
        grid=(m // tm, n // tn, k // tk),
        in_specs=[pl.BlockSpec((tm, tk), lambda i, j, kk: (i, kk)),
                  _weight_spec(w, layer, (tk, tn), lambda i, j, kk: (kk, j)),
                  pl.BlockSpec((tm, tn), lambda i, j, kk: (i, j))],
        out_specs=pl.BlockSpec((tm, tn), lambda i, j, kk: (i, j)),
        out_shape=jax.ShapeDtypeStruct((m, n), F32),
        compiler_params=_params("parallel", "parallel", "arbitrary"),
        name="matmul_residual",
    )(x, w, res)


def _hgrn_kernel(lbp_ref, gain_ref, q_ref, f_ref, i_ref, g_ref, o_ref, st_ref, ck_s, v_s,
                 *, layer, n_chunks, n_heads):
    n = pl.program_id(2)

    @pl.when(n == 0)
    def _():
        st_ref[...] = jnp.zeros_like(st_ref)

    for other in range(1, N_BRANCHES):
        o_ref[other] = jnp.zeros(o_ref.shape[1:], o_ref.dtype)

    lbp = lbp_ref[...]
    e = jnp.exp(lbp - jnp.max(lbp, axis=0, keepdims=True))
    sm = e / jnp.sum(e, axis=0, keepdims=True)
    cs = sm[0:1]
    for li in range(1, layer + 1):
        cs = cs + sm[li:li + 1]
    lb_all = cs - sm[0:1]

    gain = gain_ref[...]
    n_sub = A_CHUNK // A_SUB
    row = lax.broadcasted_iota(jnp.int32, (A_CHUNK, A_CHUNK), 0)
    col = lax.broadcasted_iota(jnp.int32, (A_CHUNK, A_CHUNK), 1)
    tri = (row >= col).astype(F32)
    off_mask = (row // A_SUB) > (col // A_SUB)
    rblk = lax.broadcasted_iota(jnp.int32, (A_CHUNK, HEAD_DIM), 0) // A_SUB
    half_row = lax.broadcasted_iota(jnp.int32, (SUBLANES, HEAD_DIM), 0)
    scale = HEAD_DIM ** -0.5

    def head_chunk(hh, r0):
        sl = pl.ds(r0, A_CHUNK)
        hs = slice(hh * HEAD_DIM, (hh + 1) * HEAD_DIM)
        lb = lb_all[:, hs]
        q = q_ref[sl, hs] * scale
        v = i_ref[sl, hs]
        f = lb + (1.0 - lb) * jax.nn.sigmoid(f_ref[sl, hs])
        kk = 1.0 - f
        cum = jnp.dot(tri, jnp.log2(f), precision=lax.Precision.HIGHEST,
                      preferred_element_type=F32)
        ends = [cum[(j + 1) * A_SUB - 1:(j + 1) * A_SUB, :] for j in range(n_sub)]
        last = ends[-1]
        eblk = jnp.concatenate([jnp.broadcast_to(ej, (A_SUB, HEAD_DIM)) for ej in ends], axis=0)
        khat = kk * jnp.exp2(eblk - cum)
        st = st_ref[hh]

        o = _dot_nt((q * jnp.exp2(cum)).astype(BF16), st.astype(BF16))

        s_off = jnp.zeros((A_CHUNK, A_CHUNK), F32)
        for j in range(n_sub - 1):
            qj = q * jnp.exp2(jnp.minimum(cum - ends[j], 0.0))
            kj = jnp.where(rblk == j, khat, 0.0)
            s_off = s_off + _dot_nt(qj.astype(BF16), kj.astype(BF16))
        s_off = jnp.where(off_mask, s_off, 0.0)
        o = o + _dot(s_off.astype(BF16), v.astype(BF16))

        ck_s[hh] = cum - jnp.log2(kk)
        v_s[hh] = v
        diag = []
        for b in range(n_sub):
            base = b * A_SUB
            q_lo, q_hi = q[base:base + SUBLANES], q[base + SUBLANES:base + A_SUB]
            c_lo, c_hi = cum[base:base + SUBLANES], cum[base + SUBLANES:base + A_SUB]
            o_lo = jnp.zeros((SUBLANES, HEAD_DIM), F32)
            o_hi = jnp.zeros((SUBLANES, HEAD_DIM), F32)
            for s in range(A_SUB):
                r = base + s
                c_row = ck_s[hh, r:r + 1, :]
                v_row = v_s[hh, r:r + 1, :]
                if s < SUBLANES:
                    dec = jnp.exp2(jnp.where(half_row >= s, c_lo - c_row, NEG_BIG))
                    o_lo = o_lo + jnp.sum(q_lo * dec, axis=-1, keepdims=True) * v_row
                    dec = jnp.exp2(c_hi - c_row)
                else:
                    dec = jnp.exp2(jnp.where(half_row >= s - SUBLANES, c_hi - c_row, NEG_BIG))
                o_hi = o_hi + jnp.sum(q_hi * dec, axis=-1, keepdims=True) * v_row
            diag += [o_lo, o_hi]
        o = o + jnp.concatenate(diag, axis=0)

        ktil = khat * jnp.exp2(last - eblk)
        st_ref[hh] = st * jnp.exp2(last) + _dot_tn(v.astype(BF16), ktil.astype(BF16))

        ms = jnp.mean(o * o, axis=-1, keepdims=True)
        on = o * lax.rsqrt(ms + EPS) * gain
        g = g_ref[sl, hs]
        o_ref[0, sl, hs] = (on * (g * jax.nn.sigmoid(g))).astype(o_ref.dtype)

    def chunk(c, carry):
        r0 = pl.multiple_of(c * A_CHUNK, A_CHUNK)
        for hh in range(n_heads):
            head_chunk(hh, r0)
        return carry

    lax.fori_loop(0, n_chunks, chunk, 0, unroll=A_UNROLL)


def hgrn2_mixer(proj3, lower_bounds, out_gain, layer):
    bsz, s, _ = proj3.shape
    depth = lower_bounds.shape[0]
    ts = min(A_STEP_ROWS, s)
    hb = A_STEP_HEADS
    n_hg = N_HEADS // hb
    wide = hb * HEAD_DIM
    blk = lambda part: pl.BlockSpec((None, ts, wide), lambda b, h, n: (b, n, part * n_hg + h))
    return pl.pallas_call(
        functools.partial(_hgrn_kernel, layer=layer, n_chunks=ts // A_CHUNK, n_heads=hb),
        grid=(bsz, n_hg, s // ts),
        in_specs=[pl.BlockSpec((depth, wide), lambda b, h, n: (0, h)),
                  pl.BlockSpec((1, HEAD_DIM), lambda b, h, n: (0, 0)),
                  blk(0), blk(1), blk(2), blk(3)],
        out_specs=pl.BlockSpec((N_BRANCHES, None, ts, wide), lambda b, h, n: (0, b, n, h)),
        out_shape=jax.ShapeDtypeStruct((N_BRANCHES, bsz, s, BRANCH_WIDTH), BF16),
        scratch_shapes=[pltpu.VMEM((hb, HEAD_DIM, HEAD_DIM), F32)]
                       + [pltpu.VMEM((hb, A_CHUNK, HEAD_DIM), F32)] * 2,
        compiler_params=_params("parallel", "parallel", "arbitrary"),
        name="hgrn2",
    )(lower_bounds, out_gain.reshape(1, HEAD_DIM), proj3, proj3, proj3, proj3)


def _rope_tables(positions):
    inv = jnp.power(jnp.float32(ROPE_THETA), -jnp.arange(0, ROT_DIM, 2, dtype=F32) / ROT_DIM)
    ang = positions.astype(F32)[..., None] * inv
    cos = jnp.cos(ang)
    sin = jnp.sin(ang)
    pad = jnp.zeros(ang.shape[:-1] + (HEAD_DIM - ROT_DIM,), F32)
    cos_f = jnp.concatenate([cos, cos, pad + 1.0], axis=-1)
    sin_f = jnp.concatenate([sin, sin, pad], axis=-1)
    return cos_f, sin_f


def _dilated_kernel(qg_ref, kg_ref, cos_ref, sin_ref, *rest):
    n_grp = len(B_PATTERNS)
    qkv = rest[:3 * n_grp]
    y_ref = rest[3 * n_grp + 1]
    scratch = list(rest[3 * n_grp + 2:])
    qn_s, kn_s, kc, vc, num_s, den_s, m_s = (scratch[i * n_grp:(i + 1) * n_grp] for i in range(7))
    s_s, p_s, vb_s = scratch[7 * n_grp:]
    n = pl.program_id(2)

    @pl.when(n == 0)
    def _():
        for g in range(n_grp):
            kc[g][...] = jnp.zeros_like(kc[g])
            vc[g][...] = jnp.zeros_like(vc[g])

    q_gain = qg_ref[...] * (HEAD_DIM ** -0.5 * math.log2(math.e))
    k_gain = kg_ref[...]

    src = lax.broadcasted_iota(jnp.int32, (HEAD_DIM, HEAD_DIM), 0)
    dst = lax.broadcasted_iota(jnp.int32, (HEAD_DIM, HEAD_DIM), 1)
    rot = jnp.where(jnp.logical_and(dst < ROT_HALF, src == dst + ROT_HALF), -1.0,
                    jnp.where(jnp.logical_and(jnp.logical_and(dst >= ROT_HALF, dst < ROT_DIM),
                                              src == dst - ROT_HALF), 1.0, 0.0)).astype(BF16)
    rot2 = jnp.concatenate([rot, rot], axis=0)

    def norm_rope(x, gain, cos, sin):
        ms = jnp.mean(x * x, axis=-1, keepdims=True)
        xn = x * lax.rsqrt(ms + EPS) * gain
        hi = xn.astype(BF16)
        lo = (xn - hi.astype(F32)).astype(BF16)
        return xn * cos + _dot(jnp.concatenate([hi, lo], axis=1), rot2) * sin

    cos, sin = cos_ref[...], sin_ref[...]
    for g in range(n_grp):
        q_ref, k_ref = qkv[3 * g], qkv[3 * g + 1]
        qn_s[g][...] = norm_rope(q_ref[...], q_gain[g:g + 1, :], cos, sin)
        kn_s[g][...] = norm_rope(k_ref[...], k_gain[g:g + 1, :], cos, sin)

    qi = lax.broadcasted_iota(jnp.int32, (B_BLOCK, 2 * B_BLOCK), 0)
    ci = lax.broadcasted_iota(jnp.int32, (B_BLOCK, 2 * B_BLOCK), 1)
    cur_ok = jnp.logical_and(ci >= B_BLOCK, ci - B_BLOCK <= qi)
    prev_ok = jnp.logical_and(ci < B_BLOCK, ci >= qi)
    mask_inner = jnp.logical_or(cur_ok, prev_ok)
    mask_first = jnp.logical_or(cur_ok, jnp.logical_and(prev_ok, n > 0))
    ones = jnp.ones((2 * B_BLOCK, HEAD_DIM), BF16)

    for g, (win, dil) in enumerate(B_PATTERNS):
        v_ref = qkv[3 * g + 2]
        n_blk = B_TILE // (B_BLOCK * dil)
        blocks = [(r, j) for r in range(dil) for j in range(n_blk)]
        k_prev = v_prev = None
        for b0 in range(0, len(blocks), B_BATCH):
            batch = blocks[b0:b0 + B_BATCH]
            rows_of = []
            for i, (r, j) in enumerate(batch):
                start = j * B_BLOCK * dil + r
                rows = pl.ds(start, B_BLOCK) if dil == 1 else pl.ds(start, B_BLOCK, stride=dil)
                rows_of.append(rows)
                blk = slice(i * B_BLOCK, (i + 1) * B_BLOCK)
                if j == 0:
                    k_prev, v_prev = kc[g][r], vc[g][r]
                k_cur = kn_s[g][rows, :].astype(BF16)
                v_cur = v_ref[rows, :].astype(BF16)
                q = qn_s[g][rows, :].astype(BF16)
                s = _dot_nt(q, jnp.concatenate([k_prev, k_cur], axis=0))
                s_s[blk, :] = jnp.where(mask_first if j == 0 else mask_inner, s, NEG_BIG)
                vb_s[i, 0:B_BLOCK, :] = v_prev
                vb_s[i, B_BLOCK:2 * B_BLOCK, :] = v_cur
                if j == n_blk - 1:
                    kc[g][r] = k_cur
                    vc[g][r] = v_cur
                k_prev, v_prev = k_cur, v_cur
            live = slice(0, len(batch) * B_BLOCK)
            m = jnp.max(jnp.maximum(s_s[live, 0:B_BLOCK], s_s[live, B_BLOCK:2 * B_BLOCK]),
                        axis=-1, keepdims=True)
            p_s[live, :] = jnp.exp2(s_s[live, :] - m).astype(BF16)
            for i, rows in enumerate(rows_of):
                blk = slice(i * B_BLOCK, (i + 1) * B_BLOCK)
                pv = _dot(p_s[blk, :], jnp.concatenate([vb_s[i], ones], axis=1))
                num_s[g][rows, :] = pv[:, :HEAD_DIM]
                den_s[g][rows, :] = pv[:, HEAD_DIM:]
                m_s[g][rows, :] = jnp.broadcast_to(m[blk], (B_BLOCK, HEAD_DIM))

    ms = [ref[...] for ref in m_s]
    top = functools.reduce(jnp.maximum, ms)
    w = [jnp.exp2(x - top) for x in ms]
    num = functools.reduce(jnp.add, [wg * ref[...] for wg, ref in zip(w, num_s)])
    den = functools.reduce(jnp.add, [wg * ref[...] for wg, ref in zip(w, den_s)])
    y_ref[...] = (num / den).astype(y_ref.dtype)


def dilated_mixer(proj3, tables, q_gain, k_gain, col0, ys, branch):
    bsz, s, _ = proj3.shape
    n_grp = len(B_PATTERNS)
    assert s % B_TILE == 0

    def part(g, p):
        base = (col0 + (3 * g + p) * BRANCH_WIDTH) // HEAD_DIM
        return pl.BlockSpec((None, B_TILE, HEAD_DIM), lambda b, h, n: (b, n, base + h))

    tab = pl.BlockSpec((None, B_TILE, HEAD_DIM), lambda b, h, n: (b, n, 0))
    gain = pl.BlockSpec((n_grp, HEAD_DIM), lambda b, h, n: (0, 0))
    for win, dil in B_PATTERNS:
        assert win // dil == B_BLOCK and B_TILE % (B_BLOCK * dil) == 0
    token_f32 = [pltpu.VMEM((B_TILE, HEAD_DIM), F32)] * n_grp
    carry = [pltpu.VMEM((dil, B_BLOCK, HEAD_DIM), BF16) for _, dil in B_PATTERNS]
    scratch = (token_f32 + token_f32 + carry + carry + token_f32 + token_f32 + token_f32
               + [pltpu.VMEM((B_BATCH * B_BLOCK, 2 * B_BLOCK), F32),
                  pltpu.VMEM((B_BATCH * B_BLOCK, 2 * B_BLOCK), BF16),
                  pltpu.VMEM((B_BATCH, 2 * B_BLOCK, HEAD_DIM), BF16)])
    return pl.pallas_call(
        _dilated_kernel,
        grid=(bsz, N_HEADS, s // B_TILE),
        in_specs=[gain, gain, tab, tab]
                 + [part(g, p) for g in range(n_grp) for p in range(3)]
                 + [pl.BlockSpec(memory_space=pl.ANY)],
        out_specs=pl.BlockSpec((None, None, B_TILE, HEAD_DIM), lambda b, h, n: (branch, b, n, h)),
        out_shape=jax.ShapeDtypeStruct(ys.shape, ys.dtype),
        input_output_aliases={4 + 3 * n_grp: 0},
        scratch_shapes=scratch,
        compiler_params=_params("parallel", "parallel", "arbitrary"),
        name="dilated_attn",
    )(q_gain, k_gain, *tables, *([proj3] * (3 * n_grp)), ys)


def _spatial_kernel(u_ref, v_ref, lng_ref, lnb_ref, w_ref, b_ref, ys_ref, o_ref):
    inv_sqrt2 = 1.0 / math.sqrt(2.0)

    def gelu(x):
        return 0.5 * x * (1.0 + lax.erf(x * inv_sqrt2))

    row = lax.broadcasted_iota(jnp.int32, (C_CHUNK, C_CHUNK), 0)
    col = lax.broadcasted_iota(jnp.int32, (C_CHUNK, C_CHUNK), 1)
    causal = row >= col
    bias = b_ref[...]
    w_causal = [jnp.where(causal, w_ref[g], 0.0).astype(BF16) for g in range(N_HEADS)]
    for c in range(u_ref.shape[0] // C_CHUNK):
        rows = slice(c * C_CHUNK, (c + 1) * C_CHUNK)
        v = gelu(v_ref[rows, :])
        mu = jnp.mean(v, axis=-1, keepdims=True)
        vc = v - mu
        var = jnp.mean(vc * vc, axis=-1, keepdims=True)
        vn = vc * lax.rsqrt(var + EPS) * lng_ref[...] + lnb_ref[...]
        for g in range(N_HEADS):
            gs = slice(g * HEAD_DIM, (g + 1) * HEAD_DIM)
            mixed = _dot(w_causal[g], vn[:, gs].astype(BF16)) + bias[:, g:g + 1]
            o_ref[rows, gs] = (gelu(u_ref[rows, gs]) * mixed).astype(o_ref.dtype)


def spatial_gating(proj, ln_g, ln_b, w_s, b_s, col0, ys, branch):
    m, _ = proj.shape
    ublk = col0 // BRANCH_WIDTH
    rows = C_STEP_ROWS
    return pl.pallas_call(
        _spatial_kernel,
        grid=(m // rows,),
        in_specs=[pl.BlockSpec((rows, BRANCH_WIDTH), lambda i: (i, ublk)),
                  pl.BlockSpec((rows, BRANCH_WIDTH), lambda i: (i, ublk + 1)),
                  pl.BlockSpec((1, BRANCH_WIDTH), lambda i: (0, 0)),
                  pl.BlockSpec((1, BRANCH_WIDTH), lambda i: (0, 0)),
                  pl.BlockSpec((N_HEADS, C_CHUNK, C_CHUNK), lambda i: (0, 0, 0)),
                  pl.BlockSpec((C_CHUNK, N_HEADS), lambda i: (0, 0)),
                  pl.BlockSpec(memory_space=pl.ANY)],
        out_specs=pl.BlockSpec((None, rows, BRANCH_WIDTH), lambda i: (branch, i, 0)),
        out_shape=jax.ShapeDtypeStruct(ys.shape, ys.dtype),
        input_output_aliases={6: 0},
        compiler_params=_params("parallel"),
        name="spatial_gating",
    )(proj, proj, ln_g.reshape(1, -1), ln_b.reshape(1, -1), w_s, b_s.T, ys)


def _merge_kernel(xn_ref, y_ref, wg_ref, wb_ref, o_ref, acc_ref, *, cast_sides):
    @pl.when(pl.program_id(2) == 0)
    def _():
        acc_ref[...] = jnp.zeros_like(acc_ref)

    cast_sides()
    gate = jax.nn.sigmoid(_dot(xn_ref[...], wg_ref[...]))
    total = acc_ref[...] + gate * _dot(y_ref[...], wb_ref[...])
    acc_ref[...] = total
    o_ref[...] = total.astype(o_ref.dtype)


def gated_merge(xn, ys, w_gate, w_branch, sides, *, tm, tn):
    m, d = xn.shape
    w = ys.shape[2]
    nblk = d // tn
    grid = (m // tm, nblk, N_BRANCHES)
    side = _side_casts(sides, grid)
    return pl.pallas_call(
        _with_side_casts(_merge_kernel, 4, 1, len(side)),
        grid=grid,
        in_specs=[pl.BlockSpec((tm, d), lambda a, b, i: (a, 0)),
                  pl.BlockSpec((None, tm, w), lambda a, b, i: (i, a, 0)),
                  pl.BlockSpec((d, tn), lambda a, b, i: (0, i * nblk + b)),
                  pl.BlockSpec((None, w, tn), lambda a, b, i: (i, 0, b))]
                 + [sc.in_spec for sc in side],
        out_specs=[pl.BlockSpec((tm, tn), lambda a, b, i: (a, b))] + [sc.out_spec for sc in side],
        out_shape=[jax.ShapeDtypeStruct((m, d), BF16)] + [sc.out_shape for sc in side],
        scratch_shapes=[pltpu.VMEM((tm, tn), F32)],
        compiler_params=_params("arbitrary", "arbitrary", "arbitrary"),
        name="gated_merge",
    )(xn, ys, w_gate, w_branch, *[sc.w for sc in side])


def _ffn_up_kernel(x_ref, wg_ref, wv_ref, cw_ref, cb_ref, o_ref, g_s, *, cast_sides, tiles_per_seq):
    i = pl.program_id(1)
    tm = x_ref.shape[0]

    @pl.when(i % tiles_per_seq == 0)
    def _():
        g_s[...] = jnp.zeros_like(g_s)

    cast_sides()
    x = x_ref[...]
    gate = _dot(x, wg_ref[...])
    val = _dot(x, wv_ref[...])
    prev = g_s[...]
    head = lax.broadcasted_iota(jnp.int32, prev.shape, 0)
    roll1 = pltpu.roll(gate, 1, 0)
    roll2 = pltpu.roll(gate, 2, 0)
    head1 = jnp.where(head == 0, prev[SUBLANES - 1:SUBLANES], roll1[:SUBLANES])
    head2 = jnp.where(head == 0, prev[SUBLANES - 2:SUBLANES - 1],
                      jnp.where(head == 1, prev[SUBLANES - 1:SUBLANES], roll2[:SUBLANES]))
    back1 = jnp.concatenate([head1, roll1[SUBLANES:]], axis=0)
    back2 = jnp.concatenate([head2, roll2[SUBLANES:]], axis=0)
    g_s[...] = gate[tm - SUBLANES:]
    cw = cw_ref[...]
    conv = cw[0:1] * back2 + cw[1:2] * back1 + cw[2:3] * gate + cb_ref[...]
    o_ref[...] = (conv * jax.nn.sigmoid(conv) * val).astype(o_ref.dtype)


def ffn_up(xn, w_up, conv_w, conv_b, seq_len, sides, *, tm, tn):
    m, d = xn.shape
    dff = conv_w.shape[1]
    nblk = dff // tn
    grid = (nblk, m // tm)
    side = _side_casts(sides, grid)
    body = functools.partial(_ffn_up_kernel, tiles_per_seq=seq_len // tm)
    return pl.pallas_call(
        _with_side_casts(body, 5, 1, len(side)),
        grid=grid,
        in_specs=[pl.BlockSpec((tm, d), lambda j, i: (i, 0)),
                  pl.BlockSpec((d, tn), lambda j, i: (0, j)),
                  pl.BlockSpec((d, tn), lambda j, i: (0, nblk + j)),
                  pl.BlockSpec((CONV_W, tn), lambda j, i: (0, j)),
                  pl.BlockSpec((1, tn), lambda j, i: (0, j))]
                 + [sc.in_spec for sc in side],
        out_specs=[pl.BlockSpec((tm, tn), lambda j, i: (i, j))] + [sc.out_spec for sc in side],
        out_shape=[jax.ShapeDtypeStruct((m, dff), BF16)] + [sc.out_shape for sc in side],
        scratch_shapes=[pltpu.VMEM((SUBLANES, tn), F32)],
        compiler_params=_params("arbitrary", "arbitrary"),
        name="ffn_up_conv",
    )(xn, w_up, w_up, conv_w, conv_b.reshape(1, dff), *[sc.w for sc in side])


def kernel(x, positions, norm_mix, w_in, hgrn_lower_bounds, hgrn_out_norm, q_norm, k_norm,
           sg_ln_g, sg_ln_b, sg_w, sg_b, w_gate, w_branch, w_out, norm_ffn, w_up,
           ffn_conv_w, ffn_conv_b, w_down):
    bsz, s, d = x.shape
    m = bsz * s
    depth = w_in.shape[0]
    in_cols = w_in.shape[2]
    a_cols = 4 * BRANCH_WIDTH
    b_cols = len(B_PATTERNS) * 3 * BRANCH_WIDTH
    tables = _rope_tables(positions)
    w_branch2d = w_branch.reshape(depth, N_BRANCHES * BRANCH_WIDTH, d)
    w_in_l = w_in[0].astype(BF16)
    w_branch_l = w_branch2d[0].astype(BF16)
    xf = x.reshape(m, d)
    for l in range(depth):
        xn = rmsnorm(xf, norm_mix[l])
        proj, w_gate_l = matmul(xn, w_in_l, None, [(w_gate, l, CAST_BLOCK_WIDE)], tm=MM_TM, tn=MM_TN)
        proj3 = proj.reshape(bsz, s, in_cols)
        ys = hgrn2_mixer(proj3, hgrn_lower_bounds, hgrn_out_norm[l], l)
        ys = dilated_mixer(proj3, tables, q_norm[l], k_norm[l], a_cols, ys, 1)
        ys = spatial_gating(proj, sg_ln_g[l], sg_ln_b[l], sg_w[l], sg_b[l], a_cols + b_cols,
                            ys.reshape(N_BRANCHES, m, BRANCH_WIDTH), 2)
        merged, w_up_l, w_out_l = gated_merge(
            xn, ys, w_gate_l, w_branch_l.reshape(N_BRANCHES, BRANCH_WIDTH, d),
            [(w_up, l, CAST_BLOCK_WIDE), (w_out, l, CAST_BLOCK)], tm=MM_TM, tn=MERGE_TN)
        xf = matmul_residual(merged, w_out_l, None, xf, tm=MM_TM, tn=MM_TN, tk=d)
        xn = rmsnorm(xf, norm_ffn[l])
        sides = [(w_down, l, CAST_BLOCK)]
        if l + 1 < depth:
            sides += [(w_in, l + 1, CAST_BLOCK_WIDE), (w_branch2d, l + 1, CAST_BLOCK)]
        h, w_down_l, *next_layer = ffn_up(xn, w_up_l, ffn_conv_w[l], ffn_conv_b[l], s, sides,
                                          tm=MM_TM, tn=FFN_TN)
        if next_layer:
            w_in_l, w_branch_l = next_layer
        xf = matmul_residual(h, w_down_l, None, xf, tm=MM_TM, tn=MM_TN, tk=d)
    return xf.reshape(bsz, s, d)
```

```python
import functools
import math

import jax
import jax.numpy as jnp
from jax import lax
from jax.experimental import pallas as pl
from jax.experimental.pallas import tpu as pltpu

F32 = jnp.float32
BF16 = jnp.bfloat16

EPS = 1e-6
NEG_BIG = -1e30

LANES = 128
SUBLANES = 8
VMEM_LIMIT = 56 * 1024 * 1024

HEAD_DIM = 128
N_HEADS = 8
BRANCH_WIDTH = N_HEADS * HEAD_DIM
N_BRANCHES = 3

A_CHUNK = 64
A_SUB = 16
A_STEP_ROWS = 512
A_STEP_HEADS = 8
A_UNROLL = 8

B_PATTERNS = ((128, 1), (512, 4), (2048, 16))
B_BLOCK = 128
B_TILE = 2048
B_BATCH = 8
ROPE_THETA = 500000.0
ROT_DIM = HEAD_DIM // 4
ROT_HALF = ROT_DIM // 2

C_CHUNK = 128
C_STEP_ROWS = 512
CONV_W = 3

NORM_ROWS = 512
MM_TM = 1024
MM_TN = 1024
MERGE_TN = 512
FFN_TN = 512
CAST_BLOCK = (512, 512)
CAST_BLOCK_WIDE = (512, 1024)


def _params(*sem):
    return pltpu.CompilerParams(dimension_semantics=sem, vmem_limit_bytes=VMEM_LIMIT)


def _dot(a, b):
    return jnp.dot(a, b, preferred_element_type=F32)


def _dot_nt(a, b):
    return lax.dot_general(a, b, (((1,), (1,)), ((), ())), preferred_element_type=F32)


def _dot_tn(a, b):
    return lax.dot_general(a, b, (((0,), (0,)), ((), ())), preferred_element_type=F32)


def _rmsnorm_kernel(x_ref, g_ref, o_ref):
    x = x_ref[...]
    ms = jnp.mean(x * x, axis=-1, keepdims=True)
    o_ref[...] = (x * lax.rsqrt(ms + EPS) * g_ref[...]).astype(o_ref.dtype)


def rmsnorm(x, g, *, rows=NORM_ROWS):
    m, d = x.shape
    return pl.pallas_call(
        _rmsnorm_kernel,
        grid=(m // rows,),
        in_specs=[pl.BlockSpec((rows, d), lambda i: (i, 0)),
                  pl.BlockSpec((1, d), lambda i: (0, 0))],
        out_specs=pl.BlockSpec((rows, d), lambda i: (i, 0)),
        out_shape=jax.ShapeDtypeStruct((m, d), BF16),
        compiler_params=_params("parallel"),
        name="rmsnorm",
    )(x, g.reshape(1, d))


def _weight_spec(w, layer, block, index):
    if layer is None:
        return pl.BlockSpec(block, index)
    return pl.BlockSpec((None,) + block, lambda *ids: (layer,) + index(*ids))


class SideCast:
    def __init__(self, w, layer, block, grid):
        _, r, c = w.shape
        br, bc = block
        n_col = c // bc
        n_blocks = (r // br) * n_col
        assert r % br == 0 and c % bc == 0 and n_blocks <= math.prod(grid)

        def block_index(*ids):
            t = ids[0]
            for extent, i in zip(grid[1:], ids[1:]):
                t = t * extent + i
            t = jnp.minimum(t, n_blocks - 1)
            return t // n_col, t % n_col

        self.w = w
        self.in_spec = pl.BlockSpec((None, br, bc), lambda *ids: (layer,) + block_index(*ids))
        self.out_spec = pl.BlockSpec((br, bc), block_index)
        self.out_shape = jax.ShapeDtypeStruct((r, c), BF16)


def _side_casts(sides, grid):
    return [SideCast(w, layer, block, grid) for w, layer, block in sides]


def _with_side_casts(body, n_in, n_out, n_side):
    def wrapped(*refs):
        ins, rest = refs[:n_in], refs[n_in:]
        side_in, rest = rest[:n_side], rest[n_side:]
        outs, rest = rest[:n_out], rest[n_out:]
        side_out, scratch = rest[:n_side], rest[n_side:]

        def cast_sides():
            for src, dst in zip(side_in, side_out):
                dst[...] = src[...].astype(dst.dtype)

        body(*ins, *outs, *scratch, cast_sides=cast_sides)

    return wrapped


def _matmul_kernel(x_ref, w_ref, o_ref, *, cast_sides):
    cast_sides()
    o_ref[...] = _dot(x_ref[...], w_ref[...])


def matmul(x, w, layer, sides, *, tm, tn):
    m, k = x.shape
    n = w.shape[-1]
    grid = (m // tm, n // tn)
    side = _side_casts(sides, grid)
    return pl.pallas_call(
        _with_side_casts(_matmul_kernel, 2, 1, len(side)),
        grid=grid,
        in_specs=[pl.BlockSpec((tm, k), lambda i, j: (i, 0)),
                  _weight_spec(w, layer, (k, tn), lambda i, j: (0, j))]
                 + [sc.in_spec for sc in side],
        out_specs=[pl.BlockSpec((tm, tn), lambda i, j: (i, j))] + [sc.out_spec for sc in side],
        out_shape=[jax.ShapeDtypeStruct((m, n), F32)] + [sc.out_shape for sc in side],
        compiler_params=_params("arbitrary", "arbitrary"),
        name="proj_matmul",
    )(x, w, *[sc.w for sc in side])


def _matmul_res_kernel(x_ref, w_ref, r_ref, o_ref):
    k = pl.program_id(2)

    @pl.when(k == 0)
    def _():
        o_ref[...] = r_ref[...] + _dot(x_ref[...], w_ref[...])

    @pl.when(k > 0)
    def _():
        o_ref[...] += _dot(x_ref[...], w_ref[...])


def matmul_residual(x, w, layer, res, *, tm, tn, tk):
    m, k = x.shape
    n = w.shape[-1]
    return pl.pallas_call(
        _matmul_res_kernel,
        grid=(m // tm, n // tn, k // tk),
        in_specs=[pl.BlockSpec((tm, tk), lambda i, j, kk: (i, kk)),
                  _weight_spec(w, layer, (tk, tn), lambda i, j, kk: (kk, j)),
                  pl.BlockSpec((tm, tn), lambda i, j, kk: (i, j))],
        out_specs=pl.BlockSpec((tm, tn), lambda i, j, kk: (i, j)),
        out_shape=jax.ShapeDtypeStruct((m, n), F32),
        compiler_params=_params("parallel", "parallel", "arbitrary"),
        name="matmul_residual",
    )(x, w, res)


def _hgrn_kernel(lbp_ref, gain_ref, q_ref, f_ref, i_ref, g_ref, o_ref, st_ref, ck_s, v_s,
                 *, layer, n_chunks, n_heads):
    n = pl.program_id(2)

    @pl.when(n == 0)
    def _():
        st_ref[...] = jnp.zeros_like(st_ref)

    for other in range(1, N_BRANCHES):
        o_ref[other] = jnp.zeros(o_ref.shape[1:], o_ref.dtype)

    lbp = lbp_ref[...]
    e = jnp.exp(lbp - jnp.max(lbp, axis=0, keepdims=True))
    sm = e / jnp.sum(e, axis=0, keepdims=True)
    cs = sm[0:1]
    for li in range(1, layer + 1):
        cs = cs + sm[li:li + 1]
    lb_all = cs - sm[0:1]

    gain = gain_ref[...]
    n_sub = A_CHUNK // A_SUB
    row = lax.broadcasted_iota(jnp.int32, (A_CHUNK, A_CHUNK), 0)
    col = lax.broadcasted_iota(jnp.int32, (A_CHUNK, A_CHUNK), 1)
    tri = (row >= col).astype(F32)
    off_mask = (row // A_SUB) > (col // A_SUB)
    rblk = lax.broadcasted_iota(jnp.int32, (A_CHUNK, HEAD_DIM), 0) // A_SUB
    half_row = lax.broadcasted_iota(jnp.int32, (SUBLANES, HEAD_DIM), 0)
    scale = HEAD_DIM ** -0.5

    def head_chunk(hh, r0):
        sl = pl.ds(r0, A_CHUNK)
        hs = slice(hh * HEAD_DIM, (hh + 1) * HEAD_DIM)
        lb = lb_all[:, hs]
        q = q_ref[sl, hs] * scale
        v = i_ref[sl, hs]
        f = lb + (1.0 - lb) * jax.nn.sigmoid(f_ref[sl, hs])
        kk = 1.0 - f
        cum = jnp.dot(tri, jnp.log2(f), precision=lax.Precision.HIGHEST,
                      preferred_element_type=F32)
        ends = [cum[(j + 1) * A_SUB - 1:(j + 1) * A_SUB, :] for j in range(n_sub)]
        last = ends[-1]
        eblk = jnp.concatenate([jnp.broadcast_to(ej, (A_SUB, HEAD_DIM)) for ej in ends], axis=0)
        khat = kk * jnp.exp2(eblk - cum)
        st = st_ref[hh]

        o = _dot_nt((q * jnp.exp2(cum)).astype(BF16), st.astype(BF16))

        s_off = jnp.zeros((A_CHUNK, A_CHUNK), F32)
        for j in range(n_sub - 1):
            qj = q * jnp.exp2(jnp.minimum(cum - ends[j], 0.0))
            kj = jnp.where(rblk == j, khat, 0.0)
            s_off = s_off + _dot_nt(qj.astype(BF16), kj.astype(BF16))
        s_off = jnp.where(off_mask, s_off, 0.0)
        o = o + _dot(s_off.astype(BF16), v.astype(BF16))

        ck_s[hh] = cum - jnp.log2(kk)
        v_s[hh] = v
        diag = []
        for b in range(n_sub):
            base = b * A_SUB
            q_lo, q_hi = q[base:base + SUBLANES], q[base + SUBLANES:base + A_SUB]
            c_lo, c_hi = cum[base:base + SUBLANES], cum[base + SUBLANES:base + A_SUB]
            o_lo = jnp.zeros((SUBLANES, HEAD_DIM), F32)
            o_hi = jnp.zeros((SUBLANES, HEAD_DIM), F32)
            for s in range(A_SUB):
                r = base + s
                c_row = ck_s[hh, r:r + 1, :]
                v_row = v_s[hh, r:r + 1, :]
                if s < SUBLANES:
                    dec = jnp.exp2(jnp.where(half_row >= s, c_lo - c_row, NEG_BIG))
                    o_lo = o_lo + jnp.sum(q_lo * dec, axis=-1, keepdims=True) * v_row
                    dec = jnp.exp2(c_hi - c_row)
                else:
                    dec = jnp.exp2(jnp.where(half_row >= s - SUBLANES, c_hi - c_row, NEG_BIG))
                o_hi = o_hi + jnp.sum(q_hi * dec, axis=-1, keepdims=True) * v_row
            diag += [o_lo, o_hi]
        o = o + jnp.concatenate(diag, axis=0)

        ktil = khat * jnp.exp2(last - eblk)
        st_ref[hh] = st * jnp.exp2(last) + _dot_tn(v.astype(BF16), ktil.astype(BF16))

        ms = jnp.mean(o * o, axis=-1, keepdims=True)
        on = o * lax.rsqrt(ms + EPS) * gain
        g = g_ref[sl, hs]
        o_ref[0, sl, hs] = (on * (g * jax.nn.sigmoid(g))).astype(o_ref.dtype)

    def chunk(c, carry):
        r0 = pl.multiple_of(c * A_CHUNK, A_CHUNK)
        for hh in range(n_heads):
            head_chunk(hh, r0)
        return carry

    lax.fori_loop(0, n_chunks, chunk, 0, unroll=A_UNROLL)


def hgrn2_mixer(proj3, lower_bounds, out_gain, layer):
    bsz, s, _ = proj3.shape
    depth = lower_bounds.shape[0]
    ts = min(A_STEP_ROWS, s)
    hb = A_STEP_HEADS
    n_hg = N_HEADS // hb
    wide = hb * HEAD_DIM
    blk = lambda part: pl.BlockSpec((None, ts, wide), lambda b, h, n: (b, n, part * n_hg + h))
    return pl.pallas_call(
        functools.partial(_hgrn_kernel, layer=layer, n_chunks=ts // A_CHUNK, n_heads=hb),
        grid=(bsz, n_hg, s // ts),
        in_specs=[pl.BlockSpec((depth, wide), lambda b, h, n: (0, h)),
                  pl.BlockSpec((1, HEAD_DIM), lambda b, h, n: (0, 0)),
                  blk(0), blk(1), blk(2), blk(3)],
        out_specs=pl.BlockSpec((N_BRANCHES, None, ts, wide), lambda b, h, n: (0, b, n, h)),
        out_shape=jax.ShapeDtypeStruct((N_BRANCHES, bsz, s, BRANCH_WIDTH), BF16),
        scratch_shapes=[pltpu.VMEM((hb, HEAD_DIM, HEAD_DIM), F32)]
                       + [pltpu.VMEM((hb, A_CHUNK, HEAD_DIM), F32)] * 2,
        compiler_params=_params("parallel", "parallel", "arbitrary"),
        name="hgrn2",
    )(lower_bounds, out_gain.reshape(1, HEAD_DIM), proj3, proj3, proj3, proj3)


def _rope_tables(positions):
    inv = jnp.power(jnp.float32(ROPE_THETA), -jnp.arange(0, ROT_DIM, 2, dtype=F32) / ROT_DIM)
    ang = positions.astype(F32)[..., None] * inv
    cos = jnp.cos(ang)
    sin = jnp.sin(ang)
    pad = jnp.zeros(ang.shape[:-1] + (HEAD_DIM - ROT_DIM,), F32)
    cos_f = jnp.concatenate([cos, cos, pad + 1.0], axis=-1)
    sin_f = jnp.concatenate([sin, sin, pad], axis=-1)
    return cos_f, sin_f


def _dilated_kernel(qg_ref, kg_ref, cos_ref, sin_ref, *rest):
    n_grp = len(B_PATTERNS)
    qkv = rest[:3 * n_grp]
    y_ref = rest[3 * n_grp + 1]
    scratch = list(rest[3 * n_grp + 2:])
    qn_s, kn_s, kc, vc, num_s, den_s, m_s = (scratch[i * n_grp:(i + 1) * n_grp] for i in range(7))
    s_s, p_s, vb_s = scratch[7 * n_grp:]
    n = pl.program_id(2)

    @pl.when(n == 0)
    def _():
        for g in range(n_grp):
            kc[g][...] = jnp.zeros_like(kc[g])
            vc[g][...] = jnp.zeros_like(vc[g])

    q_gain = qg_ref[...] * (HEAD_DIM ** -0.5 * math.log2(math.e))
    k_gain = kg_ref[...]

    src = lax.broadcasted_iota(jnp.int32, (HEAD_DIM, HEAD_DIM), 0)
    dst = lax.broadcasted_iota(jnp.int32, (HEAD_DIM, HEAD_DIM), 1)
    rot = jnp.where(jnp.logical_and(dst < ROT_HALF, src == dst + ROT_HALF), -1.0,
                    jnp.where(jnp.logical_and(jnp.logical_and(dst >= ROT_HALF, dst < ROT_DIM),
                                              src == dst - ROT_HALF), 1.0, 0.0)).astype(BF16)
    rot2 = jnp.concatenate([rot, rot], axis=0)

    def norm_rope(x, gain, cos, sin):
        ms = jnp.mean(x * x, axis=-1, keepdims=True)
        xn = x * lax.rsqrt(ms + EPS) * gain
        hi = xn.astype(BF16)
        lo = (xn - hi.astype(F32)).astype(BF16)
        return xn * cos + _dot(jnp.concatenate([hi, lo], axis=1), rot2) * sin

    cos, sin = cos_ref[...], sin_ref[...]
    for g in range(n_grp):
        q_ref, k_ref = qkv[3 * g], qkv[3 * g + 1]
        qn_s[g][...] = norm_rope(q_ref[...], q_gain[g:g + 1, :], cos, sin)
        kn_s[g][...] = norm_rope(k_ref[...], k_gain[g:g + 1, :], cos, sin)

    qi = lax.broadcasted_iota(jnp.int32, (B_BLOCK, 2 * B_BLOCK), 0)
    ci = lax.broadcasted_iota(jnp.int32, (B_BLOCK, 2 * B_BLOCK), 1)
    cur_ok = jnp.logical_and(ci >= B_BLOCK, ci - B_BLOCK <= qi)
    prev_ok = jnp.logical_and(ci < B_BLOCK, ci >= qi)
    mask_inner = jnp.logical_or(cur_ok, prev_ok)
    mask_first = jnp.logical_or(cur_ok, jnp.logical_and(prev_ok, n > 0))
    ones = jnp.ones((2 * B_BLOCK, HEAD_DIM), BF16)

    for g, (win, dil) in enumerate(B_PATTERNS):
        v_ref = qkv[3 * g + 2]
        n_blk = B_TILE // (B_BLOCK * dil)
        blocks = [(r, j) for r in range(dil) for j in range(n_blk)]
        k_prev = v_prev = None
        for b0 in range(0, len(blocks), B_BATCH):
            batch = blocks[b0:b0 + B_BATCH]
            rows_of = []
            for i, (r, j) in enumerate(batch):
                start = j * B_BLOCK * dil + r
                rows = pl.ds(start, B_BLOCK) if dil == 1 else pl.ds(start, B_BLOCK, stride=dil)
                rows_of.append(rows)
                blk = slice(i * B_BLOCK, (i + 1) * B_BLOCK)
                if j == 0:
                    k_prev, v_prev = kc[g][r], vc[g][r]
                k_cur = kn_s[g][rows, :].astype(BF16)
                v_cur = v_ref[rows, :].astype(BF16)
                q = qn_s[g][rows, :].astype(BF16)
                s = _dot_nt(q, jnp.concatenate([k_prev, k_cur], axis=0))
                s_s[blk, :] = jnp.where(mask_first if j == 0 else mask_inner, s, NEG_BIG)
                vb_s[i, 0:B_BLOCK, :] = v_prev
                vb_s[i, B_BLOCK:2 * B_BLOCK, :] = v_cur
                if j == n_blk - 1:
                    kc[g][r] = k_cur
                    vc[g][r] = v_cur
                k_prev, v_prev = k_cur, v_cur
            live = slice(0, len(batch) * B_BLOCK)
            m = jnp.max(jnp.maximum(s_s[live, 0:B_BLOCK], s_s[live, B_BLOCK:2 * B_BLOCK]),
                        axis=-1, keepdims=True)
            p_s[live, :] = jnp.exp2(s_s[live, :] - m).astype(BF16)
            for i, rows in enumerate(rows_of):
                blk = slice(i * B_BLOCK, (i + 1) * B_BLOCK)
                pv = _dot(p_s[blk, :], jnp.concatenate([vb_s[i], ones], axis=1))
                num_s[g][rows, :] = pv[:, :HEAD_DIM]
                den_s[g][rows, :] = pv[:, HEAD_DIM:]
                m_s[g][rows, :] = jnp.broadcast_to(m[blk], (B_BLOCK, HEAD_DIM))

    ms = [ref[...] for ref in m_s]
    top = functools.reduce(jnp.maximum, ms)
    w = [jnp.exp2(x - top) for x in ms]
    num = functools.reduce(jnp.add, [wg * ref[...] for wg, ref in zip(w, num_s)])
    den = functools.reduce(jnp.add, [wg * ref[...] for wg, ref in zip(w, den_s)])
    y_ref[...] = (num / den).astype(y_ref.dtype)


def dilated_mixer(proj3, tables, q_gain, k_gain, col0, ys, branch):
    bsz, s, _ = proj3.shape
    n_grp = len(B_PATTERNS)
    assert s % B_TILE == 0

    def part(g, p):
        base = (col0 + (3 * g + p) * BRANCH_WIDTH) // HEAD_DIM
        return pl.BlockSpec((None, B_TILE, HEAD_DIM), lambda b, h, n: (b, n, base + h))

    tab = pl.BlockSpec((None, B_TILE, HEAD_DIM), lambda b, h, n: (b, n, 0))
    gain = pl.BlockSpec((n_grp, HEAD_DIM), lambda b, h, n: (0, 0))
    for win, dil in B_PATTERNS:
        assert win // dil == B_BLOCK and B_TILE % (B_BLOCK * dil) == 0
    token_f32 = [pltpu.VMEM((B_TILE, HEAD_DIM), F32)] * n_grp
    carry = [pltpu.VMEM((dil, B_BLOCK, HEAD_DIM), BF16) for _, dil in B_PATTERNS]
    scratch = (token_f32 + token_f32 + carry + carry + token_f32 + token_f32 + token_f32
               + [pltpu.VMEM((B_BATCH * B_BLOCK, 2 * B_BLOCK), F32),
                  pltpu.VMEM((B_BATCH * B_BLOCK, 2 * B_BLOCK), BF16),
                  pltpu.VMEM((B_BATCH, 2 * B_BLOCK, HEAD_DIM), BF16)])
    return pl.pallas_call(
        _dilated_kernel,
        grid=(bsz, N_HEADS, s // B_TILE),
        in_specs=[gain, gain, tab, tab]
                 + [part(g, p) for g in range(n_grp) for p in range(3)]
                 + [pl.BlockSpec(memory_space=pl.ANY)],
        out_specs=pl.BlockSpec((None, None, B_TILE, HEAD_DIM), lambda b, h, n: (branch, b, n, h)),
        out_shape=jax.ShapeDtypeStruct(ys.shape, ys.dtype),
        input_output_aliases={4 + 3 * n_grp: 0},
        scratch_shapes=scratch,
        compiler_params=_params("parallel", "parallel", "arbitrary"),
        name="dilated_attn",
    )(q_gain, k_gain, *tables, *([proj3] * (3 * n_grp)), ys)


def _spatial_kernel(u_ref, v_ref, lng_ref, lnb_ref, w_ref, b_ref, ys_ref, o_ref):
    inv_sqrt2 = 1.0 / math.sqrt(2.0)

    def gelu(x):
        return 0.5 * x * (1.0 + lax.erf(x * inv_sqrt2))

    row = lax.broadcasted_iota(jnp.int32, (C_CHUNK, C_CHUNK), 0)
    col = lax.broadcasted_iota(jnp.int32, (C_CHUNK, C_CHUNK), 1)
    causal = row >= col
    bias = b_ref[...]
    w_causal = [jnp.where(causal, w_ref[g], 0.0).astype(BF16) for g in range(N_HEADS)]
    for c in range(u_ref.shape[0] // C_CHUNK):
        rows = slice(c * C_CHUNK, (c + 1) * C_CHUNK)
        v = gelu(v_ref[rows, :])
        mu = jnp.mean(v, axis=-1, keepdims=True)
        vc = v - mu
        var = jnp.mean(vc * vc, axis=-1, keepdims=True)
        vn = vc * lax.rsqrt(var + EPS) * lng_ref[...] + lnb_ref[...]
        for g in range(N_HEADS):
            gs = slice(g * HEAD_DIM, (g + 1) * HEAD_DIM)
            mixed = _dot(w_causal[g], vn[:, gs].astype(BF16)) + bias[:, g:g + 1]
            o_ref[rows, gs] = (gelu(u_ref[rows, gs]) * mixed).astype(o_ref.dtype)


def spatial_gating(proj, ln_g, ln_b, w_s, b_s, col0, ys, branch):
    m, _ = proj.shape
    ublk = col0 // BRANCH_WIDTH
    rows = C_STEP_ROWS
    return pl.pallas_call(
        _spatial_kernel,
        grid=(m // rows,),
        in_specs=[pl.BlockSpec((rows, BRANCH_WIDTH), lambda i: (i, ublk)),
                  pl.BlockSpec((rows, BRANCH_WIDTH), lambda i: (i, ublk + 1)),
                  pl.BlockSpec((1, BRANCH_WIDTH), lambda i: (0, 0)),
                  pl.BlockSpec((1, BRANCH_WIDTH), lambda i: (0, 0)),
                  pl.BlockSpec((N_HEADS, C_CHUNK, C_CHUNK), lambda i: (0, 0, 0)),
                  pl.BlockSpec((C_CHUNK, N_HEADS), lambda i: (0, 0)),
                  pl.BlockSpec(memory_space=pl.ANY)],
        out_specs=pl.BlockSpec((None, rows, BRANCH_WIDTH), lambda i: (branch, i, 0)),
        out_shape=jax.ShapeDtypeStruct(ys.shape, ys.dtype),
        input_output_aliases={6: 0},
        compiler_params=_params("parallel"),
        name="spatial_gating",
    )(proj, proj, ln_g.reshape(1, -1), ln_b.reshape(1, -1), w_s, b_s.T, ys)


def _merge_kernel(xn_ref, y_ref, wg_ref, wb_ref, o_ref, acc_ref, *, cast_sides):
    @pl.when(pl.program_id(2) == 0)
    def _():
        acc_ref[...] = jnp.zeros_like(acc_ref)

    cast_sides()
    gate = jax.nn.sigmoid(_dot(xn_ref[...], wg_ref[...]))
    total = acc_ref[...] + gate * _dot(y_ref[...], wb_ref[...])
    acc_ref[...] = total
    o_ref[...] = total.astype(o_ref.dtype)


def gated_merge(xn, ys, w_gate, w_branch, sides, *, tm, tn):
    m, d = xn.shape
    w = ys.shape[2]
    nblk = d // tn
    grid = (m // tm, nblk, N_BRANCHES)
    side = _side_casts(sides, grid)
    return pl.pallas_call(
        _with_side_casts(_merge_kernel, 4, 1, len(side)),
        grid=grid,
        in_specs=[pl.BlockSpec((tm, d), lambda a, b, i: (a, 0)),
                  pl.BlockSpec((None, tm, w), lambda a, b, i: (i, a, 0)),
                  pl.BlockSpec((d, tn), lambda a, b, i: (0, i * nblk + b)),
                  pl.BlockSpec((None, w, tn), lambda a, b, i: (i, 0, b))]
                 + [sc.in_spec for sc in side],
        out_specs=[pl.BlockSpec((tm, tn), lambda a, b, i: (a, b))] + [sc.out_spec for sc in side],
        out_shape=[jax.ShapeDtypeStruct((m, d), BF16)] + [sc.out_shape for sc in side],
        scratch_shapes=[pltpu.VMEM((tm, tn), F32)],
        compiler_params=_params("arbitrary", "arbitrary", "arbitrary"),
        name="gated_merge",
    )(xn, ys, w_gate, w_branch, *[sc.w for sc in side])


def _ffn_up_kernel(x_ref, wg_ref, wv_ref, cw_ref, cb_ref, o_ref, g_s, *, cast_sides, tiles_per_seq):
    i = pl.program_id(1)
    tm = x_ref.shape[0]

    @pl.when(i % tiles_per_seq == 0)
    def _():
        g_s[...] = jnp.zeros_like(g_s)

    cast_sides()
    x = x_ref[...]
    gate = _dot(x, wg_ref[...])
    val = _dot(x, wv_ref[...])
    prev = g_s[...]
    head = lax.broadcasted_iota(jnp.int32, prev.shape, 0)
    roll1 = pltpu.roll(gate, 1, 0)
    roll2 = pltpu.roll(gate, 2, 0)
    head1 = jnp.where(head == 0, prev[SUBLANES - 1:SUBLANES], roll1[:SUBLANES])
    head2 = jnp.where(head == 0, prev[SUBLANES - 2:SUBLANES - 1],
                      jnp.where(head == 1, prev[SUBLANES - 1:SUBLANES], roll2[:SUBLANES]))
    back1 = jnp.concatenate([head1, roll1[SUBLANES:]], axis=0)
    back2 = jnp.concatenate([head2, roll2[SUBLANES:]], axis=0)
    g_s[...] = gate[tm - SUBLANES:]
    cw = cw_ref[...]
    conv = cw[0:1] * back2 + cw[1:2] * back1 + cw[2:3] * gate + cb_ref[...]
    o_ref[...] = (conv * jax.nn.sigmoid(conv) * val).astype(o_ref.dtype)


def ffn_up(xn, w_up, conv_w, conv_b, seq_len, sides, *, tm, tn):
    m, d = xn.shape
    dff = conv_w.shape[1]
    nblk = dff // tn
    grid = (nblk, m // tm)
    side = _side_casts(sides, grid)
    body = functools.partial(_ffn_up_kernel, tiles_per_seq=seq_len // tm)
    return pl.pallas_call(
        _with_side_casts(body, 5, 1, len(side)),
        grid=grid,
        in_specs=[pl.BlockSpec((tm, d), lambda j, i: (i, 0)),
                  pl.BlockSpec((d, tn), lambda j, i: (0, j)),
                  pl.BlockSpec((d, tn), lambda j, i: (0, nblk + j)),
                  pl.BlockSpec((CONV_W, tn), lambda j, i: (0, j)),
                  pl.BlockSpec((1, tn), lambda j, i: (0, j))]
                 + [sc.in_spec for sc in side],
        out_specs=[pl.BlockSpec((tm, tn), lambda j, i: (i, j))] + [sc.out_spec for sc in side],
        out_shape=[jax.ShapeDtypeStruct((m, dff), BF16)] + [sc.out_shape for sc in side],
        scratch_shapes=[pltpu.VMEM((SUBLANES, tn), F32)],
        compiler_params=_params("arbitrary", "arbitrary"),
        name="ffn_up_conv",
    )(xn, w_up, w_up, conv_w, conv_b.reshape(1, dff), *[sc.w for sc in side])


def kernel(x, positions, norm_mix, w_in, hgrn_lower_bounds, hgrn_out_norm, q_norm, k_norm,
           sg_ln_g, sg_ln_b, sg_w, sg_b, w_gate, w_branch, w_out, norm_ffn, w_up,
           ffn_conv_w, ffn_conv_b, w_down):
    bsz, s, d = x.shape
    m = bsz * s
    depth = w_in.shape[0]
    in_cols = w_in.shape[2]
    a_cols = 4 * BRANCH_WIDTH
    b_cols = len(B_PATTERNS) * 3 * BRANCH_WIDTH
    tables = _rope_tables(positions)
    w_branch2d = w_branch.reshape(depth, N_BRANCHES * BRANCH_WIDTH, d)
    w_in_l = w_in[0].astype(BF16)
    w_branch_l = w_branch2d[0].astype(BF16)
    xf = x.reshape(m, d)
    for l in range(depth):
        xn = rmsnorm(xf, norm_mix[l])
        proj, w_gate_l = matmul(xn, w_in_l, None, [(w_gate, l, CAST_BLOCK_WIDE)], tm=MM_TM, tn=MM_TN)
        proj3 = proj.reshape(bsz, s, in_cols)
        ys = hgrn2_mixer(proj3, hgrn_lower_bounds, hgrn_out_norm[l], l)
        ys = dilated_mixer(proj3, tables, q_norm[l], k_norm[l], a_cols, ys, 1)
        ys = spatial_gating(proj, sg_ln_g[l], sg_ln_b[l], sg_w[l], sg_b[l], a_cols + b_cols,
                            ys.reshape(N_BRANCHES, m, BRANCH_WIDTH), 2)
        merged, w_up_l, w_out_l = gated_merge(
            xn, ys, w_gate_l, w_branch_l.reshape(N_BRANCHES, BRANCH_WIDTH, d),
            [(w_up, l, CAST_BLOCK_WIDE), (w_out, l, CAST_BLOCK)], tm=MM_TM, tn=MERGE_TN)
        xf = matmul_residual(merged, w_out_l, None, xf, tm=MM_TM, tn=MM_TN, tk=d)
        xn = rmsnorm(xf, norm_ffn[l])
        sides = [(w_down, l, CAST_BLOCK)]
        if l + 1 < depth:
            sides += [(w_in, l + 1, CAST_BLOCK_WIDE), (w_branch2d, l + 1, CAST_BLOCK)]
        h, w_down_l, *next_layer = ffn_up(xn, w_up_l, ffn_conv_w[l], ffn_conv_b[l], s, sides,
                                          tm=MM_TM, tn=FFN_TN)
        if next_layer:
            w_in_l, w_branch_l = next_layer
        xf = matmul_residual(h, w_down_l, None, xf, tm=MM_TM, tn=MM_TN, tk=d)
    return xf.reshape(bsz, s, d)
```

```python
import functools
import math

import jax
import jax.numpy as jnp
from jax import lax
from jax.experimental import pallas as pl
from jax.experimental.pallas import tpu as pltpu

F32 = jnp.float32
BF16 = jnp.bfloat16

EPS = 1e-6
NEG_BIG = -1e30

LANES = 128
SUBLANES = 8
VMEM_LIMIT = 56 * 1024 * 1024

HEAD_DIM = 128
N_HEADS = 8
BRANCH_WIDTH = N_HEADS * HEAD_DIM
N_BRANCHES = 3

A_CHUNK = 64
A_SUB = 16
A_STEP_ROWS = 512
A_STEP_HEADS = 8
A_UNROLL = 8

B_PATTERNS = ((128, 1), (512, 4), (2048, 16))
B_BLOCK = 128
B_TILE = 2048
B_BATCH = 8
ROPE_THETA = 500000.0
ROT_DIM = HEAD_DIM // 4
ROT_HALF = ROT_DIM // 2

C_CHUNK = 128
C_STEP_ROWS = 512
CONV_W = 3

NORM_ROWS = 512
MM_TM = 1024
MM_TN = 1024
MERGE_TN = 512
FFN_TN = 512
CAST_BLOCK = (512, 512)
CAST_BLOCK_WIDE = (512, 1024)


def _params(*sem):
    return pltpu.CompilerParams(dimension_semantics=sem, vmem_limit_bytes=VMEM_LIMIT)


def _dot(a, b):
    return jnp.dot(a, b, preferred_element_type=F32)


def _dot_nt(a, b):
    return lax.dot_general(a, b, (((1,), (1,)), ((), ())), preferred_element_type=F32)


def _dot_tn(a, b):
    return lax.dot_general(a, b, (((0,), (0,)), ((), ())), preferred_element_type=F32)


def _rmsnorm_kernel(x_ref, g_ref, o_ref):
    x = x_ref[...]
    ms = jnp.mean(x * x, axis=-1, keepdims=True)
    o_ref[...] = (x * lax.rsqrt(ms + EPS) * g_ref[...]).astype(o_ref.dtype)


def rmsnorm(x, g, *, rows=NORM_ROWS):
    m, d = x.shape
    return pl.pallas_call(
        _rmsnorm_kernel,
        grid=(m // rows,),
        in_specs=[pl.BlockSpec((rows, d), lambda i: (i, 0)),
                  pl.BlockSpec((1, d), lambda i: (0, 0))],
        out_specs=pl.BlockSpec((rows, d), lambda i: (i, 0)),
        out_shape=jax.ShapeDtypeStruct((m, d), BF16),
        compiler_params=_params("parallel"),
        name="rmsnorm",
    )(x, g.reshape(1, d))


def _weight_spec(w, layer, block, index):
    if layer is None:
        return pl.BlockSpec(block, index)
    return pl.BlockSpec((None,) + block, lambda *ids: (layer,) + index(*ids))


class SideCast:
    def __init__(self, w, layer, block, grid):
        _, r, c = w.shape
        br, bc = block
        n_col = c // bc
        n_blocks = (r // br) * n_col
        assert r % br == 0 and c % bc == 0 and n_blocks <= math.prod(grid)

        def block_index(*ids):
            t = ids[0]
            for extent, i in zip(grid[1:], ids[1:]):
                t = t * extent + i
            t = jnp.minimum(t, n_blocks - 1)
            return t // n_col, t % n_col

        self.w = w
        self.in_spec = pl.BlockSpec((None, br, bc), lambda *ids: (layer,) + block_index(*ids))
        self.out_spec = pl.BlockSpec((br, bc), block_index)
        self.out_shape = jax.ShapeDtypeStruct((r, c), BF16)


def _side_casts(sides, grid):
    return [SideCast(w, layer, block, grid) for w, layer, block in sides]


def _with_side_casts(body, n_in, n_out, n_side):
    def wrapped(*refs):
        ins, rest = refs[:n_in], refs[n_in:]
        side_in, rest = rest[:n_side], rest[n_side:]
        outs, rest = rest[:n_out], rest[n_out:]
        side_out, scratch = rest[:n_side], rest[n_side:]

        def cast_sides():
            for src, dst in zip(side_in, side_out):
                dst[...] = src[...].astype(dst.dtype)

        body(*ins, *outs, *scratch, cast_sides=cast_sides)

    return wrapped


def _matmul_kernel(x_ref, w_ref, o_ref, *, cast_sides):
    cast_sides()
    o_ref[...] = _dot(x_ref[...], w_ref[...])


def matmul(x, w, layer, sides, *, tm, tn):
    m, k = x.shape
    n = w.shape[-1]
    grid = (m // tm, n // tn)
    side = _side_casts(sides, grid)
    return pl.pallas_call(
        _with_side_casts(_matmul_kernel, 2, 1, len(side)),
        grid=grid,
        in_specs=[pl.BlockSpec((tm, k), lambda i, j: (i, 0)),
                  _weight_spec(w, layer, (k, tn), lambda i, j: (0, j))]
                 + [sc.in_spec for sc in side],
        out_specs=[pl.BlockSpec((tm, tn), lambda i, j: (i, j))] + [sc.out_spec for sc in side],
        out_shape=[jax.ShapeDtypeStruct((m, n), F32)] + [sc.out_shape for sc in side],
        compiler_params=_params("arbitrary", "arbitrary"),
        name="proj_matmul",
    )(x, w, *[sc.w for sc in side])


def _matmul_res_kernel(x_ref, w_ref, r_ref, o_ref):
    k = pl.program_id(2)

    @pl.when(k == 0)
    def _():
        o_ref[...] = r_ref[...] + _dot(x_ref[...], w_ref[...])

    @pl.when(k > 0)
    def _():
        o_ref[...] += _dot(x_ref[...], w_ref[...])


def matmul_residual(x, w, layer, res, *, tm, tn, tk):
    m, k = x.shape
    n = w.shape[-1]
    return pl.pallas_call(
        _matmul_res_kernel,
        grid=(m // tm, n // tn, k // tk),
        in_specs=[pl.BlockSpec((tm, tk), lambda i, j, kk: (i, kk)),
                  _weight_spec(w, layer, (tk, tn), lambda i, j, kk: (kk, j)),
                  pl.BlockSpec((tm, tn), lambda i, j, kk: (i, j))],
        out_specs=pl.BlockSpec((tm, tn), lambda i, j, kk: (i, j)),
        out_shape=jax.ShapeDtypeStruct((m, n), F32),
        compiler_params=_params("parallel", "parallel", "arbitrary"),
        name="matmul_residual",
    )(x, w, res)


def _hgrn_kernel(lbp_ref, gain_ref, q_ref, f_ref, i_ref, g_ref, o_ref, st_ref, ck_s, v_s,
                 *, layer, n_chunks, n_heads):
    n = pl.program_id(2)

    @pl.when(n == 0)
    def _():
        st_ref[...] = jnp.zeros_like(st_ref)

    for other in range(1, N_BRANCHES):
        o_ref[other] = jnp.zeros(o_ref.shape[1:], o_ref.dtype)

    lbp = lbp_ref[...]
    e = jnp.exp(lbp - jnp.max(lbp, axis=0, keepdims=True))
    sm = e / jnp.sum(e, axis=0, keepdims=True)
    cs = sm[0:1]
    for li in range(1, layer + 1):
        cs = cs + sm[li:li + 1]
    lb_all = cs - sm[0:1]

    gain = gain_ref[...]
    n_sub = A_CHUNK // A_SUB
    row = lax.broadcasted_iota(jnp.int32, (A_CHUNK, A_CHUNK), 0)
    col = lax.broadcasted_iota(jnp.int32, (A_CHUNK, A_CHUNK), 1)
    tri = (row >= col).astype(F32)
    off_mask = (row // A_SUB) > (col // A_SUB)
    rblk = lax.broadcasted_iota(jnp.int32, (A_CHUNK, HEAD_DIM), 0) // A_SUB
    half_row = lax.broadcasted_iota(jnp.int32, (SUBLANES, HEAD_DIM), 0)
    scale = HEAD_DIM ** -0.5

    def head_chunk(hh, r0):
        sl = pl.ds(r0, A_CHUNK)
        hs = slice(hh * HEAD_DIM, (hh + 1) * HEAD_DIM)
        lb = lb_all[:, hs]
        q = q_ref[sl, hs] * scale
        v = i_ref[sl, hs]
        f = lb + (1.0 - lb) * jax.nn.sigmoid(f_ref[sl, hs])
        kk = 1.0 - f
        cum = jnp.dot(tri, jnp.log2(f), precision=lax.Precision.HIGHEST,
                      preferred_element_type=F32)
        ends = [cum[(j + 1) * A_SUB - 1:(j + 1) * A_SUB, :] for j in range(n_sub)]
        last = ends[-1]
        eblk = jnp.concatenate([jnp.broadcast_to(ej, (A_SUB, HEAD_DIM)) for ej in ends], axis=0)
        khat = kk * jnp.exp2(eblk - cum)
        st = st_ref[hh]

        o = _dot_nt((q * jnp.exp2(cum)).astype(BF16), st.astype(BF16))

        s_off = jnp.zeros((A_CHUNK, A_CHUNK), F32)
        for j in range(n_sub - 1):
            qj = q * jnp.exp2(jnp.minimum(cum - ends[j], 0.0))
            kj = jnp.where(rblk == j, khat, 0.0)
            s_off = s_off + _dot_nt(qj.astype(BF16), kj.astype(BF16))
        s_off = jnp.where(off_mask, s_off, 0.0)
        o = o + _dot(s_off.astype(BF16), v.astype(BF16))

        ck_s[hh] = cum - jnp.log2(kk)
        v_s[hh] = v
        diag = []
        for b in range(n_sub):
            base = b * A_SUB
            q_lo, q_hi = q[base:base + SUBLANES], q[base + SUBLANES:base + A_SUB]
            c_lo, c_hi = cum[base:base + SUBLANES], cum[base + SUBLANES:base + A_SUB]
            o_lo = jnp.zeros((SUBLANES, HEAD_DIM), F32)
            o_hi = jnp.zeros((SUBLANES, HEAD_DIM), F32)
            for s in range(A_SUB):
                r = base + s
                c_row = ck_s[hh, r:r + 1, :]
                v_row = v_s[hh, r:r + 1, :]
                if s < SUBLANES:
                    dec = jnp.exp2(jnp.where(half_row >= s, c_lo - c_row, NEG_BIG))
                    o_lo = o_lo + jnp.sum(q_lo * dec, axis=-1, keepdims=True) * v_row
                    dec = jnp.exp2(c_hi - c_row)
                else:
                    dec = jnp.exp2(jnp.where(half_row >= s - SUBLANES, c_hi - c_row, NEG_BIG))
                o_hi = o_hi + jnp.sum(q_hi * dec, axis=-1, keepdims=True) * v_row
            diag += [o_lo, o_hi]
        o = o + jnp.concatenate(diag, axis=0)

        ktil = khat * jnp.exp2(last - eblk)
        st_ref[hh] = st * jnp.exp2(last) + _dot_tn(v.astype(BF16), ktil.astype(BF16))

        ms = jnp.mean(o * o, axis=-1, keepdims=True)
        on = o * lax.rsqrt(ms + EPS) * gain
        g = g_ref[sl, hs]
        o_ref[0, sl, hs] = (on * (g * jax.nn.sigmoid(g))).astype(o_ref.dtype)

    def chunk(c, carry):
        r0 = pl.multiple_of(c * A_CHUNK, A_CHUNK)
        for hh in range(n_heads):
            head_chunk(hh, r0)
        return carry

    lax.fori_loop(0, n_chunks, chunk, 0, unroll=A_UNROLL)


def hgrn2_mixer(proj3, lower_bounds, out_gain, layer):
    bsz, s, _ = proj3.shape
    depth = lower_bounds.shape[0]
    ts = min(A_STEP_ROWS, s)
    hb = A_STEP_HEADS
    n_hg = N_HEADS // hb
    wide = hb * HEAD_DIM
    blk = lambda part: pl.BlockSpec((None, ts, wide), lambda b, h, n: (b, n, part * n_hg + h))
    return pl.pallas_call(
        functools.partial(_hgrn_kernel, layer=layer, n_chunks=ts // A_CHUNK, n_heads=hb),
        grid=(bsz, n_hg, s // ts),
        in_specs=[pl.BlockSpec((depth, wide), lambda b, h, n: (0, h)),
                  pl.BlockSpec((1, HEAD_DIM), lambda b, h, n: (0, 0)),
                  blk(0), blk(1), blk(2), blk(3)],
        out_specs=pl.BlockSpec((N_BRANCHES, None, ts, wide), lambda b, h, n: (0, b, n, h)),
        out_shape=jax.ShapeDtypeStruct((N_BRANCHES, bsz, s, BRANCH_WIDTH), BF16),
        scratch_shapes=[pltpu.VMEM((hb, HEAD_DIM, HEAD_DIM), F32)]
                       + [pltpu.VMEM((hb, A_CHUNK, HEAD_DIM), F32)] * 2,
        compiler_params=_params("parallel", "parallel", "arbitrary"),
        name="hgrn2",
    )(lower_bounds, out_gain.reshape(1, HEAD_DIM), proj3, proj3, proj3, proj3)


def _rope_tables(positions):
    inv = jnp.power(jnp.float32(ROPE_THETA), -jnp.arange(0, ROT_DIM, 2, dtype=F32) / ROT_DIM)
    ang = positions.astype(F32)[..., None] * inv
    cos = jnp.cos(ang)
    sin = jnp.sin(ang)
    pad = jnp.zeros(ang.shape[:-1] + (HEAD_DIM - ROT_DIM,), F32)
    cos_f = jnp.concatenate([cos, cos, pad + 1.0], axis=-1)
    sin_f = jnp.concatenate([sin, sin, pad], axis=-1)
    return cos_f, sin_f


def _dilated_kernel(qg_ref, kg_ref, cos_ref, sin_ref, *rest):
    n_grp = len(B_PATTERNS)
    qkv = rest[:3 * n_grp]
    y_ref = rest[3 * n_grp + 1]
    scratch = list(rest[3 * n_grp + 2:])
    qn_s, kn_s, kc, vc, num_s, den_s, m_s = (scratch[i * n_grp:(i + 1) * n_grp] for i in range(7))
    s_s, p_s, vb_s = scratch[7 * n_grp:]
    n = pl.program_id(2)

    @pl.when(n == 0)
    def _():
        for g in range(n_grp):
            kc[g][...] = jnp.zeros_like(kc[g])
            vc[g][...] = jnp.zeros_like(vc[g])

    q_gain = qg_ref[...] * (HEAD_DIM ** -0.5 * math.log2(math.e))
    k_gain = kg_ref[...]

    src = lax.broadcasted_iota(jnp.int32, (HEAD_DIM, HEAD_DIM), 0)
    dst = lax.broadcasted_iota(jnp.int32, (HEAD_DIM, HEAD_DIM), 1)
    rot = jnp.where(jnp.logical_and(dst < ROT_HALF, src == dst + ROT_HALF), -1.0,
                    jnp.where(jnp.logical_and(jnp.logical_and(dst >= ROT_HALF, dst < ROT_DIM),
                                              src == dst - ROT_HALF), 1.0, 0.0)).astype(BF16)
    rot2 = jnp.concatenate([rot, rot], axis=0)

    def norm_rope(x, gain, cos, sin):
        ms = jnp.mean(x * x, axis=-1, keepdims=True)
        xn = x * lax.rsqrt(ms + EPS) * gain
        hi = xn.astype(BF16)
        lo = (xn - hi.astype(F32)).astype(BF16)
        return xn * cos + _dot(jnp.concatenate([hi, lo], axis=1), rot2) * sin

    cos, sin = cos_ref[...], sin_ref[...]
    for g in range(n_grp):
        q_ref, k_ref = qkv[3 * g], qkv[3 * g + 1]
        qn_s[g][...] = norm_rope(q_ref[...], q_gain[g:g + 1, :], cos, sin)
        kn_s[g][...] = norm_rope(k_ref[...], k_gain[g:g + 1, :], cos, sin)

    qi = lax.broadcasted_iota(jnp.int32, (B_BLOCK, 2 * B_BLOCK), 0)
    ci = lax.broadcasted_iota(jnp.int32, (B_BLOCK, 2 * B_BLOCK), 1)
    cur_ok = jnp.logical_and(ci >= B_BLOCK, ci - B_BLOCK <= qi)
    prev_ok = jnp.logical_and(ci < B_BLOCK, ci >= qi)
    mask_inner = jnp.logical_or(cur_ok, prev_ok)
    mask_first = jnp.logical_or(cur_ok, jnp.logical_and(prev_ok, n > 0))
    ones = jnp.ones((2 * B_BLOCK, HEAD_DIM), BF16)

    for g, (win, dil) in enumerate(B_PATTERNS):
        v_ref = qkv[3 * g + 2]
        n_blk = B_TILE // (B_BLOCK * dil)
        blocks = [(r, j) for r in range(dil) for j in range(n_blk)]
        k_prev = v_prev = None
        for b0 in range(0, len(blocks), B_BATCH):
            batch = blocks[b0:b0 + B_BATCH]
            rows_of = []
            for i, (r, j) in enumerate(batch):
                start = j * B_BLOCK * dil + r
                rows = pl.ds(start, B_BLOCK) if dil == 1 else pl.ds(start, B_BLOCK, stride=dil)
                rows_of.append(rows)
                blk = slice(i * B_BLOCK, (i + 1) * B_BLOCK)
                if j == 0:
                    k_prev, v_prev = kc[g][r], vc[g][r]
                k_cur = kn_s[g][rows, :].astype(BF16)
                v_cur = v_ref[rows, :].astype(BF16)
                q = qn_s[g][rows, :].astype(BF16)
                s = _dot_nt(q, jnp.concatenate([k_prev, k_cur], axis=0))
                s_s[blk, :] = jnp.where(mask_first if j == 0 else mask_inner, s, NEG_BIG)
                vb_s[i, 0:B_BLOCK, :] = v_prev
                vb_s[i, B_BLOCK:2 * B_BLOCK, :] = v_cur
                if j == n_blk - 1:
                    kc[g][r] = k_cur
                    vc[g][r] = v_cur
                k_prev, v_prev = k_cur, v_cur
            live = slice(0, len(batch) * B_BLOCK)
            m = jnp.max(jnp.maximum(s_s[live, 0:B_BLOCK], s_s[live, B_BLOCK:2 * B_BLOCK]),
                        axis=-1, keepdims=True)
            p_s[live, :] = jnp.exp2(s_s[live, :] - m).astype(BF16)
            for i, rows in enumerate(rows_of):
                blk = slice(i * B_BLOCK, (i + 1) * B_BLOCK)
                pv = _dot(p_s[blk, :], jnp.concatenate([vb_s[i], ones], axis=1))
                num_s[g][rows, :] = pv[:, :HEAD_DIM]
                den_s[g][rows, :] = pv[:, HEAD_DIM:]
                m_s[g][rows, :] = jnp.broadcast_to(m[blk], (B_BLOCK, HEAD_DIM))

    ms = [ref[...] for ref in m_s]
    top = functools.reduce(jnp.maximum, ms)
    w = [jnp.exp2(x - top) for x in ms]
    num = functools.reduce(jnp.add, [wg * ref[...] for wg, ref in zip(w, num_s)])
    den = functools.reduce(jnp.add, [wg * ref[...] for wg, ref in zip(w, den_s)])
    y_ref[...] = (num / den).astype(y_ref.dtype)


def dilated_mixer(proj3, tables, q_gain, k_gain, col0, ys, branch):
    bsz, s, _ = proj3.shape
    n_grp = len(B_PATTERNS)
    assert s % B_TILE == 0

    def part(g, p):
        base = (col0 + (3 * g + p) * BRANCH_WIDTH) // HEAD_DIM
        return pl.BlockSpec((None, B_TILE, HEAD_DIM), lambda b, h, n: (b, n, base + h))

    tab = pl.BlockSpec((None, B_TILE, HEAD_DIM), lambda b, h, n: (b, n, 0))
    gain = pl.BlockSpec((n_grp, HEAD_DIM), lambda b, h, n: (0, 0))
    for win, dil in B_PATTERNS:
        assert win // dil == B_BLOCK and B_TILE % (B_BLOCK * dil) == 0
    token_f32 = [pltpu.VMEM((B_TILE, HEAD_DIM), F32)] * n_grp
    carry = [pltpu.VMEM((dil, B_BLOCK, HEAD_DIM), BF16) for _, dil in B_PATTERNS]
    scratch = (token_f32 + token_f32 + carry + carry + token_f32 + token_f32 + token_f32
               + [pltpu.VMEM((B_BATCH * B_BLOCK, 2 * B_BLOCK), F32),
                  pltpu.VMEM((B_BATCH * B_BLOCK, 2 * B_BLOCK), BF16),
                  pltpu.VMEM((B_BATCH, 2 * B_BLOCK, HEAD_DIM), BF16)])
    return pl.pallas_call(
        _dilated_kernel,
        grid=(bsz, N_HEADS, s // B_TILE),
        in_specs=[gain, gain, tab, tab]
                 + [part(g, p) for g in range(n_grp) for p in range(3)]
                 + [pl.BlockSpec(memory_space=pl.ANY)],
        out_specs=pl.BlockSpec((None, None, B_TILE, HEAD_DIM), lambda b, h, n: (branch, b, n, h)),
        out_shape=jax.ShapeDtypeStruct(ys.shape, ys.dtype),
        input_output_aliases={4 + 3 * n_grp: 0},
        scratch_shapes=scratch,
        compiler_params=_params("parallel", "parallel", "arbitrary"),
        name="dilated_attn",
    )(q_gain, k_gain, *tables, *([proj3] * (3 * n_grp)), ys)


def _spatial_kernel(u_ref, v_ref, lng_ref, lnb_ref, w_ref, b_ref, ys_ref, o_ref):
    inv_sqrt2 = 1.0 / math.sqrt(2.0)

    def gelu(x):
        return 0.5 * x * (1.0 + lax.erf(x * inv_sqrt2))

    row = lax.broadcasted_iota(jnp.int32, (C_CHUNK, C_CHUNK), 0)
    col = lax.broadcasted_iota(jnp.int32, (C_CHUNK, C_CHUNK), 1)
    causal = row >= col
    bias = b_ref[...]
    w_causal = [jnp.where(causal, w_ref[g], 0.0).astype(BF16) for g in range(N_HEADS)]
    for c in range(u_ref.shape[0] // C_CHUNK):
        rows = slice(c * C_CHUNK, (c + 1) * C_CHUNK)
        v = gelu(v_ref[rows, :])
        mu = jnp.mean(v, axis=-1, keepdims=True)
        vc = v - mu
        var = jnp.mean(vc * vc, axis=-1, keepdims=True)
        vn = vc * lax.rsqrt(var + EPS) * lng_ref[...] + lnb_ref[...]
        for g in range(N_HEADS):
            gs = slice(g * HEAD_DIM, (g + 1) * HEAD_DIM)
            mixed = _dot(w_causal[g], vn[:, gs].astype(BF16)) + bias[:, g:g + 1]
            o_ref[rows, gs] = (gelu(u_ref[rows, gs]) * mixed).astype(o_ref.dtype)


def spatial_gating(proj, ln_g, ln_b, w_s, b_s, col0, ys, branch):
    m, _ = proj.shape
    ublk = col0 // BRANCH_WIDTH
    rows = C_STEP_ROWS
    return pl.pallas_call(
        _spatial_kernel,
        grid=(m // rows,),
        in_specs=[pl.BlockSpec((rows, BRANCH_WIDTH), lambda i: (i, ublk)),
                  pl.BlockSpec((rows, BRANCH_WIDTH), lambda i: (i, ublk + 1)),
                  pl.BlockSpec((1, BRANCH_WIDTH), lambda i: (0, 0)),
                  pl.BlockSpec((1, BRANCH_WIDTH), lambda i: (0, 0)),
                  pl.BlockSpec((N_HEADS, C_CHUNK, C_CHUNK), lambda i: (0, 0, 0)),
                  pl.BlockSpec((C_CHUNK, N_HEADS), lambda i: (0, 0)),
                  pl.BlockSpec(memory_space=pl.ANY)],
        out_specs=pl.BlockSpec((None, rows, BRANCH_WIDTH), lambda i: (branch, i, 0)),
        out_shape=jax.ShapeDtypeStruct(ys.shape, ys.dtype),
        input_output_aliases={6: 0},
        compiler_params=_params("parallel"),
        name="spatial_gating",
    )(proj, proj, ln_g.reshape(1, -1), ln_b.reshape(1, -1), w_s, b_s.T, ys)


def _merge_kernel(xn_ref, y_ref, wg_ref, wb_ref, o_ref, acc_ref, *, cast_sides):
    @pl.when(pl.program_id(2) == 0)
    def _():
        acc_ref[...] = jnp.zeros_like(acc_ref)

    cast_sides()
    gate = jax.nn.sigmoid(_dot(xn_ref[...], wg_ref[...]))
    total = acc_ref[...] + gate * _dot(y_ref[...], wb_ref[...])
    acc_ref[...] = total
    o_ref[...] = total.astype(o_ref.dtype)


def gated_merge(xn, ys, w_gate, w_branch, sides, *, tm, tn):
    m, d = xn.shape
    w = ys.shape[2]
    nblk = d // tn
    grid = (m // tm, nblk, N_BRANCHES)
    side = _side_casts(sides, grid)
    return pl.pallas_call(
        _with_side_casts(_merge_kernel, 4, 1, len(side)),
        grid=grid,
        in_specs=[pl.BlockSpec((tm, d), lambda a, b, i: (a, 0)),
                  pl.BlockSpec((None, tm, w), lambda a, b, i: (i, a, 0)),
                  pl.BlockSpec((d, tn), lambda a, b, i: (0, i * nblk + b)),
                  pl.BlockSpec((None, w, tn), lambda a, b, i: (i, 0, b))]
                 + [sc.in_spec for sc in side],
        out_specs=[pl.BlockSpec((tm, tn), lambda a, b, i: (a, b))] + [sc.out_spec for sc in side],
        out_shape=[jax.ShapeDtypeStruct((m, d), BF16)] + [sc.out_shape for sc in side],
        scratch_shapes=[pltpu.VMEM((tm, tn), F32)],
        compiler_params=_params("arbitrary", "arbitrary", "arbitrary"),
        name="gated_merge",
    )(xn, ys, w_gate, w_branch, *[sc.w for sc in side])


def _gates_kernel(x_ref, w_ref, o_ref, *, cast_sides):
    cast_sides()
    o_ref[...] = jax.nn.sigmoid(_dot(x_ref[...], w_ref[...])).astype(o_ref.dtype)


def gates_matmul(xn, w_gate, sides, *, tm, tn):
    m, k = xn.shape
    n = w_gate.shape[-1]
    grid = (m // tm, n // tn)
    side = _side_casts(sides, grid)
    return pl.pallas_call(
        _with_side_casts(_gates_kernel, 2, 1, len(side)),
        grid=grid,
        in_specs=[pl.BlockSpec((tm, k), lambda i, j: (i, 0)),
                  pl.BlockSpec((k, tn), lambda i, j: (0, j))]
                 + [sc.in_spec for sc in side],
        out_specs=[pl.BlockSpec((tm, tn), lambda i, j: (i, j))] + [sc.out_spec for sc in side],
        out_shape=[jax.ShapeDtypeStruct((m, n), BF16)] + [sc.out_shape for sc in side],
        compiler_params=_params("arbitrary", "arbitrary"),
        name="gates_matmul",
    )(xn, w_gate, *[sc.w for sc in side])


def _branch_kernel(g_ref, y_ref, wb_ref, o_ref, acc_ref, *, cast_sides):
    @pl.when(pl.program_id(2) == 0)
    def _():
        acc_ref[...] = jnp.zeros_like(acc_ref)

    cast_sides()
    total = acc_ref[...] + g_ref[...].astype(F32) * _dot(y_ref[...], wb_ref[...])
    acc_ref[...] = total
    o_ref[...] = total.astype(o_ref.dtype)


def branch_combine(gates, ys, w_branch, sides, *, tm, tn):
    m = gates.shape[0]
    w = ys.shape[2]
    d = w_branch.shape[2]
    nblk = d // tn
    grid = (m // tm, nblk, N_BRANCHES)
    side = _side_casts(sides, grid)
    return pl.pallas_call(
        _with_side_casts(_branch_kernel, 3, 1, len(side)),
        grid=grid,
        in_specs=[pl.BlockSpec((tm, tn), lambda a, b, i: (a, i * nblk + b)),
                  pl.BlockSpec((None, tm, w), lambda a, b, i: (i, a, 0)),
                  pl.BlockSpec((None, w, tn), lambda a, b, i: (i, 0, b))]
                 + [sc.in_spec for sc in side],
        out_specs=[pl.BlockSpec((tm, tn), lambda a, b, i: (a, b))] + [sc.out_spec for sc in side],
        out_shape=[jax.ShapeDtypeStruct((m, d), BF16)] + [sc.out_shape for sc in side],
        scratch_shapes=[pltpu.VMEM((tm, tn), F32)],
        compiler_params=_params("arbitrary", "arbitrary", "arbitrary"),
        name="branch_combine",
    )(gates, ys, w_branch, *[sc.w for sc in side])


def _ffn_up_kernel(x_ref, wg_ref, wv_ref, cw_ref, cb_ref, o_ref, g_s, *, cast_sides, tiles_per_seq):
    i = pl.program_id(1)
    tm = x_ref.shape[0]

    @pl.when(i % tiles_per_seq == 0)
    def _():
        g_s[...] = jnp.zeros_like(g_s)

    cast_sides()
    x = x_ref[...]
    gate = _dot(x, wg_ref[...])
    val = _dot(x, wv_ref[...])
    prev = g_s[...]
    head = lax.broadcasted_iota(jnp.int32, prev.shape, 0)
    roll1 = pltpu.roll(gate, 1, 0)
    roll2 = pltpu.roll(gate, 2, 0)
    head1 = jnp.where(head == 0, prev[SUBLANES - 1:SUBLANES], roll1[:SUBLANES])
    head2 = jnp.where(head == 0, prev[SUBLANES - 2:SUBLANES - 1],
                      jnp.where(head == 1, prev[SUBLANES - 1:SUBLANES], roll2[:SUBLANES]))
    back1 = jnp.concatenate([head1, roll1[SUBLANES:]], axis=0)
    back2 = jnp.concatenate([head2, roll2[SUBLANES:]], axis=0)
    g_s[...] = gate[tm - SUBLANES:]
    cw = cw_ref[...]
    conv = cw[0:1] * back2 + cw[1:2] * back1 + cw[2:3] * gate + cb_ref[...]
    o_ref[...] = (conv * jax.nn.sigmoid(conv) * val).astype(o_ref.dtype)


def ffn_up(xn, w_up, conv_w, conv_b, seq_len, sides, *, tm, tn):
    m, d = xn.shape
    dff = conv_w.shape[1]
    nblk = dff // tn
    grid = (nblk, m // tm)
    side = _side_casts(sides, grid)
    body = functools.partial(_ffn_up_kernel, tiles_per_seq=seq_len // tm)
    return pl.pallas_call(
        _with_side_casts(body, 5, 1, len(side)),
        grid=grid,
        in_specs=[pl.BlockSpec((tm, d), lambda j, i: (i, 0)),
                  pl.BlockSpec((d, tn), lambda j, i: (0, j)),
                  pl.BlockSpec((d, tn), lambda j, i: (0, nblk + j)),
                  pl.BlockSpec((CONV_W, tn), lambda j, i: (0, j)),
                  pl.BlockSpec((1, tn), lambda j, i: (0, j))]
                 + [sc.in_spec for sc in side],
        out_specs=[pl.BlockSpec((tm, tn), lambda j, i: (i, j))] + [sc.out_spec for sc in side],
        out_shape=[jax.ShapeDtypeStruct((m, dff), BF16)] + [sc.out_shape for sc in side],
        scratch_shapes=[pltpu.VMEM((SUBLANES, tn), F32)],
        compiler_params=_params("arbitrary", "arbitrary"),
        name="ffn_up_conv",
    )(xn, w_up, w_up, conv_w, conv_b.reshape(1, dff), *[sc.w for sc in side])


def kernel(x, positions, norm_mix, w_in, hgrn_lower_bounds, hgrn_out_norm, q_norm, k_norm,
           sg_ln_g, sg_ln_b, sg_w, sg_b, w_gate, w_branch, w_out, norm_ffn, w_up,
           ffn_conv_w, ffn_conv_b, w_down):
    bsz, s, d = x.shape
    m = bsz * s
    depth = w_in.shape[0]
    in_cols = w_in.shape[2]
    a_cols = 4 * BRANCH_WIDTH
    b_cols = len(B_PATTERNS) * 3 * BRANCH_WIDTH
    tables = _rope_tables(positions)
    w_branch2d = w_branch.reshape(depth, N_BRANCHES * BRANCH_WIDTH, d)
    w_in_l = w_in[0].astype(BF16)
    w_branch_l = w_branch2d[0].astype(BF16)
    xf = x.reshape(m, d)
    for l in range(depth):
        xn = rmsnorm(xf, norm_mix[l])
        proj, w_gate_l = matmul(xn, w_in_l, None, [(w_gate, l, CAST_BLOCK_WIDE)], tm=MM_TM, tn=MM_TN)
        proj3 = proj.reshape(bsz, s, in_cols)
        ys = hgrn2_mixer(proj3, hgrn_lower_bounds, hgrn_out_norm[l], l)
        ys = dilated_mixer(proj3, tables, q_norm[l], k_norm[l], a_cols, ys, 1)
        ys = spatial_gating(proj, sg_ln_g[l], sg_ln_b[l], sg_w[l], sg_b[l], a_cols + b_cols,
                            ys.reshape(N_BRANCHES, m, BRANCH_WIDTH), 2)
        gates, w_out_l = gates_matmul(xn, w_gate_l, [(w_out, l, CAST_BLOCK)], tm=MM_TM, tn=MM_TN)
        merged, w_up_l = branch_combine(
            gates, ys, w_branch_l.reshape(N_BRANCHES, BRANCH_WIDTH, d),
            [(w_up, l, (1024, 1024))], tm=MM_TM, tn=MM_TN)
        xf = matmul_residual(merged, w_out_l, None, xf, tm=MM_TM, tn=MM_TN, tk=d)
        xn = rmsnorm(xf, norm_ffn[l])
        sides = [(w_down, l, CAST_BLOCK)]
        if l + 1 < depth:
            sides += [(w_in, l + 1, CAST_BLOCK_WIDE), (w_branch2d, l + 1, CAST_BLOCK)]
        h, w_down_l, *next_layer = ffn_up(xn, w_up_l, ffn_conv_w[l], ffn_conv_b[l], s, sides,
                                          tm=MM_TM, tn=FFN_TN)
        if next_layer:
            w_in_l, w_branch_l = next_layer
        xf = matmul_residual(h, w_down_l, None, xf, tm=MM_TM, tn=MM_TN, tk=d)
    return xf.reshape(bsz, s, d)
```
